```python
import math
import jax, jax.numpy as jnp
from jax import lax
import numpy as np

D_MODEL = 1024
BATCH = 4
SEQ = 4096
DEPTH = 4
DEC_BATCH = 32
DEC_SEQ = 1
PAST_LEN = 8192
PAGE_SIZE = 128

N_MIXERS = 3
N_POOL_LAYERS = (DEPTH + 2) // 3
N_SWA_LAYERS = (DEPTH + 1) // 3
N_CONV_LAYERS = DEPTH // 3
RMS_EPS = 1e-6
LN_EPS = 1e-5
NEG_INF = -1e30
POOL_WINDOWS = (2, 4, 8, 16)
N_POOL_GROUPS = 4
POOL_GROUP_DIM = D_MODEL // N_POOL_GROUPS
POOL_HIST = max(POOL_WINDOWS) - 1
SWA_CONFIGS = ((128, 1), (512, 4), (2048, 16))
N_GROUPS = 3
HEAD_DIM = 64
HEADS_PER_GROUP = D_MODEL // HEAD_DIM
SPAN = 128
BLK = 128
N_BUCKETS = 32
MAX_DISTANCE = 2048
CONV_WIDTH = 31
CONV_HIST = CONV_WIDTH - 1
D_FF = -(-8 * D_MODEL // (3 * 256)) * 256

kernel_name = "hybrid_pool_dilattn_conv_decoder_step"


def _t5_bucket(dist):
    max_exact = N_BUCKETS // 2
    d = np.maximum(np.asarray(dist), 0)
    large = max_exact + (np.log(np.maximum(d, max_exact) / max_exact) / math.log(MAX_DISTANCE / max_exact)
                         * (N_BUCKETS - max_exact)).astype(np.int64)
    large = np.minimum(large, N_BUCKETS - 1)
    return np.where(d < max_exact, d, large).astype(np.int32)


def rmsnorm(x, g):
    xf = x.astype(jnp.float32)
    y = xf * lax.rsqrt(jnp.mean(xf * xf, axis=-1, keepdims=True) + RMS_EPS)
    return (y * g.astype(jnp.float32)).astype(x.dtype)


def swiglu(h, w_in, w_out):
    gu = h @ w_in
    return (jax.nn.silu(gu[..., :D_FF]) * gu[..., D_FF:]) @ w_out


def pool_mixer(h, hist, n_past, w_grp, scale):
    B, T, D = h.shape
    xh = jnp.concatenate([hist.astype(h.dtype), h], axis=1)
    cs = jnp.pad(jnp.cumsum(xh.astype(jnp.float32), axis=1), ((0, 0), (1, 0), (0, 0)))
    pos1 = n_past + 1 + jnp.arange(T)
    hf = h.astype(jnp.float32)
    outs = []
    for g, w in enumerate(POOL_WINDOWS):
        sl = slice(g * POOL_GROUP_DIM, (g + 1) * POOL_GROUP_DIM)
        win = cs[:, POOL_HIST + 1:POOL_HIST + 1 + T, sl] - cs[:, POOL_HIST + 1 - w:POOL_HIST + 1 - w + T, sl]
        cnt = jnp.minimum(pos1, w).astype(jnp.float32)
        outs.append(win / cnt[None, :, None] - hf[..., sl])
    d = jnp.stack(outs, axis=2)
    y = jnp.einsum('btgc,gce->btge', d, w_grp.astype(jnp.float32)).reshape(B, T, D)
    y = y * scale.astype(jnp.float32)
    return y.astype(h.dtype), xh[:, -POOL_HIST:]


def conv_mixer(h, hist, w_in, b_in, w_dw, b_dw, ln_g, ln_b, w_out, b_out):
    a = h @ w_in + b_in
    u = a[..., :D_MODEL] * jax.nn.sigmoid(a[..., D_MODEL:])
    uh = jnp.concatenate([hist.astype(u.dtype), u], axis=1)
    c = lax.conv_general_dilated(uh, w_dw[:, None, :].astype(uh.dtype), window_strides=(1,), padding='VALID',
                                 dimension_numbers=('NWC', 'WIO', 'NWC'), feature_group_count=D_MODEL) + b_dw
    cf = c.astype(jnp.float32)
    mu = jnp.mean(cf, axis=-1, keepdims=True)
    var = jnp.mean(jnp.square(cf - mu), axis=-1, keepdims=True)
    z = (cf - mu) * lax.rsqrt(var + LN_EPS) * ln_g.astype(jnp.float32) + ln_b.astype(jnp.float32)
    y = jax.nn.silu(z).astype(h.dtype) @ w_out + b_out
    return y, uh[:, -CONV_HIST:]


def _dilated_attn_prompt(q, k, v, rel_bias, g):
    B, S, H, Dh = q.shape
    _, dil = SWA_CONFIGS[g]
    L = S // dil
    nblk = -(-L // BLK)
    Lp = nblk * BLK

    def to_sub(t):
        t = t.reshape(B, L, dil, H, Dh).transpose(0, 2, 1, 3, 4)
        return jnp.pad(t, ((0, 0), (0, 0), (0, Lp - L), (0, 0), (0, 0)))

    def band(t):
        t = jnp.pad(t, ((0, 0), (0, 0), (BLK, 0), (0, 0), (0, 0))).reshape(B, dil, nblk + 1, BLK, H, Dh)
        return jnp.concatenate([t[:, :, :-1], t[:, :, 1:]], axis=3)

    def from_sub(t):
        t = t.reshape((B, dil, Lp) + t.shape[4:])[:, :, :L]
        t = jnp.moveaxis(t, 1, 2)
        return t.reshape((B, S) + t.shape[3:])

    qb = to_sub(q).reshape(B, dil, nblk, BLK, H, Dh)
    kb = band(to_sub(k))
    vb = band(to_sub(v))
    qi = np.arange(BLK)[:, None]
    kj = np.arange(2 * BLK)[None, :]
    delta = qi + BLK - kj
    band_ok = (delta >= 0) & (delta <= SPAN)
    first = (np.arange(nblk) == 0)[:, None, None] & (kj < BLK)[None]
    valid = band_ok[None] & ~first
    bucket = _t5_bucket(np.clip(delta, 0, SPAN) * dil)
    bias = jnp.transpose(rel_bias[bucket, g], (2, 0, 1)).astype(jnp.float32)
    logits = jnp.einsum('brnqhd,brnkhd->brnhqk', qb, kb, preferred_element_type=jnp.float32) * (Dh ** -0.5) + bias
    logits = jnp.where(valid[:, None], logits, NEG_INF)
    m = jnp.max(logits, axis=-1, keepdims=True)
    p = jnp.exp(logits - m)
    s = jnp.sum(p, axis=-1, keepdims=True)
    o = jnp.einsum('brnhqk,brnkhd->brnqhd', p, vb.astype(jnp.float32)) / jnp.transpose(s, (0, 1, 2, 4, 3, 5))
    lse = jnp.transpose((m + jnp.log(s))[..., 0], (0, 1, 2, 4, 3))
    return from_sub(o), from_sub(lse)


def _dilated_attn_sample(q, kbuf, vbuf, rel_bias, g):
    _, dil = SWA_CONFIGS[g]
    T = q.shape[1]
    Dh = q.shape[-1]
    n_cache = kbuf.shape[1] - T
    steps = np.arange(SPAN + 1)
    idx = n_cache + np.arange(T)[:, None] - steps[None, :] * dil
    valid = idx >= 0
    idx = np.maximum(idx, 0)
    kg = kbuf[:, idx]
    vg = vbuf[:, idx]
    bias = jnp.transpose(rel_bias[_t5_bucket(steps * dil), g]).astype(jnp.float32)
    logits = jnp.einsum('bthd,btkhd->bhtk', q, kg, preferred_element_type=jnp.float32) * (Dh ** -0.5) + bias[:, None, :]
    logits = jnp.where(valid[None, None], logits, NEG_INF)
    m = jnp.max(logits, axis=-1, keepdims=True)
    p = jnp.exp(logits - m)
    s = jnp.sum(p, axis=-1, keepdims=True)
    o = jnp.einsum('bhtk,btkhd->bthd', p, vg.astype(jnp.float32)) / jnp.transpose(s, (0, 2, 1, 3))
    lse = jnp.transpose((m + jnp.log(s))[..., 0], (0, 2, 1))
    return o, lse


def _merge_groups(outs, lses):
    o = jnp.stack(outs, axis=0)
    wgt = jax.nn.softmax(jnp.stack(lses, axis=0), axis=0)
    return jnp.einsum('gbth,gbthd->bthd', wgt, o)


def swa_prompt(h, w_qkv, w_o, rel_bias):
    B, T, D = h.shape
    qkv = (h @ w_qkv).reshape(B, T, N_GROUPS, 3, HEADS_PER_GROUP, HEAD_DIM)
    outs, lses, kv_new = [], [], []
    for g, (win, _) in enumerate(SWA_CONFIGS):
        o, l = _dilated_attn_prompt(qkv[:, :, g, 0], qkv[:, :, g, 1], qkv[:, :, g, 2], rel_bias, g)
        outs.append(o)
        lses.append(l)
        keep = min(win, T)
        kv_new.append(qkv[:, T - keep:, g, 1:3])
    y = _merge_groups(outs, lses).reshape(B, T, D).astype(h.dtype) @ w_o
    return y, kv_new


def swa_sample(h, caches, w_qkv, w_o, rel_bias):
    B, T, D = h.shape
    qkv = (h @ w_qkv).reshape(B, T, N_GROUPS, 3, HEADS_PER_GROUP, HEAD_DIM)
    outs, lses, kv_new = [], [], []
    for g in range(N_GROUPS):
        cache = caches[g].astype(qkv.dtype)
        kbuf = jnp.concatenate([cache[:, :, 0], qkv[:, :, g, 1]], axis=1)
        vbuf = jnp.concatenate([cache[:, :, 1], qkv[:, :, g, 2]], axis=1)
        o, l = _dilated_attn_sample(qkv[:, :, g, 0], kbuf, vbuf, rel_bias, g)
        outs.append(o)
        lses.append(l)
        kv_new.append(qkv[:, :, g, 1:3])
    y = _merge_groups(outs, lses).reshape(B, T, D).astype(h.dtype) @ w_o
    return y, kv_new


def setup_inputs(seed: int = 0) -> dict:
    key = jax.random.key(seed)
    ks = jax.random.split(key, 24)
    f32 = jnp.float32

    def nrm(k, shape, scale=1.0):
        return jax.random.normal(k, shape, f32) * scale

    D = D_MODEL
    cache_shape = lambda w: (N_SWA_LAYERS, DEC_BATCH, min(w, PAST_LEN), 2, HEADS_PER_GROUP, HEAD_DIM)
    return {
        "x_prompt": nrm(ks[0], (BATCH, SEQ, D)),
        "x_sample": nrm(ks[1], (DEC_BATCH, DEC_SEQ, D)),
        "state_pool": nrm(ks[2], (N_POOL_LAYERS, DEC_BATCH, POOL_HIST, D)),
        "cache_swa_g0": nrm(ks[3], cache_shape(SWA_CONFIGS[0][0])),
        "cache_swa_g1": nrm(ks[4], cache_shape(SWA_CONFIGS[1][0])),
        "cache_swa_g2": nrm(ks[5], cache_shape(SWA_CONFIGS[2][0])),
        "state_conv": nrm(ks[6], (N_CONV_LAYERS, DEC_BATCH, CONV_HIST, D), 0.5),
        "norm_g": 1.0 + nrm(ks[7], (DEPTH, 4, D), 0.1),
        "w_ffn_in": nrm(ks[8], (DEPTH, D, 2 * D_FF), D ** -0.5),
        "w_ffn_out": nrm(ks[9], (DEPTH, D_FF, D), D_FF ** -0.5),
        "pool_w": nrm(ks[10], (N_POOL_LAYERS, N_POOL_GROUPS, POOL_GROUP_DIM, POOL_GROUP_DIM), POOL_GROUP_DIM ** -0.5),
        "pool_scale": 1.0 + nrm(ks[11], (N_POOL_LAYERS, D), 0.1),
        "w_qkv": nrm(ks[12], (N_SWA_LAYERS, D, N_GROUPS * 3 * D), D ** -0.5),
        "w_o": nrm(ks[13], (N_SWA_LAYERS, D, D), D ** -0.5),
        "rel_bias": nrm(ks[14], (N_BUCKETS, N_GROUPS, HEADS_PER_GROUP), 0.5),
        "conv_w_in": nrm(ks[15], (N_CONV_LAYERS, D, 2 * D), D ** -0.5),
        "conv_b_in": nrm(ks[16], (N_CONV_LAYERS, 2 * D), 0.02),
        "conv_w_dw": nrm(ks[17], (N_CONV_LAYERS, CONV_WIDTH, D), CONV_WIDTH ** -0.5),
        "conv_b_dw": nrm(ks[18], (N_CONV_LAYERS, D), 0.02),
        "conv_ln_g": 1.0 + nrm(ks[19], (N_CONV_LAYERS, D), 0.1),
        "conv_ln_b": nrm(ks[20], (N_CONV_LAYERS, D), 0.02),
        "conv_w_out": nrm(ks[21], (N_CONV_LAYERS, D, D), D ** -0.5),
        "conv_b_out": nrm(ks[22], (N_CONV_LAYERS, D), 0.02),
    }


def reference(x_prompt, x_sample, state_pool, cache_swa_g0, cache_swa_g1, cache_swa_g2, state_conv,
              norm_g, w_ffn_in, w_ffn_out, pool_w, pool_scale, w_qkv, w_o, rel_bias,
              conv_w_in, conv_b_in, conv_w_dw, conv_b_dw, conv_ln_g, conv_ln_b, conv_w_out, conv_b_out):
    swa_caches = (cache_swa_g0, cache_swa_g1, cache_swa_g2)
    xp, xs = x_prompt, x_sample
    Bp = xp.shape[0]
    pool_p, pool_s, conv_p, conv_s = [], [], [], []
    swa_p = [[] for _ in range(N_GROUPS)]
    swa_s = [[] for _ in range(N_GROUPS)]
    for i in range(DEPTH):
        kind, j = i % N_MIXERS, i // N_MIXERS
        hp = rmsnorm(xp, norm_g[i, 0])
        hs = rmsnorm(xs, norm_g[i, 0])
        if kind == 0:
            mp, sp = pool_mixer(hp, jnp.zeros((Bp, POOL_HIST, D_MODEL), hp.dtype), 0, pool_w[j], pool_scale[j])
            ms, ss = pool_mixer(hs, state_pool[j], PAST_LEN, pool_w[j], pool_scale[j])
            pool_p.append(sp)
            pool_s.append(ss)
        elif kind == 1:
            mp, kvp = swa_prompt(hp, w_qkv[j], w_o[j], rel_bias)
            ms, kvs = swa_sample(hs, [c[j] for c in swa_caches], w_qkv[j], w_o[j], rel_bias)
            for g in range(N_GROUPS):
                swa_p[g].append(kvp[g])
                swa_s[g].append(kvs[g])
        else:
            cw = (conv_w_in[j], conv_b_in[j], conv_w_dw[j], conv_b_dw[j], conv_ln_g[j], conv_ln_b[j],
                  conv_w_out[j], conv_b_out[j])
            mp, sp = conv_mixer(hp, jnp.zeros((Bp, CONV_HIST, D_MODEL), hp.dtype), *cw)
            ms, ss = conv_mixer(hs, state_conv[j], *cw)
            conv_p.append(sp)
            conv_s.append(ss)
        xp = xp + rmsnorm(mp, norm_g[i, 1])
        xs = xs + rmsnorm(ms, norm_g[i, 1])
        xp = xp + rmsnorm(swiglu(rmsnorm(xp, norm_g[i, 2]), w_ffn_in[i], w_ffn_out[i]), norm_g[i, 3])
        xs = xs + rmsnorm(swiglu(rmsnorm(xs, norm_g[i, 2]), w_ffn_in[i], w_ffn_out[i]), norm_g[i, 3])
    return (xp, xs,
            jnp.stack(pool_p), jnp.stack(pool_s),
            jnp.stack(swa_p[0]), jnp.stack(swa_p[1]), jnp.stack(swa_p[2]),
            jnp.stack(swa_s[0]), jnp.stack(swa_s[1]), jnp.stack(swa_s[2]),
            jnp.stack(conv_p), jnp.stack(conv_s))
```

```python
import functools
import math

import numpy as np
import jax
import jax.numpy as jnp
from jax import lax
from jax.experimental import pallas as pl
from jax.experimental.pallas import tpu as pltpu

F32 = jnp.float32
BF16 = jnp.bfloat16

RMS_EPS = 1e-6
LN_EPS = 1e-5
NEG_INF = -1e30
PAST_LEN = 8192
POOL_WINDOWS = (2, 4, 8, 16)
POOL_HIST = max(POOL_WINDOWS) - 1
SWA_CONFIGS = ((128, 1), (512, 4), (2048, 16))
N_GROUPS = len(SWA_CONFIGS)
SPAN = 128
BLK = 128
HEAD_DIM = 64
N_BUCKETS = 32
MAX_DISTANCE = 2048
CONV_WIDTH = 31
CONV_HIST = CONV_WIDTH - 1

LANES = 128
SUBLANES = 8
MXU_DIM = 256
VMEM_LIMIT_BYTES = 56 * 1024 * 1024

POOL_HALO = 16
CONV_HALO = 32
HEAD_PAIR = 2 * HEAD_DIM


def _params(*sem):
    return pltpu.CompilerParams(dimension_semantics=sem, vmem_limit_bytes=VMEM_LIMIT_BYTES)


def _resident(shape):
    nd = len(shape)
    return pl.BlockSpec(shape, lambda *_: (0,) * nd, pipeline_mode=pl.Buffered(1))


def _rms(x, g):
    return x * lax.rsqrt(jnp.mean(x * x, axis=-1, keepdims=True) + RMS_EPS) * g


def _dot(a, b):
    return jnp.dot(a, b, preferred_element_type=F32)


def _sel_dot(x, sel):
    hi = x.astype(BF16)
    r1 = x - hi.astype(F32)
    mid = r1.astype(BF16)
    lo = (r1 - mid.astype(F32)).astype(BF16)
    return _dot(hi, sel) + _dot(mid, sel) + _dot(lo, sel)


def _t5_bucket(dist):
    max_exact = N_BUCKETS // 2
    d = np.maximum(np.asarray(dist), 0)
    large = max_exact + (np.log(np.maximum(d, max_exact) / max_exact) / math.log(MAX_DISTANCE / max_exact)
                         * (N_BUCKETS - max_exact)).astype(np.int64)
    large = np.minimum(large, N_BUCKETS - 1)
    return np.where(d < max_exact, d, large).astype(np.int32)


def _ffn_body(x_ref, g_pre_ref, w_in_ref, w_out_ref, g_post_ref, o_ref, *, d_ff, chunks):
    x = x_ref[...]
    h = _rms(x, g_pre_ref[...]).astype(BF16)
    acc = None
    for c0, cw in chunks:
        gate = _dot(h, w_in_ref[:, c0:c0 + cw])
        up = _dot(h, w_in_ref[:, d_ff + c0:d_ff + c0 + cw])
        act = (gate * jax.nn.sigmoid(gate) * up).astype(BF16)
        part = _dot(act, w_out_ref[c0:c0 + cw, :])
        acc = part if acc is None else acc + part
    o_ref[...] = x + _rms(acc, g_post_ref[...])


def _ff_chunks(d_ff):
    assert d_ff % MXU_DIM == 0
    n = d_ff // MXU_DIM
    first = (n + 1) // 2 * MXU_DIM
    return ((0, first), (first, d_ff - first)) if d_ff > first else ((0, d_ff),)


def _ffn(x, g_pre, w_in, w_out, g_post, tm):
    n, d = x.shape
    d_ff = w_out.shape[0]
    assert n % tm == 0
    row = pl.BlockSpec((tm, d), lambda i: (i, 0))
    return pl.pallas_call(
        functools.partial(_ffn_body, d_ff=d_ff, chunks=_ff_chunks(d_ff)),
        grid=(n // tm,),
        in_specs=[row, _resident((1, d)), _resident(w_in.shape), _resident(w_out.shape), _resident((1, d))],
        out_specs=row,
        out_shape=jax.ShapeDtypeStruct((n, d), F32),
        compiler_params=_params("parallel"),
        name="ffn",
    )(x, g_pre, w_in, w_out, g_post)


def _pool_groups(h, window_sum, cnt_of, pw_ref, ps_ref):
    gd = h.shape[-1] // len(POOL_WINDOWS)
    outs = []
    for gi, w in enumerate(POOL_WINDOWS):
        c0 = gi * gd
        hg = h[:, c0:c0 + gd]
        dg = window_sum(gi, w, c0, gd, hg) / cnt_of(w) - hg
        outs.append(_dot(dg.astype(BF16), pw_ref[gi]))
    return jnp.concatenate(outs, axis=1) * ps_ref[...]


def _pool_prompt_body(x_ref, xprev_ref, hist_ref, g_pre_ref, pw_ref, ps_ref, g_post_ref,
                      o_ref, tail_ref, buf, *, ts, n_past):
    t = pl.program_id(1)
    x = x_ref[0]
    g_pre = g_pre_ref[...]
    h = _rms(x, g_pre)
    halo = jnp.where(t == 0, hist_ref[0], _rms(xprev_ref[0], g_pre))
    buf[0:POOL_HALO, :] = halo
    buf[POOL_HALO:POOL_HALO + ts, :] = h
    tail_ref[0] = buf[ts:ts + POOL_HALO, :]
    pos1 = n_past + 1 + t * ts + lax.broadcasted_iota(jnp.int32, (ts, 1), 0)

    def window_sum(gi, w, c0, gd, hg):
        win = hg
        for k in range(1, w):
            win = win + buf[POOL_HALO - k:POOL_HALO - k + ts, c0:c0 + gd]
        return win

    y = _pool_groups(h, window_sum, lambda w: jnp.minimum(pos1, w).astype(F32), pw_ref, ps_ref)
    o_ref[0] = x + _rms(y, g_post_ref[...])


def _pool_prompt(x, hist16, g_pre, pw, ps, g_post, ts, n_past):
    b, t, d = x.shape
    assert t % ts == 0 and ts % POOL_HALO == 0
    per = ts // POOL_HALO
    tile = pl.BlockSpec((1, ts, d), lambda i, j: (i, j, 0))
    halo = pl.BlockSpec((1, POOL_HALO, d), lambda i, j: (i, jnp.maximum(j * per - 1, 0), 0))
    first = pl.BlockSpec((1, POOL_HALO, d), lambda i, j: (i, 0, 0))
    return pl.pallas_call(
        functools.partial(_pool_prompt_body, ts=ts, n_past=n_past),
        grid=(b, t // ts),
        in_specs=[tile, halo, first, _resident((1, d)), _resident(pw.shape), _resident((1, d)), _resident((1, d))],
        out_specs=[tile, first],
        out_shape=[jax.ShapeDtypeStruct((b, t, d), F32), jax.ShapeDtypeStruct((b, POOL_HALO, d), F32)],
        scratch_shapes=[pltpu.VMEM((POOL_HALO + ts, d), F32)],
        compiler_params=_params("parallel", "arbitrary"),
        name="pool_prompt",
    )(x, x, hist16, g_pre, pw, ps, g_post)


def _pool_sample_body(x_ref, hist_ref, g_pre_ref, pw_ref, ps_ref, g_post_ref, o_ref, h_ref, *, n_past):
    x = x_ref[...]
    h = _rms(x, g_pre_ref[...])
    h_ref[...] = h

    def window_sum(gi, w, c0, gd, hg):
        win = hg
        for k in range(1, w):
            win = win + hist_ref[POOL_HIST - k, :, c0:c0 + gd]
        return win

    y = _pool_groups(h, window_sum, lambda w: float(min(n_past + 1, w)), pw_ref, ps_ref)
    o_ref[...] = x + _rms(y, g_post_ref[...])


def _pool_sample(x, hist_t, g_pre, pw, ps, g_post, n_past):
    bd, d = x.shape
    return pl.pallas_call(
        functools.partial(_pool_sample_body, n_past=n_past),
        out_shape=[jax.ShapeDtypeStruct((bd, d), F32), jax.ShapeDtypeStruct((bd, d), F32)],
        compiler_params=pltpu.CompilerParams(vmem_limit_bytes=VMEM_LIMIT_BYTES),
        name="pool_sample",
    )(x, hist_t, g_pre, pw, ps, g_post)


def _glu(h, w_in_ref, b_in_ref):
    d = h.shape[-1]
    a = _dot(h, w_in_ref[...]) + b_in_ref[...]
    return a[:, :d] * jax.nn.sigmoid(a[:, d:])


def _conv_tail(c, x, ln_g_ref, ln_b_ref, w_out_ref, b_out_ref, g_post_ref):
    mu = jnp.mean(c, axis=-1, keepdims=True)
    cc = c - mu
    var = jnp.mean(cc * cc, axis=-1, keepdims=True)
    z = cc * lax.rsqrt(var + LN_EPS) * ln_g_ref[...] + ln_b_ref[...]
    y = _dot((z * jax.nn.sigmoid(z)).astype(BF16), w_out_ref[...]) + b_out_ref[...]
    return x + _rms(y, g_post_ref[...])


def _conv_prompt_body(x_ref, hist_ref, g_pre_ref, w_in_ref, b_in_ref, w_dw_ref, b_dw_ref, ln_g_ref, ln_b_ref,
                      w_out_ref, b_out_ref, g_post_ref, o_ref, tail_ref, ubuf, cbuf, *, ts, rc):
    t = pl.program_id(1)
    d = x_ref.shape[-1]

    @pl.when(t == 0)
    def _():
        ubuf[0:CONV_HALO, :] = hist_ref[0]

    @pl.when(t > 0)
    def _():
        ubuf[0:CONV_HALO, :] = ubuf[ts:ts + CONV_HALO, :]

    x = x_ref[0]
    h = _rms(x, g_pre_ref[...]).astype(BF16)
    ubuf[CONV_HALO:CONV_HALO + ts, :] = _glu(h, w_in_ref, b_in_ref)
    tail_ref[0] = ubuf[ts:ts + CONV_HALO, :]

    lead = CONV_HALO - CONV_HIST

    def chunk(i, carry):
        base = pl.multiple_of(i * rc, rc)
        win = ubuf[pl.ds(base, rc + CONV_HALO), :]
        acc = jnp.broadcast_to(b_dw_ref[...], (rc, d))
        for phase in range(SUBLANES):
            offs = [o for o in range(lead, lead + CONV_WIDTH) if o % SUBLANES == phase]
            shifted = win[phase:phase + offs[-1] - phase + rc, :]
            for o in offs:
                acc = acc + shifted[o - phase:o - phase + rc, :] * w_dw_ref[o - lead:o - lead + 1, :]
        cbuf[pl.ds(base, rc), :] = acc
        return carry

    lax.fori_loop(0, ts // rc, chunk, 0)
    o_ref[0] = _conv_tail(cbuf[...], x, ln_g_ref, ln_b_ref, w_out_ref, b_out_ref, g_post_ref)


def _conv_prompt(x, hist32, g_pre, w_in, b_in, w_dw, b_dw, ln_g, ln_b, w_out, b_out, g_post, ts, rc):
    b, t, d = x.shape
    assert t % ts == 0 and ts % rc == 0 and ts >= CONV_HALO
    tile = pl.BlockSpec((1, ts, d), lambda i, j: (i, j, 0))
    first = pl.BlockSpec((1, CONV_HALO, d), lambda i, j: (i, 0, 0))
    vec = _resident((1, d))
    return pl.pallas_call(
        functools.partial(_conv_prompt_body, ts=ts, rc=rc),
        grid=(b, t // ts),
        in_specs=[tile, first, vec, _resident(w_in.shape), _resident(b_in.shape), _resident(w_dw.shape), vec, vec,
                  vec, _resident(w_out.shape), vec, vec],
        out_specs=[tile, first],
        out_shape=[jax.ShapeDtypeStruct((b, t, d), F32), jax.ShapeDtypeStruct((b, CONV_HALO, d), F32)],
        scratch_shapes=[pltpu.VMEM((CONV_HALO + ts, d), F32), pltpu.VMEM((ts, d), F32)],
        compiler_params=_params("parallel", "arbitrary"),
        name="conv_prompt",
    )(x, hist32, g_pre, w_in, b_in, w_dw, b_dw, ln_g, ln_b, w_out, b_out, g_post)


def _conv_sample_body(x_ref, hist_ref, g_pre_ref, w_in_ref, b_in_ref, w_dw_ref, b_dw_ref, ln_g_ref, ln_b_ref,
                      w_out_ref, b_out_ref, g_post_ref, o_ref, u_ref):
    x = x_ref[...]
    h = _rms(x, g_pre_ref[...]).astype(BF16)
    u = _glu(h, w_in_ref, b_in_ref)
    u_ref[...] = u
    c = b_dw_ref[...] + u * w_dw_ref[CONV_HIST:CONV_WIDTH, :]
    for k in range(CONV_HIST):
        c = c + hist_ref[k] * w_dw_ref[k:k + 1, :]
    o_ref[...] = _conv_tail(c, x, ln_g_ref, ln_b_ref, w_out_ref, b_out_ref, g_post_ref)


def _conv_sample(x, hist_t, g_pre, w_in, b_in, w_dw, b_dw, ln_g, ln_b, w_out, b_out, g_post):
    bd, d = x.shape
    return pl.pallas_call(
        _conv_sample_body,
        out_shape=[jax.ShapeDtypeStruct((bd, d), F32), jax.ShapeDtypeStruct((bd, d), F32)],
        compiler_params=pltpu.CompilerParams(vmem_limit_bytes=VMEM_LIMIT_BYTES),
        name="conv_sample",
    )(x, hist_t, g_pre, w_in, b_in, w_dw, b_dw, ln_g, ln_b, w_out, b_out, g_post)


def _kv_plan(t, tm):
    plan = []
    for win, _ in SWA_CONFIGS:
        keep = min(win, t)
        rows = min(keep, tm)
        assert keep % rows == 0 and t % tm == 0
        plan.append((rows, t // tm - keep // rows if keep >= tm else t // tm - 1))
    return tuple(plan)


def _qkv_prompt_body(x_ref, g_pre_ref, w_ref, q0, q1, q2, kv0, kv1, kv2, *, tm, plan):
    t = pl.program_id(1)
    d = x_ref.shape[-1]
    h = _rms(x_ref[0], g_pre_ref[...]).astype(BF16)
    qkv_refs = (q0, q1, q2)
    kv_refs = (kv0, kv1, kv2)
    for g in range(N_GROUPS):
        rows, first = plan[g]
        for part in range(3):
            c0 = (g * 3 + part) * d
            r = _dot(h, w_ref[:, c0:c0 + d])
            if part == 0:
                r = r * (HEAD_DIM ** -0.5)
            qkv_refs[g][0, :, part * d:(part + 1) * d] = r.astype(BF16)
            if part > 0:
                @pl.when(t >= first)
                def _(r=r, g=g, part=part, rows=rows):
                    kv_refs[g][0, :, (part - 1) * d:part * d] = r[tm - rows:, :]


def _qkv_prompt(x, g_pre, w_qkv, tm):
    b, t, d = x.shape
    plan = _kv_plan(t, tm)
    tile = pl.BlockSpec((1, tm, d), lambda i, j: (i, j, 0))
    qkv_spec = pl.BlockSpec((1, tm, 3 * d), lambda i, j: (i, j, 0))
    kv_specs, kv_shapes = [], []
    for g, (win, _) in enumerate(SWA_CONFIGS):
        rows, first = plan[g]
        kv_specs.append(pl.BlockSpec((1, rows, 2 * d), lambda i, j, first=first: (i, jnp.maximum(j - first, 0), 0)))
        kv_shapes.append(jax.ShapeDtypeStruct((b, min(win, t), 2 * d), F32))
    return pl.pallas_call(
        functools.partial(_qkv_prompt_body, tm=tm, plan=plan),
        grid=(b, t // tm),
        in_specs=[tile, _resident((1, d)), _resident(w_qkv.shape)],
        out_specs=[qkv_spec] * N_GROUPS + kv_specs,
        out_shape=[jax.ShapeDtypeStruct((b, t, 3 * d), BF16)] * N_GROUPS + kv_shapes,
        compiler_params=_params("parallel", "arbitrary"),
        name="qkv_prompt",
    )(x, g_pre, w_qkv)


def _qkv_sample_body(x_ref, g_pre_ref, w_ref, o_ref):
    h = _rms(x_ref[...], g_pre_ref[...]).astype(BF16)
    o_ref[...] = _dot(h, w_ref[...])


def _qkv_sample(x, g_pre, w_qkv):
    bd, _ = x.shape
    return pl.pallas_call(
        _qkv_sample_body,
        out_shape=jax.ShapeDtypeStruct((bd, w_qkv.shape[1]), F32),
        compiler_params=pltpu.CompilerParams(vmem_limit_bytes=VMEM_LIMIT_BYTES),
        name="qkv_sample",
    )(x, g_pre, w_qkv)


def _attn_prompt_body(*refs, chained, last):
    if chained:
        q_ref, kc_ref, vc_ref, kp_ref, vp_ref, bias_ref, op_ref, mp_ref, sp_ref, o_ref, m_ref, s_ref = refs
    else:
        q_ref, kc_ref, vc_ref, kp_ref, vp_ref, bias_ref, o_ref, m_ref, s_ref = refs
    n = pl.program_id(2)
    qi = lax.broadcasted_iota(jnp.int32, (BLK, 2 * BLK), 0)
    kj = lax.broadcasted_iota(jnp.int32, (BLK, 2 * BLK), 1)
    delta = qi + BLK - kj
    valid = (delta >= 0) & (delta <= SPAN) & ((kj >= BLK) | (n > 0))
    lane = lax.broadcasted_iota(jnp.int32, (BLK, LANES), 1)
    low = lane < HEAD_DIM

    q = q_ref[0]
    k = jnp.concatenate([kp_ref[0], kc_ref[0]], axis=0)
    v = jnp.concatenate([vp_ref[0], vc_ref[0]], axis=0)
    n_heads = q.shape[-1] // HEAD_DIM
    m_blk = jnp.zeros((BLK, LANES), F32)
    s_blk = jnp.zeros((BLK, LANES), F32)
    for pair in range(n_heads // 2):
        sl = slice(pair * HEAD_PAIR, (pair + 1) * HEAD_PAIR)
        q2, k2, v2 = q[:, sl], k[:, sl], v[:, sl]
        pv, alpha, s_new = [], [], []
        for half in range(2):
            hh = 2 * pair + half
            qm = jnp.where(low if half == 0 else jnp.logical_not(low), q2, jnp.zeros_like(q2))
            logits = lax.dot_general(qm, k2, (((1,), (1,)), ((), ())), preferred_element_type=F32) + bias_ref[hh]
            logits = jnp.where(valid, logits, NEG_INF)
            m_new = jnp.max(logits, axis=-1, keepdims=True)
            if chained:
                m_prev = mp_ref[0, :, hh:hh + 1]
                m_new = jnp.maximum(m_new, m_prev)
            p = jnp.exp(logits - m_new)
            s_g = jnp.sum(p, axis=-1, keepdims=True)
            pv.append(_dot(p.astype(BF16), v2))
            if chained:
                a = jnp.exp(m_prev - m_new)
                alpha.append(a)
                s_g = sp_ref[0, :, hh:hh + 1] * a + s_g
            s_new.append(s_g)
            m_blk = jnp.where(lane == hh, m_new, m_blk)
            s_blk = jnp.where(lane == hh, s_g, s_blk)
        o = jnp.where(low, pv[0], pv[1])
        if chained:
            o = op_ref[0, :, sl] * jnp.where(low, alpha[0], alpha[1]) + o
        if last:
            o = o / jnp.where(low, s_new[0], s_new[1])
        o_ref[0, :, sl] = o
    m_ref[0] = m_blk
    s_ref[0] = s_blk


def _attn_prompt(qkv, bias, dil, prev, last):
    b, t, d3 = qkv.shape
    d = d3 // 3
    assert t % (dil * BLK) == 0
    sub = t // dil
    nblk = sub // BLK
    qkv_v = qkv.reshape(b, sub, dil * d3)

    def col(part):
        return pl.BlockSpec((1, BLK, d), lambda i, r, n: (i, n, 3 * r + part))

    def col_prev(part):
        return pl.BlockSpec((1, BLK, d), lambda i, r, n: (i, jnp.maximum(n - 1, 0), 3 * r + part))

    o_spec = pl.BlockSpec((1, BLK, d), lambda i, r, n: (i, n, r))
    st_spec = pl.BlockSpec((1, BLK, LANES), lambda i, r, n: (i, n, r))
    in_specs = [col(0), col(1), col(2), col_prev(1), col_prev(2), _resident(bias.shape)]
    args = [qkv_v, qkv_v, qkv_v, qkv_v, qkv_v, bias]
    if prev is not None:
        in_specs += [o_spec, st_spec, st_spec]
        args += [prev[0].reshape(b, sub, dil * d), prev[1].reshape(b, sub, dil * LANES),
                 prev[2].reshape(b, sub, dil * LANES)]
    o, m, s = pl.pallas_call(
        functools.partial(_attn_prompt_body, chained=prev is not None, last=last),
        grid=(b, dil, nblk),
        in_specs=in_specs,
        out_specs=[o_spec, st_spec, st_spec],
        out_shape=[jax.ShapeDtypeStruct((b, sub, dil * d), F32), jax.ShapeDtypeStruct((b, sub, dil * LANES), F32),
                   jax.ShapeDtypeStruct((b, sub, dil * LANES), F32)],
        compiler_params=_params("parallel", "parallel", "arbitrary"),
        name=f"attn_prompt_d{dil}",
    )(*args)
    return o.reshape(b, t, d), m.reshape(b, t, LANES), s.reshape(b, t, LANES)


def _prompt_bias(rel_bias, g, dil):
    delta = np.arange(BLK)[:, None] + BLK - np.arange(2 * BLK)[None, :]
    bucket = _t5_bucket(np.clip(delta, 0, SPAN) * dil)
    return jnp.transpose(rel_bias[bucket, g], (2, 0, 1)).astype(F32)


def _attn_sample_body(q_ref, kn_ref, vn_ref, cache_ref, bias_c_ref, bias_n_ref, sel_ref, selt_ref, o_ref, lse_ref):
    bb, rows, d2 = cache_ref.shape
    d = d2 // 2
    sel = sel_ref[...]
    selt = selt_ref[...]
    nh = sel.shape[1]
    q = q_ref[...] * (HEAD_DIM ** -0.5)
    kc = cache_ref[:, :, 0:d]
    vc = cache_ref[:, :, d:d2]
    lc = _sel_dot((kc * q[:, None, :]).reshape(bb * rows, d), sel).reshape(bb, rows, nh) + bias_c_ref[...][None]
    ln = _sel_dot(q * kn_ref[...], sel) + bias_n_ref[...]
    m = jnp.maximum(jnp.max(lc, axis=1), ln)
    pc = jnp.exp(lc - m[:, None, :])
    pn = jnp.exp(ln - m)
    s = jnp.sum(pc, axis=1) + pn
    pce = _sel_dot(pc.reshape(bb * rows, nh), selt).reshape(bb, rows, d)
    o = jnp.sum(pce * vc, axis=1) + _sel_dot(pn, selt) * vn_ref[...]
    o_ref[...] = o / _sel_dot(s, selt)
    lse_ref[...] = m + jnp.log(s)


def _attn_sample(q, kn, vn, cache, bias_c, bias_n, sel, selt, dil, bb):
    bd, d = q.shape
    n_cache = cache.shape[1]
    assert n_cache == SPAN * dil and bd % bb == 0
    nh = sel.shape[1]
    cache_v = cache.reshape(bd, SPAN, dil * 2 * d)
    row = pl.BlockSpec((bb, d), lambda i: (i, 0))
    return pl.pallas_call(
        _attn_sample_body,
        grid=(bd // bb,),
        in_specs=[row, row, row, pl.BlockSpec((bb, SPAN, 2 * d), lambda i: (i, 0, 0)),
                  _resident(bias_c.shape), _resident(bias_n.shape), _resident(sel.shape), _resident(selt.shape)],
        out_specs=[row, pl.BlockSpec((bb, nh), lambda i: (i, 0))],
        out_shape=[jax.ShapeDtypeStruct((bd, d), F32), jax.ShapeDtypeStruct((bd, nh), F32)],
        compiler_params=_params("parallel"),
        name=f"attn_sample_d{dil}",
    )(q, kn, vn, cache_v, bias_c, bias_n, sel, selt)


def _wo_prompt_body(x_ref, o_ref_in, w_ref, g_post_ref, out_ref):
    y = _dot(o_ref_in[...].astype(BF16), w_ref[...])
    out_ref[...] = x_ref[...] + _rms(y, g_post_ref[...])


def _wo_prompt(x, o, w_o, g_post, tm):
    n, d = x.shape
    row = pl.BlockSpec((tm, d), lambda i: (i, 0))
    return pl.pallas_call(
        _wo_prompt_body,
        grid=(n // tm,),
        in_specs=[row, row, _resident(w_o.shape), _resident((1, d))],
        out_specs=row,
        out_shape=jax.ShapeDtypeStruct((n, d), F32),
        compiler_params=_params("parallel"),
        name="wo_prompt",
    )(x, o, w_o, g_post)


def _wo_sample_body(x_ref, o0_ref, o1_ref, o2_ref, l0_ref, l1_ref, l2_ref, selt_ref, w_ref, g_post_ref, out_ref):
    lses = (l0_ref[...], l1_ref[...], l2_ref[...])
    outs = (o0_ref[...], o1_ref[...], o2_ref[...])
    mx = jnp.maximum(jnp.maximum(lses[0], lses[1]), lses[2])
    es = [jnp.exp(l - mx) for l in lses]
    tot = es[0] + es[1] + es[2]
    selt = selt_ref[...]
    merged = sum(_sel_dot(e / tot, selt) * o for e, o in zip(es, outs))
    y = _dot(merged.astype(BF16), w_ref[...])
    out_ref[...] = x_ref[...] + _rms(y, g_post_ref[...])


def _wo_sample(x, outs, lses, selt, w_o, g_post):
    return pl.pallas_call(
        _wo_sample_body,
        out_shape=jax.ShapeDtypeStruct(x.shape, F32),
        compiler_params=pltpu.CompilerParams(vmem_limit_bytes=VMEM_LIMIT_BYTES),
        name="wo_sample",
    )(x, *outs, *lses, selt, w_o, g_post)


def kernel(x_prompt, x_sample, state_pool, cache_swa_g0, cache_swa_g1, cache_swa_g2, state_conv,
           norm_g, w_ffn_in, w_ffn_out, pool_w, pool_scale, w_qkv, w_o, rel_bias,
           conv_w_in, conv_b_in, conv_w_dw, conv_b_dw, conv_ln_g, conv_ln_b, conv_w_out, conv_b_out):
    b, t, d = x_prompt.shape
    bd, t_dec, _ = x_sample.shape
    assert t_dec == 1 and d % HEAD_PAIR == 0
    nh = d // HEAD_DIM
    depth = norm_g.shape[0]
    caches = (cache_swa_g0, cache_swa_g1, cache_swa_g2)

    tm = 512 if (b * t) % 512 == 0 else b * t
    ts = min(512, t)
    tq = min(256, t)

    xp = x_prompt
    xs = x_sample.reshape(bd, d)
    vec = lambda a: a.reshape(1, -1).astype(F32)

    sel_np = (np.arange(d)[:, None] // HEAD_DIM == np.arange(nh)[None, :]).astype(np.float32)
    sel = jnp.asarray(sel_np, BF16)
    selt = jnp.asarray(sel_np.T, BF16)

    pool_p, pool_s, conv_p, conv_s = [], [], [], []
    swa_p = [[] for _ in range(N_GROUPS)]
    swa_s = [[] for _ in range(N_GROUPS)]
    for i in range(depth):
        kind, j = i % 3, i // 3
        g = [vec(norm_g[i, k]) for k in range(4)]
        if kind == 0:
            pw = pool_w[j].astype(BF16)
            ps = vec(pool_scale[j])
            xp, tail = _pool_prompt(xp, jnp.zeros((b, POOL_HALO, d), F32), g[0], pw, ps, g[1], ts, 0)
            pool_p.append(tail[:, POOL_HALO - POOL_HIST:])
            hist = state_pool[j]
            xs, hs = _pool_sample(xs, jnp.swapaxes(hist, 0, 1), g[0], pw, ps, g[1], PAST_LEN)
            pool_s.append(jnp.concatenate([hist[:, 1:], hs[:, None, :]], axis=1))
        elif kind == 1:
            wq = w_qkv[j].astype(BF16)
            wo = w_o[j].astype(BF16)
            outs = _qkv_prompt(xp, g[0], wq, tq)
            prev = None
            for gi, (_, dil) in enumerate(SWA_CONFIGS):
                prev = _attn_prompt(outs[gi], _prompt_bias(rel_bias, gi, dil), dil, prev, gi == N_GROUPS - 1)
                swa_p[gi].append(outs[N_GROUPS + gi].reshape(b, -1, 2, nh, HEAD_DIM))
            xp = _wo_prompt(xp.reshape(b * t, d), prev[0].reshape(b * t, d), wo, g[1], tm).reshape(b, t, d)

            qkv_s = _qkv_sample(xs, g[0], wq)
            o_s, lse_s = [], []
            for gi, (_, dil) in enumerate(SWA_CONFIGS):
                c0 = gi * 3 * d
                q_s, k_s, v_s = (qkv_s[:, c0 + p * d:c0 + (p + 1) * d] for p in range(3))
                steps = np.arange(SPAN, 0, -1) * dil
                bias_c = rel_bias[_t5_bucket(steps), gi].astype(F32)
                bias_n = rel_bias[_t5_bucket(np.zeros(1, np.int64)), gi].astype(F32)
                o_g, l_g = _attn_sample(q_s, k_s, v_s, caches[gi][j], bias_c, bias_n, sel, selt, dil, 8)
                o_s.append(o_g)
                lse_s.append(l_g)
                swa_s[gi].append(qkv_s[:, c0 + d:c0 + 3 * d].reshape(bd, 1, 2, nh, HEAD_DIM))
            xs = _wo_sample(xs, o_s, lse_s, selt, wo, g[1])
        else:
            cw = (conv_w_in[j].astype(BF16), vec(conv_b_in[j]),
                  jnp.pad(conv_w_dw[j], ((0, CONV_HALO - CONV_WIDTH), (0, 0))), vec(conv_b_dw[j]),
                  vec(conv_ln_g[j]), vec(conv_ln_b[j]), conv_w_out[j].astype(BF16), vec(conv_b_out[j]))
            xp, tail = _conv_prompt(xp, jnp.zeros((b, CONV_HALO, d), F32), g[0], *cw, g[1], ts, 32)
            conv_p.append(tail[:, CONV_HALO - CONV_HIST:])
            hist = state_conv[j]
            xs, us = _conv_sample(xs, jnp.swapaxes(hist, 0, 1), g[0], *cw, g[1])
            conv_s.append(jnp.concatenate([hist[:, 1:], us[:, None, :]], axis=1))
        w_in = w_ffn_in[i].astype(BF16)
        w_out = w_ffn_out[i].astype(BF16)
        xp = _ffn(xp.reshape(b * t, d), g[2], w_in, w_out, g[3], tm).reshape(b, t, d)
        xs = _ffn(xs, g[2], w_in, w_out, g[3], bd)

    return (xp, xs.reshape(bd, 1, d),
            jnp.stack(pool_p), jnp.stack(pool_s),
            jnp.stack(swa_p[0]), jnp.stack(swa_p[1]), jnp.stack(swa_p[2]),
            jnp.stack(swa_s[0]), jnp.stack(swa_s[1]), jnp.stack(swa_s[2]),
            jnp.stack(conv_p), jnp.stack(conv_s))
```

```python
import functools
import math

import numpy as np
import jax
import jax.numpy as jnp
from jax import lax
from jax.experimental import pallas as pl
from jax.experimental.pallas import tpu as pltpu

F32 = jnp.float32
BF16 = jnp.bfloat16

RMS_EPS = 1e-6
LN_EPS = 1e-5
NEG_INF = -1e30
PAST_LEN = 8192
POOL_WINDOWS = (2, 4, 8, 16)
POOL_HIST = max(POOL_WINDOWS) - 1
SWA_CONFIGS = ((128, 1), (512, 4), (2048, 16))
N_GROUPS = len(SWA_CONFIGS)
SPAN = 128
BLK = 128
HEAD_DIM = 64
N_BUCKETS = 32
MAX_DISTANCE = 2048
CONV_WIDTH = 31
CONV_HIST = CONV_WIDTH - 1

LANES = 128
SUBLANES = 8
MXU_DIM = 256
VMEM_LIMIT_BYTES = 56 * 1024 * 1024

POOL_HALO = 16
CONV_HALO = 32
HEAD_PAIR = 2 * HEAD_DIM


def _params(*sem):
    return pltpu.CompilerParams(dimension_semantics=sem, vmem_limit_bytes=VMEM_LIMIT_BYTES)


def _resident(shape):
    nd = len(shape)
    return pl.BlockSpec(shape, lambda *_: (0,) * nd, pipeline_mode=pl.Buffered(1))


def _rms(x, g):
    return x * lax.rsqrt(jnp.mean(x * x, axis=-1, keepdims=True) + RMS_EPS) * g


def _dot(a, b):
    return jnp.dot(a, b, preferred_element_type=F32)


def _t5_bucket(dist):
    max_exact = N_BUCKETS // 2
    d = np.maximum(np.asarray(dist), 0)
    large = max_exact + (np.log(np.maximum(d, max_exact) / max_exact) / math.log(MAX_DISTANCE / max_exact)
                         * (N_BUCKETS - max_exact)).astype(np.int64)
    large = np.minimum(large, N_BUCKETS - 1)
    return np.where(d < max_exact, d, large).astype(np.int32)


def _ffn_body(x_ref, g_pre_ref, w_in_ref, w_out_ref, g_post_ref, o_ref, *, d_ff, chunks):
    x = x_ref[...]
    h = _rms(x, g_pre_ref[...]).astype(BF16)
    acc = None
    for c0, cw in chunks:
        gate = _dot(h, w_in_ref[:, c0:c0 + cw])
        up = _dot(h, w_in_ref[:, d_ff + c0:d_ff + c0 + cw])
        act = (gate * jax.nn.sigmoid(gate) * up).astype(BF16)
        part = _dot(act, w_out_ref[c0:c0 + cw, :])
        acc = part if acc is None else acc + part
    o_ref[...] = x + _rms(acc, g_post_ref[...])


def _ff_chunks(d_ff):
    assert d_ff % MXU_DIM == 0
    n = d_ff // MXU_DIM
    first = (n + 1) // 2 * MXU_DIM
    return ((0, first), (first, d_ff - first)) if d_ff > first else ((0, d_ff),)


def _ffn(x, g_pre, w_in, w_out, g_post, tm):
    n, d = x.shape
    d_ff = w_out.shape[0]
    assert n % tm == 0
    row = pl.BlockSpec((tm, d), lambda i: (i, 0))
    return pl.pallas_call(
        functools.partial(_ffn_body, d_ff=d_ff, chunks=_ff_chunks(d_ff)),
        grid=(n // tm,),
        in_specs=[row, _resident((1, d)), _resident(w_in.shape), _resident(w_out.shape), _resident((1, d))],
        out_specs=row,
        out_shape=jax.ShapeDtypeStruct((n, d), F32),
        compiler_params=_params("parallel"),
        name="ffn",
    )(x, g_pre, w_in, w_out, g_post)


def _pool_groups(h, window_sum, cnt_of, pw_ref, ps_ref):
    gd = h.shape[-1] // len(POOL_WINDOWS)
    outs = []
    for gi, w in enumerate(POOL_WINDOWS):
        c0 = gi * gd
        hg = h[:, c0:c0 + gd]
        dg = window_sum(gi, w, c0, gd, hg) / cnt_of(w) - hg
        outs.append(_dot(dg.astype(BF16), pw_ref[gi]))
    return jnp.concatenate(outs, axis=1) * ps_ref[...]


def _pool_prompt_body(x_ref, xprev_ref, hist_ref, g_pre_ref, pw_ref, ps_ref, g_post_ref,
                      o_ref, tail_ref, buf, *, ts, n_past):
    t = pl.program_id(1)
    x = x_ref[0]
    g_pre = g_pre_ref[...]
    h = _rms(x, g_pre)
    halo = jnp.where(t == 0, hist_ref[0], _rms(xprev_ref[0], g_pre))
    buf[0:POOL_HALO, :] = halo
    buf[POOL_HALO:POOL_HALO + ts, :] = h
    tail_ref[0] = buf[ts:ts + POOL_HALO, :]
    pos1 = n_past + 1 + t * ts + lax.broadcasted_iota(jnp.int32, (ts, 1), 0)

    def window_sum(gi, w, c0, gd, hg):
        win = hg
        for k in range(1, w):
            win = win + buf[POOL_HALO - k:POOL_HALO - k + ts, c0:c0 + gd]
        return win

    y = _pool_groups(h, window_sum, lambda w: jnp.minimum(pos1, w).astype(F32), pw_ref, ps_ref)
    o_ref[0] = x + _rms(y, g_post_ref[...])


def _pool_prompt(x, hist16, g_pre, pw, ps, g_post, ts, n_past):
    b, t, d = x.shape
    assert t % ts == 0 and ts % POOL_HALO == 0
    per = ts // POOL_HALO
    tile = pl.BlockSpec((1, ts, d), lambda i, j: (i, j, 0))
    halo = pl.BlockSpec((1, POOL_HALO, d), lambda i, j: (i, jnp.maximum(j * per - 1, 0), 0))
    first = pl.BlockSpec((1, POOL_HALO, d), lambda i, j: (i, 0, 0))
    return pl.pallas_call(
        functools.partial(_pool_prompt_body, ts=ts, n_past=n_past),
        grid=(b, t // ts),
        in_specs=[tile, halo, first, _resident((1, d)), _resident(pw.shape), _resident((1, d)), _resident((1, d))],
        out_specs=[tile, first],
        out_shape=[jax.ShapeDtypeStruct((b, t, d), F32), jax.ShapeDtypeStruct((b, POOL_HALO, d), F32)],
        scratch_shapes=[pltpu.VMEM((POOL_HALO + ts, d), F32)],
        compiler_params=_params("parallel", "arbitrary"),
        name="pool_prompt",
    )(x, x, hist16, g_pre, pw, ps, g_post)


def _pool_sample_body(x_ref, hist_ref, g_pre_ref, pw_ref, ps_ref, g_post_ref, o_ref, h_ref, *, n_past):
    x = x_ref[...]
    h = _rms(x, g_pre_ref[...])
    h_ref[...] = h

    def window_sum(gi, w, c0, gd, hg):
        win = hg
        for k in range(1, w):
            win = win + hist_ref[POOL_HIST - k, :, c0:c0 + gd]
        return win

    y = _pool_groups(h, window_sum, lambda w: float(min(n_past + 1, w)), pw_ref, ps_ref)
    o_ref[...] = x + _rms(y, g_post_ref[...])


def _pool_sample(x, hist_t, g_pre, pw, ps, g_post, n_past):
    bd, d = x.shape
    return pl.pallas_call(
        functools.partial(_pool_sample_body, n_past=n_past),
        out_shape=[jax.ShapeDtypeStruct((bd, d), F32), jax.ShapeDtypeStruct((bd, d), F32)],
        compiler_params=pltpu.CompilerParams(vmem_limit_bytes=VMEM_LIMIT_BYTES),
        name="pool_sample",
    )(x, hist_t, g_pre, pw, ps, g_post)


def _glu(h, w_in_ref, b_in_ref):
    d = h.shape[-1]
    a = _dot(h, w_in_ref[...]) + b_in_ref[...]
    return a[:, :d] * jax.nn.sigmoid(a[:, d:])


def _conv_tail(c, x, ln_g_ref, ln_b_ref, w_out_ref, b_out_ref, g_post_ref):
    mu = jnp.mean(c, axis=-1, keepdims=True)
    cc = c - mu
    var = jnp.mean(cc * cc, axis=-1, keepdims=True)
    z = cc * lax.rsqrt(var + LN_EPS) * ln_g_ref[...] + ln_b_ref[...]
    y = _dot((z * jax.nn.sigmoid(z)).astype(BF16), w_out_ref[...]) + b_out_ref[...]
    return x + _rms(y, g_post_ref[...])


def _conv_prompt_body(x_ref, hist_ref, g_pre_ref, w_in_ref, b_in_ref, w_dw_ref, b_dw_ref, ln_g_ref, ln_b_ref,
                      w_out_ref, b_out_ref, g_post_ref, o_ref, tail_ref, ubuf, cbuf, *, ts, rc):
    t = pl.program_id(1)
    d = x_ref.shape[-1]

    @pl.when(t == 0)
    def _():
        ubuf[0:CONV_HALO, :] = hist_ref[0]

    @pl.when(t > 0)
    def _():
        ubuf[0:CONV_HALO, :] = ubuf[ts:ts + CONV_HALO, :]

    x = x_ref[0]
    h = _rms(x, g_pre_ref[...]).astype(BF16)
    ubuf[CONV_HALO:CONV_HALO + ts, :] = _glu(h, w_in_ref, b_in_ref)
    tail_ref[0] = ubuf[ts:ts + CONV_HALO, :]

    lead = CONV_HALO - CONV_HIST

    def chunk(i, carry):
        base = pl.multiple_of(i * rc, rc)
        win = ubuf[pl.ds(base, rc + CONV_HALO), :]
        acc = jnp.broadcast_to(b_dw_ref[...], (rc, d))
        for phase in range(SUBLANES):
            offs = [o for o in range(lead, lead + CONV_WIDTH) if o % SUBLANES == phase]
            shifted = win[phase:phase + offs[-1] - phase + rc, :]
            for o in offs:
                acc = acc + shifted[o - phase:o - phase + rc, :] * w_dw_ref[o - lead:o - lead + 1, :]
        cbuf[pl.ds(base, rc), :] = acc
        return carry

    lax.fori_loop(0, ts // rc, chunk, 0)
    o_ref[0] = _conv_tail(cbuf[...], x, ln_g_ref, ln_b_ref, w_out_ref, b_out_ref, g_post_ref)


def _conv_prompt(x, hist32, g_pre, w_in, b_in, w_dw, b_dw, ln_g, ln_b, w_out, b_out, g_post, ts, rc):
    b, t, d = x.shape
    assert t % ts == 0 and ts % rc == 0 and ts >= CONV_HALO
    tile = pl.BlockSpec((1, ts, d), lambda i, j: (i, j, 0))
    first = pl.BlockSpec((1, CONV_HALO, d), lambda i, j: (i, 0, 0))
    vec = _resident((1, d))
    return pl.pallas_call(
        functools.partial(_conv_prompt_body, ts=ts, rc=rc),
        grid=(b, t // ts),
        in_specs=[tile, first, vec, _resident(w_in.shape), _resident(b_in.shape), _resident(w_dw.shape), vec, vec,
                  vec, _resident(w_out.shape), vec, vec],
        out_specs=[tile, first],
        out_shape=[jax.ShapeDtypeStruct((b, t, d), F32), jax.ShapeDtypeStruct((b, CONV_HALO, d), F32)],
        scratch_shapes=[pltpu.VMEM((CONV_HALO + ts, d), F32), pltpu.VMEM((ts, d), F32)],
        compiler_params=_params("parallel", "arbitrary"),
        name="conv_prompt",
    )(x, hist32, g_pre, w_in, b_in, w_dw, b_dw, ln_g, ln_b, w_out, b_out, g_post)


def _conv_sample_body(x_ref, hist_ref, g_pre_ref, w_in_ref, b_in_ref, w_dw_ref, b_dw_ref, ln_g_ref, ln_b_ref,
                      w_out_ref, b_out_ref, g_post_ref, o_ref, u_ref):
    x = x_ref[...]
    h = _rms(x, g_pre_ref[...]).astype(BF16)
    u = _glu(h, w_in_ref, b_in_ref)
    u_ref[...] = u
    c = b_dw_ref[...] + u * w_dw_ref[CONV_HIST:CONV_WIDTH, :]
    for k in range(CONV_HIST):
        c = c + hist_ref[k] * w_dw_ref[k:k + 1, :]
    o_ref[...] = _conv_tail(c, x, ln_g_ref, ln_b_ref, w_out_ref, b_out_ref, g_post_ref)


def _conv_sample(x, hist_t, g_pre, w_in, b_in, w_dw, b_dw, ln_g, ln_b, w_out, b_out, g_post):
    bd, d = x.shape
    return pl.pallas_call(
        _conv_sample_body,
        out_shape=[jax.ShapeDtypeStruct((bd, d), F32), jax.ShapeDtypeStruct((bd, d), F32)],
        compiler_params=pltpu.CompilerParams(vmem_limit_bytes=VMEM_LIMIT_BYTES),
        name="conv_sample",
    )(x, hist_t, g_pre, w_in, b_in, w_dw, b_dw, ln_g, ln_b, w_out, b_out, g_post)


def _kv_plan(t, tm):
    plan = []
    for win, _ in SWA_CONFIGS:
        keep = min(win, t)
        rows = min(keep, tm)
        assert keep % rows == 0 and t % tm == 0
        plan.append((rows, t // tm - keep // rows if keep >= tm else t // tm - 1))
    return tuple(plan)


def _qkv_prompt_body(x_ref, g_pre_ref, w_ref, q0, q1, q2, kv0, kv1, kv2, *, tm, plan):
    t = pl.program_id(1)
    d = x_ref.shape[-1]
    h = _rms(x_ref[0], g_pre_ref[...]).astype(BF16)
    qkv_refs = (q0, q1, q2)
    kv_refs = (kv0, kv1, kv2)
    for g in range(N_GROUPS):
        rows, first = plan[g]
        for part in range(3):
            c0 = (g * 3 + part) * d
            r = _dot(h, w_ref[:, c0:c0 + d])
            if part == 0:
                r = r * (HEAD_DIM ** -0.5)
            qkv_refs[g][0, :, part * d:(part + 1) * d] = r.astype(BF16)
            if part > 0:
                @pl.when(t >= first)
                def _(r=r, g=g, part=part, rows=rows):
                    kv_refs[g][0, :, (part - 1) * d:part * d] = r[tm - rows:, :]


def _qkv_prompt(x, g_pre, w_qkv, tm):
    b, t, d = x.shape
    plan = _kv_plan(t, tm)
    tile = pl.BlockSpec((1, tm, d), lambda i, j: (i, j, 0))
    qkv_spec = pl.BlockSpec((1, tm, 3 * d), lambda i, j: (i, j, 0))
    kv_specs, kv_shapes = [], []
    for g, (win, _) in enumerate(SWA_CONFIGS):
        rows, first = plan[g]
        kv_specs.append(pl.BlockSpec((1, rows, 2 * d), lambda i, j, first=first: (i, jnp.maximum(j - first, 0), 0)))
        kv_shapes.append(jax.ShapeDtypeStruct((b, min(win, t), 2 * d), F32))
    return pl.pallas_call(
        functools.partial(_qkv_prompt_body, tm=tm, plan=plan),
        grid=(b, t // tm),
        in_specs=[tile, _resident((1, d)), _resident(w_qkv.shape)],
        out_specs=[qkv_spec] * N_GROUPS + kv_specs,
        out_shape=[jax.ShapeDtypeStruct((b, t, 3 * d), BF16)] * N_GROUPS + kv_shapes,
        compiler_params=_params("parallel", "arbitrary"),
        name="qkv_prompt",
    )(x, g_pre, w_qkv)


def _qkv_sample_body(x_ref, g_pre_ref, w_ref, o_ref):
    h = _rms(x_ref[...], g_pre_ref[...]).astype(BF16)
    o_ref[...] = _dot(h, w_ref[...])


def _qkv_sample(x, g_pre, w_qkv):
    bd, _ = x.shape
    return pl.pallas_call(
        _qkv_sample_body,
        out_shape=jax.ShapeDtypeStruct((bd, w_qkv.shape[1]), F32),
        compiler_params=pltpu.CompilerParams(vmem_limit_bytes=VMEM_LIMIT_BYTES),
        name="qkv_sample",
    )(x, g_pre, w_qkv)


def _attn_prompt_body(*refs, chained, last):
    if chained:
        q_ref, kc_ref, vc_ref, kp_ref, vp_ref, bias_ref, op_ref, mp_ref, sp_ref, o_ref, m_ref, s_ref = refs
    else:
        q_ref, kc_ref, vc_ref, kp_ref, vp_ref, bias_ref, o_ref, m_ref, s_ref = refs
    n = pl.program_id(2)
    qi = lax.broadcasted_iota(jnp.int32, (BLK, 2 * BLK), 0)
    kj = lax.broadcasted_iota(jnp.int32, (BLK, 2 * BLK), 1)
    delta = qi + BLK - kj
    valid = (delta >= 0) & (delta <= SPAN) & ((kj >= BLK) | (n > 0))
    lane = lax.broadcasted_iota(jnp.int32, (BLK, LANES), 1)
    low = lane < HEAD_DIM

    q = q_ref[0]
    k = jnp.concatenate([kp_ref[0], kc_ref[0]], axis=0)
    v = jnp.concatenate([vp_ref[0], vc_ref[0]], axis=0)
    n_heads = q.shape[-1] // HEAD_DIM
    m_blk = jnp.zeros((BLK, LANES), F32)
    s_blk = jnp.zeros((BLK, LANES), F32)
    for pair in range(n_heads // 2):
        sl = slice(pair * HEAD_PAIR, (pair + 1) * HEAD_PAIR)
        q2, k2, v2 = q[:, sl], k[:, sl], v[:, sl]
        pv, alpha, s_new = [], [], []
        for half in range(2):
            hh = 2 * pair + half
            qm = jnp.where(low if half == 0 else jnp.logical_not(low), q2, jnp.zeros_like(q2))
            logits = lax.dot_general(qm, k2, (((1,), (1,)), ((), ())), preferred_element_type=F32) + bias_ref[hh]
            logits = jnp.where(valid, logits, NEG_INF)
            m_new = jnp.max(logits, axis=-1, keepdims=True)
            if chained:
                m_prev = mp_ref[0, :, hh:hh + 1]
                m_new = jnp.maximum(m_new, m_prev)
            p = jnp.exp(logits - m_new)
            s_g = jnp.sum(p, axis=-1, keepdims=True)
            pv.append(_dot(p.astype(BF16), v2))
            if chained:
                a = jnp.exp(m_prev - m_new)
                alpha.append(a)
                s_g = sp_ref[0, :, hh:hh + 1] * a + s_g
            s_new.append(s_g)
            m_blk = jnp.where(lane == hh, m_new, m_blk)
            s_blk = jnp.where(lane == hh, s_g, s_blk)
        o = jnp.where(low, pv[0], pv[1])
        if chained:
            o = op_ref[0, :, sl] * jnp.where(low, alpha[0], alpha[1]) + o
        if last:
            o = o / jnp.where(low, s_new[0], s_new[1])
        o_ref[0, :, sl] = o
    m_ref[0] = m_blk
    s_ref[0] = s_blk


def _attn_prompt(qkv, bias, dil, prev, last):
    b, t, d3 = qkv.shape
    d = d3 // 3
    assert t % (dil * BLK) == 0
    sub = t // dil
    nblk = sub // BLK
    qkv_v = qkv.reshape(b, sub, dil * d3)

    def col(part):
        return pl.BlockSpec((1, BLK, d), lambda i, r, n: (i, n, 3 * r + part))

    def col_prev(part):
        return pl.BlockSpec((1, BLK, d), lambda i, r, n: (i, jnp.maximum(n - 1, 0), 3 * r + part))

    o_spec = pl.BlockSpec((1, BLK, d), lambda i, r, n: (i, n, r))
    st_spec = pl.BlockSpec((1, BLK, LANES), lambda i, r, n: (i, n, r))
    in_specs = [col(0), col(1), col(2), col_prev(1), col_prev(2), _resident(bias.shape)]
    args = [qkv_v, qkv_v, qkv_v, qkv_v, qkv_v, bias]
    if prev is not None:
        in_specs += [o_spec, st_spec, st_spec]
        args += [prev[0].reshape(b, sub, dil * d), prev[1].reshape(b, sub, dil * LANES),
                 prev[2].reshape(b, sub, dil * LANES)]
    o, m, s = pl.pallas_call(
        functools.partial(_attn_prompt_body, chained=prev is not None, last=last),
        grid=(b, dil, nblk),
        in_specs=in_specs,
        out_specs=[o_spec, st_spec, st_spec],
        out_shape=[jax.ShapeDtypeStruct((b, sub, dil * d), F32), jax.ShapeDtypeStruct((b, sub, dil * LANES), F32),
                   jax.ShapeDtypeStruct((b, sub, dil * LANES), F32)],
        compiler_params=_params("parallel", "parallel", "arbitrary"),
        name=f"attn_prompt_d{dil}",
    )(*args)
    return o.reshape(b, t, d), m.reshape(b, t, LANES), s.reshape(b, t, LANES)


def _bias_table(rel_bias, g, dil):
    onehot = np.eye(N_BUCKETS, dtype=np.float32)[_t5_bucket(np.arange(SPAN + 1) * dil)]
    return jnp.dot(rel_bias[:, g, :].astype(F32).T, jnp.asarray(onehot).T, precision=lax.Precision.HIGHEST)


def _prompt_bias(table):
    nh = table.shape[0]
    width = 3 * BLK - 1
    rev = jnp.concatenate([jnp.broadcast_to(table[:, SPAN:], (nh, BLK - 1)), table[:, ::-1],
                           jnp.broadcast_to(table[:, :1], (nh, BLK - 1)), jnp.zeros((nh, 1), F32)], axis=1)
    skew = jnp.tile(rev, (1, BLK))[:, :BLK * width].reshape(nh, BLK, width)
    return skew[:, :, BLK - 1:]


def _attn_sample_body(q_ref, kn_ref, vn_ref, cache_ref, bias_c_ref, bias_n_ref, o_ref, lse_ref):
    bias_c = bias_c_ref[...]
    bias_n = bias_n_ref[...]
    for i in range(q_ref.shape[0]):
        q = q_ref[i] * (HEAD_DIM ** -0.5)
        lc = jnp.sum(cache_ref[i, :, 0] * q[None], axis=-1, keepdims=True) + bias_c
        ln = jnp.sum(kn_ref[i] * q, axis=-1, keepdims=True) + bias_n
        m = jnp.maximum(jnp.max(lc, axis=0), ln)
        pc = jnp.exp(lc - m[None])
        pn = jnp.exp(ln - m)
        s = jnp.sum(pc, axis=0) + pn
        o = jnp.sum(pc * cache_ref[i, :, 1], axis=0) + pn * vn_ref[i]
        o_ref[i] = o / s
        lse_ref[i] = m + jnp.log(s)


def _attn_sample(q, kn, vn, cache, layer, bias_c, bias_n, dil, bb):
    bd, nh, dh = q.shape
    n_layers, _, n_cache = cache.shape[:3]
    assert n_cache == SPAN * dil and bd % bb == 0
    cache_v = cache.reshape(n_layers, bd, SPAN, dil, 2, nh, dh)
    row = pl.BlockSpec((bb, nh, dh), lambda i: (i, 0, 0))
    return pl.pallas_call(
        _attn_sample_body,
        grid=(bd // bb,),
        in_specs=[row, row, row,
                  pl.BlockSpec((None, bb, SPAN, None, 2, nh, dh), lambda i: (layer, i, 0, 0, 0, 0, 0)),
                  _resident(bias_c.shape), _resident(bias_n.shape)],
        out_specs=[row, pl.BlockSpec((bb, nh, 1), lambda i: (i, 0, 0))],
        out_shape=[jax.ShapeDtypeStruct((bd, nh, dh), F32), jax.ShapeDtypeStruct((bd, nh, 1), F32)],
        compiler_params=_params("parallel"),
        name=f"attn_sample_d{dil}",
    )(q, kn, vn, cache_v, bias_c, bias_n)


def _merge_sample_body(o0_ref, o1_ref, o2_ref, l0_ref, l1_ref, l2_ref, out_ref):
    lses = (l0_ref[...], l1_ref[...], l2_ref[...])
    outs = (o0_ref[...], o1_ref[...], o2_ref[...])
    mx = jnp.maximum(jnp.maximum(lses[0], lses[1]), lses[2])
    es = [jnp.exp(l - mx) for l in lses]
    tot = es[0] + es[1] + es[2]
    out_ref[...] = sum((e / tot) * o for e, o in zip(es, outs))


def _merge_sample(outs, lses):
    return pl.pallas_call(
        _merge_sample_body,
        out_shape=jax.ShapeDtypeStruct(outs[0].shape, F32),
        name="merge_sample",
    )(*outs, *lses)


def _wo_prompt_body(x_ref, o_ref_in, w_ref, g_post_ref, out_ref):
    y = _dot(o_ref_in[...].astype(BF16), w_ref[...])
    out_ref[...] = x_ref[...] + _rms(y, g_post_ref[...])


def _wo_prompt(x, o, w_o, g_post, tm):
    n, d = x.shape
    row = pl.BlockSpec((tm, d), lambda i: (i, 0))
    return pl.pallas_call(
        _wo_prompt_body,
        grid=(n // tm,),
        in_specs=[row, row, _resident(w_o.shape), _resident((1, d))],
        out_specs=row,
        out_shape=jax.ShapeDtypeStruct((n, d), F32),
        compiler_params=_params("parallel"),
        name="wo_prompt",
    )(x, o, w_o, g_post)


def _wo_sample(x, o, w_o, g_post):
    return pl.pallas_call(
        _wo_prompt_body,
        out_shape=jax.ShapeDtypeStruct(x.shape, F32),
        compiler_params=pltpu.CompilerParams(vmem_limit_bytes=VMEM_LIMIT_BYTES),
        name="wo_sample",
    )(x, o, w_o, g_post)


def kernel(x_prompt, x_sample, state_pool, cache_swa_g0, cache_swa_g1, cache_swa_g2, state_conv,
           norm_g, w_ffn_in, w_ffn_out, pool_w, pool_scale, w_qkv, w_o, rel_bias,
           conv_w_in, conv_b_in, conv_w_dw, conv_b_dw, conv_ln_g, conv_ln_b, conv_w_out, conv_b_out):
    b, t, d = x_prompt.shape
    bd, t_dec, _ = x_sample.shape
    assert t_dec == 1 and d % HEAD_PAIR == 0
    nh = d // HEAD_DIM
    depth = norm_g.shape[0]
    caches = (cache_swa_g0, cache_swa_g1, cache_swa_g2)

    tm = 512 if (b * t) % 512 == 0 else b * t
    ts = min(512, t)
    tq = min(256, t)

    xp = x_prompt
    xs = x_sample.reshape(bd, d)
    vec = lambda a: a.reshape(1, -1).astype(F32)

    pool_p, pool_s, conv_p, conv_s = [], [], [], []
    swa_p = [[] for _ in range(N_GROUPS)]
    swa_s = [[] for _ in range(N_GROUPS)]
    for i in range(depth):
        kind, j = i % 3, i // 3
        g = [vec(norm_g[i, k]) for k in range(4)]
        if kind == 0:
            pw = pool_w[j].astype(BF16)
            ps = vec(pool_scale[j])
            xp, tail = _pool_prompt(xp, jnp.zeros((b, POOL_HALO, d), F32), g[0], pw, ps, g[1], ts, 0)
            pool_p.append(tail[:, POOL_HALO - POOL_HIST:])
            hist = state_pool[j]
            xs, hs = _pool_sample(xs, jnp.swapaxes(hist, 0, 1), g[0], pw, ps, g[1], PAST_LEN)
            pool_s.append(jnp.concatenate([hist[:, 1:], hs[:, None, :]], axis=1))
        elif kind == 1:
            wq = w_qkv[j].astype(BF16)
            wo = w_o[j].astype(BF16)
            outs = _qkv_prompt(xp, g[0], wq, tq)
            tables = [_bias_table(rel_bias, gi, dil) for gi, (_, dil) in enumerate(SWA_CONFIGS)]
            prev = None
            for gi, (_, dil) in enumerate(SWA_CONFIGS):
                prev = _attn_prompt(outs[gi], _prompt_bias(tables[gi]), dil, prev, gi == N_GROUPS - 1)
                swa_p[gi].append(outs[N_GROUPS + gi].reshape(b, -1, 2, nh, HEAD_DIM))
            xp = _wo_prompt(xp.reshape(b * t, d), prev[0].reshape(b * t, d), wo, g[1], tm).reshape(b, t, d)

            qkv_s = _qkv_sample(xs, g[0], wq).reshape(bd, N_GROUPS, 3, nh, HEAD_DIM)
            o_s, lse_s = [], []
            for gi, (_, dil) in enumerate(SWA_CONFIGS):
                bias_c = jnp.transpose(tables[gi][:, :0:-1])[:, :, None]
                bias_n = tables[gi][:, :1]
                o_g, l_g = _attn_sample(qkv_s[:, gi, 0], qkv_s[:, gi, 1], qkv_s[:, gi, 2], caches[gi], j,
                                        bias_c, bias_n, dil, 4)
                o_s.append(o_g)
                lse_s.append(l_g)
                swa_s[gi].append(qkv_s[:, gi, 1:3].reshape(bd, 1, 2, nh, HEAD_DIM))
            xs = _wo_sample(xs, _merge_sample(o_s, lse_s).reshape(bd, d), wo, g[1])
        else:
            cw = (conv_w_in[j].astype(BF16), vec(conv_b_in[j]),
                  jnp.pad(conv_w_dw[j], ((0, CONV_HALO - CONV_WIDTH), (0, 0))), vec(conv_b_dw[j]),
                  vec(conv_ln_g[j]), vec(conv_ln_b[j]), conv_w_out[j].astype(BF16), vec(conv_b_out[j]))
            xp, tail = _conv_prompt(xp, jnp.zeros((b, CONV_HALO, d), F32), g[0], *cw, g[1], ts, 32)
            conv_p.append(tail[:, CONV_HALO - CONV_HIST:])
            hist = state_conv[j]
            xs, us = _conv_sample(xs, jnp.swapaxes(hist, 0, 1), g[0], *cw, g[1])
            conv_s.append(jnp.concatenate([hist[:, 1:], us[:, None, :]], axis=1))
        w_in = w_ffn_in[i].astype(BF16)
        w_out = w_ffn_out[i].astype(BF16)
        xp = _ffn(xp.reshape(b * t, d), g[2], w_in, w_out, g[3], tm).reshape(b, t, d)
        xs = _ffn(xs, g[2], w_in, w_out, g[3], bd)

    return (xp, xs.reshape(bd, 1, d),
            jnp.stack(pool_p), jnp.stack(pool_s),
            jnp.stack(swa_p[0]), jnp.stack(swa_p[1]), jnp.stack(swa_p[2]),
            jnp.stack(swa_s[0]), jnp.stack(swa_s[1]), jnp.stack(swa_s[2]),
            jnp.stack(conv_p), jnp.stack(conv_s))
```

```python
import functools
import math

import numpy as np
import jax
import jax.numpy as jnp
from jax import lax
from jax.experimental import pallas as pl
from jax.experimental.pallas import tpu as pltpu

F32 = jnp.float32
BF16 = jnp.bfloat16

RMS_EPS = 1e-6
LN_EPS = 1e-5
NEG_INF = -1e30
PAST_LEN = 8192
POOL_WINDOWS = (2, 4, 8, 16)
POOL_HIST = max(POOL_WINDOWS) - 1
SWA_CONFIGS = ((128, 1), (512, 4), (2048, 16))
N_GROUPS = len(SWA_CONFIGS)
SPAN = 128
BLK = 128
HEAD_DIM = 64
N_BUCKETS = 32
MAX_DISTANCE = 2048
CONV_WIDTH = 31
CONV_HIST = CONV_WIDTH - 1

LANES = 128
SUBLANES = 8
MXU_DIM = 256
VMEM_LIMIT_BYTES = 56 * 1024 * 1024

SAMPLE_KV_BLOCK_BYTES = 8 * 1024 * 1024

POOL_HALO = 16
CONV_HALO = 32
HEAD_PAIR = 2 * HEAD_DIM


def _params(*sem):
    return pltpu.CompilerParams(dimension_semantics=sem, vmem_limit_bytes=VMEM_LIMIT_BYTES)


def _resident(shape):
    nd = len(shape)
    return pl.BlockSpec(shape, lambda *_: (0,) * nd, pipeline_mode=pl.Buffered(1))


def _rms(x, g):
    return x * lax.rsqrt(jnp.mean(x * x, axis=-1, keepdims=True) + RMS_EPS) * g


def _dot(a, b):
    return jnp.dot(a, b, preferred_element_type=F32)


def _t5_bucket(dist):
    max_exact = N_BUCKETS // 2
    d = np.maximum(np.asarray(dist), 0)
    large = max_exact + (np.log(np.maximum(d, max_exact) / max_exact) / math.log(MAX_DISTANCE / max_exact)
                         * (N_BUCKETS - max_exact)).astype(np.int64)
    large = np.minimum(large, N_BUCKETS - 1)
    return np.where(d < max_exact, d, large).astype(np.int32)


def _ffn_body(x_ref, g_pre_ref, w_in_ref, w_out_ref, g_post_ref, o_ref, *, d_ff, chunks):
    x = x_ref[...]
    h = _rms(x, g_pre_ref[...]).astype(BF16)
    acc = None
    for c0, cw in chunks:
        gate = _dot(h, w_in_ref[:, c0:c0 + cw])
        up = _dot(h, w_in_ref[:, d_ff + c0:d_ff + c0 + cw])
        act = (gate * jax.nn.sigmoid(gate) * up).astype(BF16)
        part = _dot(act, w_out_ref[c0:c0 + cw, :])
        acc = part if acc is None else acc + part
    o_ref[...] = x + _rms(acc, g_post_ref[...])


def _ff_chunks(d_ff):
    assert d_ff % MXU_DIM == 0
    n = d_ff // MXU_DIM
    first = (n + 1) // 2 * MXU_DIM
    return ((0, first), (first, d_ff - first)) if d_ff > first else ((0, d_ff),)


def _ffn(x, g_pre, w_in, w_out, g_post, tm):
    n, d = x.shape
    d_ff = w_out.shape[0]
    assert n % tm == 0
    row = pl.BlockSpec((tm, d), lambda i: (i, 0))
    return pl.pallas_call(
        functools.partial(_ffn_body, d_ff=d_ff, chunks=_ff_chunks(d_ff)),
        grid=(n // tm,),
        in_specs=[row, _resident((1, d)), _resident(w_in.shape), _resident(w_out.shape), _resident((1, d))],
        out_specs=row,
        out_shape=jax.ShapeDtypeStruct((n, d), F32),
        compiler_params=_params("parallel"),
        name="ffn",
    )(x, g_pre, w_in, w_out, g_post)


def _pool_groups(h, window_sum, cnt_of, pw_ref, ps_ref):
    gd = h.shape[-1] // len(POOL_WINDOWS)
    outs = []
    for gi, w in enumerate(POOL_WINDOWS):
        c0 = gi * gd
        hg = h[:, c0:c0 + gd]
        dg = window_sum(gi, w, c0, gd, hg) / cnt_of(w) - hg
        outs.append(_dot(dg.astype(BF16), pw_ref[gi]))
    return jnp.concatenate(outs, axis=1) * ps_ref[...]


def _pool_prompt_body(x_ref, xprev_ref, hist_ref, g_pre_ref, pw_ref, ps_ref, g_post_ref,
                      o_ref, tail_ref, buf, *, ts, n_past):
    t = pl.program_id(1)
    x = x_ref[0]
    g_pre = g_pre_ref[...]
    h = _rms(x, g_pre)
    halo = jnp.where(t == 0, hist_ref[0], _rms(xprev_ref[0], g_pre))
    buf[0:POOL_HALO, :] = halo
    buf[POOL_HALO:POOL_HALO + ts, :] = h
    tail_ref[0] = buf[ts:ts + POOL_HALO, :]
    pos1 = n_past + 1 + t * ts + lax.broadcasted_iota(jnp.int32, (ts, 1), 0)

    def window_sum(gi, w, c0, gd, hg):
        win = hg
        for k in range(1, w):
            win = win + buf[POOL_HALO - k:POOL_HALO - k + ts, c0:c0 + gd]
        return win

    y = _pool_groups(h, window_sum, lambda w: jnp.minimum(pos1, w).astype(F32), pw_ref, ps_ref)
    o_ref[0] = x + _rms(y, g_post_ref[...])


def _pool_prompt(x, hist16, g_pre, pw, ps, g_post, ts, n_past):
    b, t, d = x.shape
    assert t % ts == 0 and ts % POOL_HALO == 0
    per = ts // POOL_HALO
    tile = pl.BlockSpec((1, ts, d), lambda i, j: (i, j, 0))
    halo = pl.BlockSpec((1, POOL_HALO, d), lambda i, j: (i, jnp.maximum(j * per - 1, 0), 0))
    first = pl.BlockSpec((1, POOL_HALO, d), lambda i, j: (i, 0, 0))
    return pl.pallas_call(
        functools.partial(_pool_prompt_body, ts=ts, n_past=n_past),
        grid=(b, t // ts),
        in_specs=[tile, halo, first, _resident((1, d)), _resident(pw.shape), _resident((1, d)), _resident((1, d))],
        out_specs=[tile, first],
        out_shape=[jax.ShapeDtypeStruct((b, t, d), F32), jax.ShapeDtypeStruct((b, POOL_HALO, d), F32)],
        scratch_shapes=[pltpu.VMEM((POOL_HALO + ts, d), F32)],
        compiler_params=_params("parallel", "arbitrary"),
        name="pool_prompt",
    )(x, x, hist16, g_pre, pw, ps, g_post)


def _pool_sample_body(x_ref, hist_ref, g_pre_ref, pw_ref, ps_ref, g_post_ref, o_ref, h_ref, *, n_past):
    x = x_ref[...]
    h = _rms(x, g_pre_ref[...])
    h_ref[...] = h

    def window_sum(gi, w, c0, gd, hg):
        win = hg
        for k in range(1, w):
            win = win + hist_ref[POOL_HIST - k, :, c0:c0 + gd]
        return win

    y = _pool_groups(h, window_sum, lambda w: float(min(n_past + 1, w)), pw_ref, ps_ref)
    o_ref[...] = x + _rms(y, g_post_ref[...])


def _pool_sample(x, hist_t, g_pre, pw, ps, g_post, n_past):
    bd, d = x.shape
    return pl.pallas_call(
        functools.partial(_pool_sample_body, n_past=n_past),
        out_shape=[jax.ShapeDtypeStruct((bd, d), F32), jax.ShapeDtypeStruct((bd, d), F32)],
        compiler_params=pltpu.CompilerParams(vmem_limit_bytes=VMEM_LIMIT_BYTES),
        name="pool_sample",
    )(x, hist_t, g_pre, pw, ps, g_post)


def _glu(h, w_in_ref, b_in_ref):
    d = h.shape[-1]
    a = _dot(h, w_in_ref[...]) + b_in_ref[...]
    return a[:, :d] * jax.nn.sigmoid(a[:, d:])


def _conv_tail(c, x, ln_g_ref, ln_b_ref, w_out_ref, b_out_ref, g_post_ref):
    mu = jnp.mean(c, axis=-1, keepdims=True)
    cc = c - mu
    var = jnp.mean(cc * cc, axis=-1, keepdims=True)
    z = cc * lax.rsqrt(var + LN_EPS) * ln_g_ref[...] + ln_b_ref[...]
    y = _dot((z * jax.nn.sigmoid(z)).astype(BF16), w_out_ref[...]) + b_out_ref[...]
    return x + _rms(y, g_post_ref[...])


def _conv_prompt_body(x_ref, hist_ref, g_pre_ref, w_in_ref, b_in_ref, w_dw_ref, b_dw_ref, ln_g_ref, ln_b_ref,
                      w_out_ref, b_out_ref, g_post_ref, o_ref, tail_ref, ubuf, cbuf, *, ts, rc):
    t = pl.program_id(1)
    d = x_ref.shape[-1]

    @pl.when(t == 0)
    def _():
        ubuf[0:CONV_HALO, :] = hist_ref[0]

    @pl.when(t > 0)
    def _():
        ubuf[0:CONV_HALO, :] = ubuf[ts:ts + CONV_HALO, :]

    x = x_ref[0]
    h = _rms(x, g_pre_ref[...]).astype(BF16)
    ubuf[CONV_HALO:CONV_HALO + ts, :] = _glu(h, w_in_ref, b_in_ref)
    tail_ref[0] = ubuf[ts:ts + CONV_HALO, :]

    lead = CONV_HALO - CONV_HIST

    def chunk(i, carry):
        base = pl.multiple_of(i * rc, rc)
        win = ubuf[pl.ds(base, rc + CONV_HALO), :]
        acc = jnp.broadcast_to(b_dw_ref[...], (rc, d))
        for phase in range(SUBLANES):
            offs = [o for o in range(lead, lead + CONV_WIDTH) if o % SUBLANES == phase]
            shifted = win[phase:phase + offs[-1] - phase + rc, :]
            for o in offs:
                acc = acc + shifted[o - phase:o - phase + rc, :] * w_dw_ref[o - lead:o - lead + 1, :]
        cbuf[pl.ds(base, rc), :] = acc
        return carry

    lax.fori_loop(0, ts // rc, chunk, 0)
    o_ref[0] = _conv_tail(cbuf[...], x, ln_g_ref, ln_b_ref, w_out_ref, b_out_ref, g_post_ref)


def _conv_prompt(x, hist32, g_pre, w_in, b_in, w_dw, b_dw, ln_g, ln_b, w_out, b_out, g_post, ts, rc):
    b, t, d = x.shape
    assert t % ts == 0 and ts % rc == 0 and ts >= CONV_HALO
    tile = pl.BlockSpec((1, ts, d), lambda i, j: (i, j, 0))
    first = pl.BlockSpec((1, CONV_HALO, d), lambda i, j: (i, 0, 0))
    vec = _resident((1, d))
    return pl.pallas_call(
        functools.partial(_conv_prompt_body, ts=ts, rc=rc),
        grid=(b, t // ts),
        in_specs=[tile, first, vec, _resident(w_in.shape), _resident(b_in.shape), _resident(w_dw.shape), vec, vec,
                  vec, _resident(w_out.shape), vec, vec],
        out_specs=[tile, first],
        out_shape=[jax.ShapeDtypeStruct((b, t, d), F32), jax.ShapeDtypeStruct((b, CONV_HALO, d), F32)],
        scratch_shapes=[pltpu.VMEM((CONV_HALO + ts, d), F32), pltpu.VMEM((ts, d), F32)],
        compiler_params=_params("parallel", "arbitrary"),
        name="conv_prompt",
    )(x, hist32, g_pre, w_in, b_in, w_dw, b_dw, ln_g, ln_b, w_out, b_out, g_post)


def _conv_sample_body(x_ref, hist_ref, g_pre_ref, w_in_ref, b_in_ref, w_dw_ref, b_dw_ref, ln_g_ref, ln_b_ref,
                      w_out_ref, b_out_ref, g_post_ref, o_ref, u_ref):
    x = x_ref[...]
    h = _rms(x, g_pre_ref[...]).astype(BF16)
    u = _glu(h, w_in_ref, b_in_ref)
    u_ref[...] = u
    c = b_dw_ref[...] + u * w_dw_ref[CONV_HIST:CONV_WIDTH, :]
    for k in range(CONV_HIST):
        c = c + hist_ref[k] * w_dw_ref[k:k + 1, :]
    o_ref[...] = _conv_tail(c, x, ln_g_ref, ln_b_ref, w_out_ref, b_out_ref, g_post_ref)


def _conv_sample(x, hist_t, g_pre, w_in, b_in, w_dw, b_dw, ln_g, ln_b, w_out, b_out, g_post):
    bd, d = x.shape
    return pl.pallas_call(
        _conv_sample_body,
        out_shape=[jax.ShapeDtypeStruct((bd, d), F32), jax.ShapeDtypeStruct((bd, d), F32)],
        compiler_params=pltpu.CompilerParams(vmem_limit_bytes=VMEM_LIMIT_BYTES),
        name="conv_sample",
    )(x, hist_t, g_pre, w_in, b_in, w_dw, b_dw, ln_g, ln_b, w_out, b_out, g_post)


def _kv_plan(t, tm):
    plan = []
    for win, _ in SWA_CONFIGS:
        keep = min(win, t)
        rows = min(keep, tm)
        assert keep % rows == 0 and t % tm == 0
        plan.append((rows, t // tm - keep // rows if keep >= tm else t // tm - 1))
    return tuple(plan)


def _qkv_prompt_body(x_ref, g_pre_ref, w_ref, q0, q1, q2, kv0, kv1, kv2, rbuf, *, tm, plan):
    t = pl.program_id(1)
    d = x_ref.shape[-1]
    n_slab = d // LANES
    h = _rms(x_ref[0], g_pre_ref[...]).astype(BF16)
    qkv_refs = (q0, q1, q2)
    kv_refs = (kv0, kv1, kv2)
    for g, (_, dil) in enumerate(SWA_CONFIGS):
        rows, first = plan[g]
        for part in range(3):
            c0 = (g * 3 + part) * d
            r = _dot(h, w_ref[:, c0:c0 + d])
            if part == 0:
                r = r * (HEAD_DIM ** -0.5)
            if part > 0:
                @pl.when(t >= first)
                def _(r=r, g=g, part=part, rows=rows):
                    kv_refs[g][0, :, (part - 1) * d:part * d] = r[tm - rows:, :]
            if dil == 1:
                qkv_refs[g][0, 0, :, part * d:(part + 1) * d] = r.astype(BF16)
            else:
                for s in range(n_slab):
                    rbuf[s] = r[:, s * LANES:(s + 1) * LANES]
                for res in range(dil):
                    piece = jnp.concatenate(
                        [rbuf[s, pl.ds(res, tm // dil, stride=dil), :] for s in range(n_slab)], axis=1)
                    qkv_refs[g][0, res, :, part * d:(part + 1) * d] = piece.astype(BF16)


def _qkv_prompt(x, g_pre, w_qkv, tm):
    b, t, d = x.shape
    plan = _kv_plan(t, tm)
    tile = pl.BlockSpec((1, tm, d), lambda i, j: (i, j, 0))
    qkv_specs, qkv_shapes, kv_specs, kv_shapes = [], [], [], []
    for g, (win, dil) in enumerate(SWA_CONFIGS):
        rows, first = plan[g]
        assert tm % (dil * 16) == 0
        qkv_specs.append(pl.BlockSpec((1, dil, tm // dil, 3 * d), lambda i, j: (i, 0, j, 0)))
        qkv_shapes.append(jax.ShapeDtypeStruct((b, dil, t // dil, 3 * d), BF16))
        kv_specs.append(pl.BlockSpec((1, rows, 2 * d), lambda i, j, first=first: (i, jnp.maximum(j - first, 0), 0)))
        kv_shapes.append(jax.ShapeDtypeStruct((b, min(win, t), 2 * d), F32))
    return pl.pallas_call(
        functools.partial(_qkv_prompt_body, tm=tm, plan=plan),
        grid=(b, t // tm),
        in_specs=[tile, _resident((1, d)), _resident(w_qkv.shape)],
        out_specs=qkv_specs + kv_specs,
        out_shape=qkv_shapes + kv_shapes,
        scratch_shapes=[pltpu.VMEM((d // LANES, tm, LANES), F32)],
        compiler_params=_params("parallel", "arbitrary"),
        name="qkv_prompt",
    )(x, g_pre, w_qkv)


def _qkv_sample_body(x_ref, g_pre_ref, w_ref, o_ref):
    h = _rms(x_ref[...], g_pre_ref[...]).astype(BF16)
    o_ref[...] = _dot(h, w_ref[...])


def _qkv_sample(x, g_pre, w_qkv):
    bd, _ = x.shape
    return pl.pallas_call(
        _qkv_sample_body,
        out_shape=jax.ShapeDtypeStruct((bd, w_qkv.shape[1]), F32),
        compiler_params=pltpu.CompilerParams(vmem_limit_bytes=VMEM_LIMIT_BYTES),
        name="qkv_sample",
    )(x, g_pre, w_qkv)


def _attn_prompt_body(*refs, dil, chained, last):
    refs = list(refs)
    q_ref, kc_ref, vc_ref, kp_ref, vp_ref, bias_ref = refs[:6]
    del refs[:6]
    if chained:
        op_ref, mp_ref, sp_ref = refs[:3]
        del refs[:3]
    o_ref = refs[0]
    if not last:
        m_ref, s_ref = refs[1:3]
    n = pl.program_id(1)
    res = pl.program_id(2)
    rows = pl.ds(res, BLK, stride=dil) if dil > 1 else slice(None)

    kj = lax.broadcasted_iota(jnp.int32, (2 * BLK, 2 * BLK), 0)
    qi = lax.broadcasted_iota(jnp.int32, (2 * BLK, 2 * BLK), 1) % BLK
    delta = qi + BLK - kj
    first_key = jnp.where(n > 0, 0, BLK)
    valid = (delta >= 0) & (delta <= SPAN) & (kj >= first_key)
    lane = lax.broadcasted_iota(jnp.int32, (BLK, HEAD_PAIR), 1)
    low = lane < HEAD_DIM

    q = q_ref[0, 0]
    k = jnp.concatenate([kp_ref[0, 0], kc_ref[0, 0]], axis=0)
    v = jnp.concatenate([vp_ref[0, 0], vc_ref[0, 0]], axis=0)
    n_pairs = q.shape[-1] // HEAD_PAIR
    head_row = lax.broadcasted_iota(jnp.int32, (2 * n_pairs, BLK), 0)
    if chained:
        m_prev_t = mp_ref[0, rows, :].T
        s_prev_t = sp_ref[0, rows, :].T
    m_out_t = jnp.zeros((2 * n_pairs, BLK), F32)
    s_out_t = jnp.zeros((2 * n_pairs, BLK), F32)
    for pair in range(n_pairs):
        sl = slice(pair * HEAD_PAIR, (pair + 1) * HEAD_PAIR)
        q2, k2, v2 = q[:, sl], k[:, sl], v[:, sl]
        qq = jnp.concatenate([jnp.where(low, q2, jnp.zeros_like(q2)), jnp.where(low, jnp.zeros_like(q2), q2)], axis=0)
        st = lax.dot_general(k2, qq, (((1,), (1,)), ((), ())), preferred_element_type=F32) + bias_ref[pair]
        st = jnp.where(valid, st, NEG_INF)
        m_new = jnp.max(st, axis=0, keepdims=True)
        if chained:
            m_prev = jnp.concatenate([m_prev_t[2 * pair:2 * pair + 1], m_prev_t[2 * pair + 1:2 * pair + 2]], axis=1)
            m_new = jnp.maximum(m_new, m_prev)
        p = jnp.exp(st - m_new)
        s_new = jnp.sum(p, axis=0, keepdims=True)
        pv = lax.dot_general(v2, p.astype(BF16), (((0,), (0,)), ((), ())), preferred_element_type=F32)
        if chained:
            alpha = jnp.exp(m_prev - m_new)
            s_prev = jnp.concatenate([s_prev_t[2 * pair:2 * pair + 1], s_prev_t[2 * pair + 1:2 * pair + 2]], axis=1)
            s_new = s_prev * alpha + s_new
        o_t = [pv[:HEAD_DIM, :BLK], pv[HEAD_DIM:, BLK:]]
        if chained:
            o_prev_t = op_ref[0, pair, rows, :].T
        for half in range(2):
            cols = slice(half * BLK, (half + 1) * BLK)
            if chained:
                o_t[half] = o_prev_t[half * HEAD_DIM:(half + 1) * HEAD_DIM] * alpha[:, cols] + o_t[half]
            if last:
                o_t[half] = o_t[half] / s_new[:, cols]
            else:
                m_out_t = jnp.where(head_row == 2 * pair + half, m_new[:, cols], m_out_t)
                s_out_t = jnp.where(head_row == 2 * pair + half, s_new[:, cols], s_out_t)
        o_ref[0, pair, rows, :] = jnp.concatenate(o_t, axis=0).T
    if not last:
        pad = jnp.zeros((BLK - 2 * n_pairs, BLK), F32)
        m_ref[0, rows, :] = jnp.concatenate([m_out_t, pad], axis=0).T
        s_ref[0, rows, :] = jnp.concatenate([s_out_t, pad], axis=0).T


def _attn_prompt(qkv, bias, dil, prev, last):
    b, _, sub, d3 = qkv.shape
    d = d3 // 3
    t = sub * dil
    assert sub % BLK == 0 and d % HEAD_PAIR == 0 and d // HEAD_DIM <= LANES
    n_pairs = d // HEAD_PAIR
    span = BLK * dil

    def col(part):
        return pl.BlockSpec((1, 1, BLK, d), lambda i, n, r: (i, r, n, part))

    def col_prev(part):
        return pl.BlockSpec((1, 1, BLK, d), lambda i, n, r: (i, r, jnp.maximum(n - 1, 0), part))

    o_spec = pl.BlockSpec((1, n_pairs, span, HEAD_PAIR), lambda i, n, r: (i, 0, n, 0))
    st_spec = pl.BlockSpec((1, span, LANES), lambda i, n, r: (i, n, 0))
    in_specs = [col(0), col(1), col(2), col_prev(1), col_prev(2), _resident(bias.shape)]
    args = [qkv, qkv, qkv, qkv, qkv, bias]
    if prev is not None:
        in_specs += [o_spec, st_spec, st_spec]
        args += list(prev)
    o_shape = jax.ShapeDtypeStruct((b, n_pairs, t, HEAD_PAIR), F32)
    st_shape = jax.ShapeDtypeStruct((b, t, LANES), F32)
    return pl.pallas_call(
        functools.partial(_attn_prompt_body, dil=dil, chained=prev is not None, last=last),
        grid=(b, sub // BLK, dil),
        in_specs=in_specs,
        out_specs=[o_spec] if last else [o_spec, st_spec, st_spec],
        out_shape=[o_shape] if last else [o_shape, st_shape, st_shape],
        compiler_params=_params("parallel", "arbitrary", "arbitrary"),
        name=f"attn_prompt_d{dil}",
    )(*args)


def _bias_table(rel_bias, g, dil):
    onehot = np.eye(N_BUCKETS, dtype=np.float32)[_t5_bucket(np.arange(SPAN + 1) * dil)]
    return jnp.dot(rel_bias[:, g, :].astype(F32).T, jnp.asarray(onehot).T, precision=lax.Precision.HIGHEST)


def _prompt_bias(table):
    nh = table.shape[0]
    width = 3 * BLK - 1
    rev = jnp.concatenate([jnp.broadcast_to(table[:, SPAN:], (nh, BLK - 1)), table[:, ::-1],
                           jnp.broadcast_to(table[:, :1], (nh, BLK - 1)), jnp.zeros((nh, 1), F32)], axis=1)
    skew = jnp.tile(rev, (1, BLK))[:, :BLK * width].reshape(nh, BLK, width)
    bias = skew[:, :, BLK - 1:]
    bias_t = jnp.transpose(bias, (0, 2, 1)).reshape(nh // 2, 2, 2 * BLK, BLK)
    return jnp.transpose(bias_t, (0, 2, 1, 3)).reshape(nh // 2, 2 * BLK, 2 * BLK)


def _attn_sample_body(q_ref, kn_ref, vn_ref, kc_ref, vc_ref, bias_c_ref, bias_n_ref, o_ref, lse_ref, *, dil):
    nh, _, n_cache = bias_c_ref.shape
    pos = lax.broadcasted_iota(jnp.int32, (1, n_cache), 1)
    on_grid = pos % dil == 0
    head = lax.broadcasted_iota(jnp.int32, (1, nh), 1)

    def one_seq(i, carry):
        q_all = q_ref[i] * (HEAD_DIM ** -0.5)
        ln_all = jnp.sum(kn_ref[i] * q_all, axis=0, keepdims=True) + bias_n_ref[...]
        vn_all = vn_ref[i]
        o_all = jnp.zeros_like(q_all)
        lse_all = jnp.zeros_like(ln_all)
        for h in range(nh):
            lc = jnp.sum(kc_ref[i, h] * q_all[:, h:h + 1], axis=0, keepdims=True) + bias_c_ref[h]
            lc = jnp.where(on_grid, lc, NEG_INF)
            ln = ln_all[:, h:h + 1]
            m = jnp.maximum(jnp.max(lc, axis=1, keepdims=True), ln)
            pc = jnp.exp(lc - m)
            pn = jnp.exp(ln - m)
            s = jnp.sum(pc, axis=1, keepdims=True) + pn
            o = jnp.sum(vc_ref[i, h] * pc, axis=1, keepdims=True) + pn * vn_all[:, h:h + 1]
            o_all = jnp.where(head == h, o / s, o_all)
            lse_all = jnp.where(head == h, m + jnp.log(s), lse_all)
        o_ref[i] = o_all
        lse_ref[i] = lse_all
        return carry

    lax.fori_loop(0, q_ref.shape[0], one_seq, 0)


def _attn_sample(q, kn, vn, cache, layer, bias_c, bias_n, dil, bb):
    bd, dh, nh = q.shape
    n_cache = cache.shape[2]
    assert n_cache == SPAN * dil and bd % bb == 0
    cache_t = jnp.transpose(cache, (0, 1, 3, 4, 5, 2))
    col = pl.BlockSpec((bb, dh, nh), lambda i: (i, 0, 0))

    def kv(which):
        return pl.BlockSpec((None, bb, None, nh, dh, n_cache), lambda i: (layer, i, which, 0, 0, 0))

    return pl.pallas_call(
        functools.partial(_attn_sample_body, dil=dil),
        grid=(bd // bb,),
        in_specs=[col, col, col, kv(0), kv(1), _resident(bias_c.shape), _resident(bias_n.shape)],
        out_specs=[col, pl.BlockSpec((bb, 1, nh), lambda i: (i, 0, 0))],
        out_shape=[jax.ShapeDtypeStruct((bd, dh, nh), F32), jax.ShapeDtypeStruct((bd, 1, nh), F32)],
        compiler_params=_params("parallel"),
        name=f"attn_sample_d{dil}",
    )(q, kn, vn, cache_t, cache_t, bias_c, bias_n)


def _merge_sample_body(o0_ref, o1_ref, o2_ref, l0_ref, l1_ref, l2_ref, out_ref):
    lses = (l0_ref[...], l1_ref[...], l2_ref[...])
    outs = (o0_ref[...], o1_ref[...], o2_ref[...])
    mx = jnp.maximum(jnp.maximum(lses[0], lses[1]), lses[2])
    es = [jnp.exp(l - mx) for l in lses]
    tot = es[0] + es[1] + es[2]
    out_ref[...] = sum((e / tot) * o for e, o in zip(es, outs))


def _merge_sample(outs, lses):
    return pl.pallas_call(
        _merge_sample_body,
        out_shape=jax.ShapeDtypeStruct(outs[0].shape, F32),
        name="merge_sample",
    )(*outs, *lses)


def _wo_prompt_body(x_ref, o_ref_in, w_ref, g_post_ref, out_ref):
    o = jnp.concatenate([o_ref_in[0, p] for p in range(o_ref_in.shape[1])], axis=1)
    y = _dot(o.astype(BF16), w_ref[...])
    out_ref[0] = x_ref[0] + _rms(y, g_post_ref[...])


def _wo_prompt(x, o, w_o, g_post, tm):
    b, t, d = x.shape
    row = pl.BlockSpec((1, tm, d), lambda i, j: (i, j, 0))
    return pl.pallas_call(
        _wo_prompt_body,
        grid=(b, t // tm),
        in_specs=[row, pl.BlockSpec((1, o.shape[1], tm, HEAD_PAIR), lambda i, j: (i, 0, j, 0)),
                  _resident(w_o.shape), _resident((1, d))],
        out_specs=row,
        out_shape=jax.ShapeDtypeStruct((b, t, d), F32),
        compiler_params=_params("parallel", "parallel"),
        name="wo_prompt",
    )(x, o, w_o, g_post)


def _wo_sample_body(x_ref, o_ref_in, w_ref, g_post_ref, out_ref):
    y = _dot(o_ref_in[...].astype(BF16), w_ref[...])
    out_ref[...] = x_ref[...] + _rms(y, g_post_ref[...])


def _wo_sample(x, o, w_o, g_post):
    return pl.pallas_call(
        _wo_sample_body,
        out_shape=jax.ShapeDtypeStruct(x.shape, F32),
        compiler_params=pltpu.CompilerParams(vmem_limit_bytes=VMEM_LIMIT_BYTES),
        name="wo_sample",
    )(x, o, w_o, g_post)


def kernel(x_prompt, x_sample, state_pool, cache_swa_g0, cache_swa_g1, cache_swa_g2, state_conv,
           norm_g, w_ffn_in, w_ffn_out, pool_w, pool_scale, w_qkv, w_o, rel_bias,
           conv_w_in, conv_b_in, conv_w_dw, conv_b_dw, conv_ln_g, conv_ln_b, conv_w_out, conv_b_out):
    b, t, d = x_prompt.shape
    bd, t_dec, _ = x_sample.shape
    assert t_dec == 1 and d % HEAD_PAIR == 0
    nh = d // HEAD_DIM
    depth = norm_g.shape[0]
    caches = (cache_swa_g0, cache_swa_g1, cache_swa_g2)

    tm = 512 if (b * t) % 512 == 0 else b * t
    ts = min(512, t)
    tq = min(256, t)

    xp = x_prompt
    xs = x_sample.reshape(bd, d)
    vec = lambda a: a.reshape(1, -1).astype(F32)

    pool_p, pool_s, conv_p, conv_s = [], [], [], []
    swa_p = [[] for _ in range(N_GROUPS)]
    swa_s = [[] for _ in range(N_GROUPS)]
    for i in range(depth):
        kind, j = i % 3, i // 3
        g = [vec(norm_g[i, k]) for k in range(4)]
        if kind == 0:
            pw = pool_w[j].astype(BF16)
            ps = vec(pool_scale[j])
            xp, tail = _pool_prompt(xp, jnp.zeros((b, POOL_HALO, d), F32), g[0], pw, ps, g[1], ts, 0)
            pool_p.append(tail[:, POOL_HALO - POOL_HIST:])
            hist = state_pool[j]
            xs, hs = _pool_sample(xs, jnp.swapaxes(hist, 0, 1), g[0], pw, ps, g[1], PAST_LEN)
            pool_s.append(jnp.concatenate([hist[:, 1:], hs[:, None, :]], axis=1))
        elif kind == 1:
            wq = w_qkv[j].astype(BF16)
            wo = w_o[j].astype(BF16)
            outs = _qkv_prompt(xp, g[0], wq, tq)
            tables = [_bias_table(rel_bias, gi, dil) for gi, (_, dil) in enumerate(SWA_CONFIGS)]
            order = sorted(range(N_GROUPS), key=lambda gi: -SWA_CONFIGS[gi][1])
            prev = None
            for k, gi in enumerate(order):
                prev = _attn_prompt(outs[gi], _prompt_bias(tables[gi]), SWA_CONFIGS[gi][1], prev, k == N_GROUPS - 1)
            for gi in range(N_GROUPS):
                swa_p[gi].append(outs[N_GROUPS + gi].reshape(b, -1, 2, nh, HEAD_DIM))
            xp = _wo_prompt(xp, prev[0], wo, g[1], ts)

            qkv_s = _qkv_sample(xs, g[0], wq).reshape(bd, N_GROUPS, 3, nh, HEAD_DIM)
            o_s, lse_s = [], []
            for gi, (_, dil) in enumerate(SWA_CONFIGS):
                bias_c = jnp.repeat(tables[gi][:, :0:-1], dil, axis=1)[:, None, :]
                bias_n = jnp.transpose(tables[gi][:, :1])
                per_seq = d * SPAN * dil * 4
                bb = max(1, min(bd, SAMPLE_KV_BLOCK_BYTES // per_seq))
                q_s, k_s, v_s = (jnp.swapaxes(qkv_s[:, gi, p], 1, 2) for p in range(3))
                o_g, l_g = _attn_sample(q_s, k_s, v_s, caches[gi], j, bias_c, bias_n, dil, bb)
                o_s.append(o_g)
                lse_s.append(l_g)
                swa_s[gi].append(qkv_s[:, gi, 1:3].reshape(bd, 1, 2, nh, HEAD_DIM))
            merged = jnp.swapaxes(_merge_sample(o_s, lse_s), 1, 2).reshape(bd, d)
            xs = _wo_sample(xs, merged, wo, g[1])
        else:
            cw = (conv_w_in[j].astype(BF16), vec(conv_b_in[j]),
                  jnp.pad(conv_w_dw[j], ((0, CONV_HALO - CONV_WIDTH), (0, 0))), vec(conv_b_dw[j]),
                  vec(conv_ln_g[j]), vec(conv_ln_b[j]), conv_w_out[j].astype(BF16), vec(conv_b_out[j]))
            xp, tail = _conv_prompt(xp, jnp.zeros((b, CONV_HALO, d), F32), g[0], *cw, g[1], ts, 32)
            conv_p.append(tail[:, CONV_HALO - CONV_HIST:])
            hist = state_conv[j]
            xs, us = _conv_sample(xs, jnp.swapaxes(hist, 0, 1), g[0], *cw, g[1])
            conv_s.append(jnp.concatenate([hist[:, 1:], us[:, None, :]], axis=1))
        w_in = w_ffn_in[i].astype(BF16)
        w_out = w_ffn_out[i].astype(BF16)
        xp = _ffn(xp.reshape(b * t, d), g[2], w_in, w_out, g[3], tm).reshape(b, t, d)
        xs = _ffn(xs, g[2], w_in, w_out, g[3], bd)

    return (xp, xs.reshape(bd, 1, d),
            jnp.stack(pool_p), jnp.stack(pool_s),
            jnp.stack(swa_p[0]), jnp.stack(swa_p[1]), jnp.stack(swa_p[2]),
            jnp.stack(swa_s[0]), jnp.stack(swa_s[1]), jnp.stack(swa_s[2]),
            jnp.stack(conv_p), jnp.stack(conv_s))
```

```python
import functools
import math

import numpy as np
import jax
import jax.numpy as jnp
from jax import lax
from jax.experimental import pallas as pl
from jax.experimental.pallas import tpu as pltpu

F32 = jnp.float32
BF16 = jnp.bfloat16

RMS_EPS = 1e-6
LN_EPS = 1e-5
NEG_INF = -1e30
PAST_LEN = 8192
POOL_WINDOWS = (2, 4, 8, 16)
POOL_HIST = max(POOL_WINDOWS) - 1
SWA_CONFIGS = ((128, 1), (512, 4), (2048, 16))
N_GROUPS = len(SWA_CONFIGS)
SPAN = 128
BLK = 128
HEAD_DIM = 64
N_BUCKETS = 32
MAX_DISTANCE = 2048
CONV_WIDTH = 31
CONV_HIST = CONV_WIDTH - 1

LANES = 128
SUBLANES = 8
MXU_DIM = 256
VMEM_LIMIT_BYTES = 56 * 1024 * 1024
REGROUP_STRIDE = 4

SAMPLE_KV_BLOCK_BYTES = 8 * 1024 * 1024

POOL_HALO = 16
CONV_HALO = 32
HEAD_PAIR = 2 * HEAD_DIM


def _params(*sem):
    return pltpu.CompilerParams(dimension_semantics=sem, vmem_limit_bytes=VMEM_LIMIT_BYTES)


def _resident(shape):
    nd = len(shape)
    return pl.BlockSpec(shape, lambda *_: (0,) * nd, pipeline_mode=pl.Buffered(1))


def _rms(x, g):
    return x * lax.rsqrt(jnp.mean(x * x, axis=-1, keepdims=True) + RMS_EPS) * g


def _dot(a, b):
    return jnp.dot(a, b, preferred_element_type=F32)


def _t5_bucket(dist):
    max_exact = N_BUCKETS // 2
    d = np.maximum(np.asarray(dist), 0)
    large = max_exact + (np.log(np.maximum(d, max_exact) / max_exact) / math.log(MAX_DISTANCE / max_exact)
                         * (N_BUCKETS - max_exact)).astype(np.int64)
    large = np.minimum(large, N_BUCKETS - 1)
    return np.where(d < max_exact, d, large).astype(np.int32)


def _ff_chunks(d_ff):
    assert d_ff % MXU_DIM == 0
    n = d_ff // MXU_DIM
    first = (n + 1) // 2 * MXU_DIM
    return ((0, first), (first, d_ff - first)) if d_ff > first else ((0, d_ff),)


def _ffn_tile(x, g_pre_ref, w_in_ref, w_out_ref, g_post_ref):
    d_ff = w_out_ref.shape[0]
    h = _rms(x, g_pre_ref[...]).astype(BF16)
    acc = None
    for c0, cw in _ff_chunks(d_ff):
        gate = _dot(h, w_in_ref[:, c0:c0 + cw])
        up = _dot(h, w_in_ref[:, d_ff + c0:d_ff + c0 + cw])
        act = (gate * jax.nn.sigmoid(gate) * up).astype(BF16)
        part = _dot(act, w_out_ref[c0:c0 + cw, :])
        acc = part if acc is None else acc + part
    return x + _rms(acc, g_post_ref[...])


def _ffn_specs(ffn):
    return [_resident(a.shape) for a in ffn]


def _ffn_body(x_ref, g_pre_ref, w_in_ref, w_out_ref, g_post_ref, o_ref):
    o_ref[...] = _ffn_tile(x_ref[...], g_pre_ref, w_in_ref, w_out_ref, g_post_ref)


def _ffn(x, ffn, tm):
    n, d = x.shape
    assert n % tm == 0
    row = pl.BlockSpec((tm, d), lambda i: (i, 0))
    return pl.pallas_call(
        _ffn_body,
        grid=(n // tm,),
        in_specs=[row] + _ffn_specs(ffn),
        out_specs=row,
        out_shape=jax.ShapeDtypeStruct((n, d), F32),
        compiler_params=_params("parallel"),
        name="ffn",
    )(x, *ffn)


def _pool_groups(h, window_sum, cnt_of, pw_ref, ps_ref):
    gd = h.shape[-1] // len(POOL_WINDOWS)
    outs = []
    for gi, w in enumerate(POOL_WINDOWS):
        c0 = gi * gd
        hg = h[:, c0:c0 + gd]
        dg = window_sum(gi, w, c0, gd, hg) / cnt_of(w) - hg
        outs.append(_dot(dg.astype(BF16), pw_ref[gi]))
    return jnp.concatenate(outs, axis=1) * ps_ref[...]


def _pool_prompt_body(x_ref, xprev_ref, hist_ref, g_pre_ref, pw_ref, ps_ref, g_post_ref, f0, f1, f2, f3,
                      o_ref, tail_ref, buf, *, ts, n_past):
    t = pl.program_id(1)
    x = x_ref[0]
    g_pre = g_pre_ref[...]
    h = _rms(x, g_pre)
    halo = jnp.where(t == 0, hist_ref[0], _rms(xprev_ref[0], g_pre))
    buf[0:POOL_HALO, :] = halo
    buf[POOL_HALO:POOL_HALO + ts, :] = h
    tail_ref[0] = buf[ts:ts + POOL_HALO, :]
    pos1 = n_past + 1 + t * ts + lax.broadcasted_iota(jnp.int32, (ts, 1), 0)

    def window_sum(gi, w, c0, gd, hg):
        win = hg
        for k in range(1, w):
            win = win + buf[POOL_HALO - k:POOL_HALO - k + ts, c0:c0 + gd]
        return win

    y = _pool_groups(h, window_sum, lambda w: jnp.minimum(pos1, w).astype(F32), pw_ref, ps_ref)
    o_ref[0] = _ffn_tile(x + _rms(y, g_post_ref[...]), f0, f1, f2, f3)


def _pool_prompt(x, hist16, g_pre, pw, ps, g_post, ffn, ts, n_past):
    b, t, d = x.shape
    assert t % ts == 0 and ts % POOL_HALO == 0
    per = ts // POOL_HALO
    tile = pl.BlockSpec((1, ts, d), lambda i, j: (i, j, 0))
    halo = pl.BlockSpec((1, POOL_HALO, d), lambda i, j: (i, jnp.maximum(j * per - 1, 0), 0))
    first = pl.BlockSpec((1, POOL_HALO, d), lambda i, j: (i, 0, 0))
    return pl.pallas_call(
        functools.partial(_pool_prompt_body, ts=ts, n_past=n_past),
        grid=(b, t // ts),
        in_specs=[tile, halo, first, _resident((1, d)), _resident(pw.shape), _resident((1, d)), _resident((1, d))]
        + _ffn_specs(ffn),
        out_specs=[tile, first],
        out_shape=[jax.ShapeDtypeStruct((b, t, d), F32), jax.ShapeDtypeStruct((b, POOL_HALO, d), F32)],
        scratch_shapes=[pltpu.VMEM((POOL_HALO + ts, d), F32)],
        compiler_params=_params("parallel", "arbitrary"),
        name="pool_ffn_prompt",
    )(x, x, hist16, g_pre, pw, ps, g_post, *ffn)


def _pool_sample_body(x_ref, hist_ref, g_pre_ref, pw_ref, ps_ref, g_post_ref, o_ref, h_ref, *, n_past):
    x = x_ref[...]
    h = _rms(x, g_pre_ref[...])
    h_ref[...] = h

    def window_sum(gi, w, c0, gd, hg):
        win = hg
        for k in range(1, w):
            win = win + hist_ref[POOL_HIST - k, :, c0:c0 + gd]
        return win

    y = _pool_groups(h, window_sum, lambda w: float(min(n_past + 1, w)), pw_ref, ps_ref)
    o_ref[...] = x + _rms(y, g_post_ref[...])


def _pool_sample(x, hist_t, g_pre, pw, ps, g_post, n_past):
    bd, d = x.shape
    return pl.pallas_call(
        functools.partial(_pool_sample_body, n_past=n_past),
        out_shape=[jax.ShapeDtypeStruct((bd, d), F32), jax.ShapeDtypeStruct((bd, d), F32)],
        compiler_params=pltpu.CompilerParams(vmem_limit_bytes=VMEM_LIMIT_BYTES),
        name="pool_sample",
    )(x, hist_t, g_pre, pw, ps, g_post)


def _glu(h, w_in_ref, b_in_ref):
    d = h.shape[-1]
    a = _dot(h, w_in_ref[...]) + b_in_ref[...]
    return a[:, :d] * jax.nn.sigmoid(a[:, d:])


def _conv_tail(c, x, ln_g_ref, ln_b_ref, w_out_ref, b_out_ref, g_post_ref):
    mu = jnp.mean(c, axis=-1, keepdims=True)
    cc = c - mu
    var = jnp.mean(cc * cc, axis=-1, keepdims=True)
    z = cc * lax.rsqrt(var + LN_EPS) * ln_g_ref[...] + ln_b_ref[...]
    y = _dot((z * jax.nn.sigmoid(z)).astype(BF16), w_out_ref[...]) + b_out_ref[...]
    return x + _rms(y, g_post_ref[...])


def _conv_prompt_body(x_ref, hist_ref, g_pre_ref, w_in_ref, b_in_ref, w_dw_ref, b_dw_ref, ln_g_ref, ln_b_ref,
                      w_out_ref, b_out_ref, g_post_ref, o_ref, tail_ref, ubuf, cbuf, *, ts, rc):
    t = pl.program_id(1)
    d = x_ref.shape[-1]

    @pl.when(t == 0)
    def _():
        ubuf[0:CONV_HALO, :] = hist_ref[0]

    @pl.when(t > 0)
    def _():
        ubuf[0:CONV_HALO, :] = ubuf[ts:ts + CONV_HALO, :]

    x = x_ref[0]
    h = _rms(x, g_pre_ref[...]).astype(BF16)
    ubuf[CONV_HALO:CONV_HALO + ts, :] = _glu(h, w_in_ref, b_in_ref)
    tail_ref[0] = ubuf[ts:ts + CONV_HALO, :]

    lead = CONV_HALO - CONV_HIST

    def chunk(i, carry):
        base = pl.multiple_of(i * rc, rc)
        for l0 in range(0, d, LANES):
            win = ubuf[pl.ds(base, rc + CONV_HALO), l0:l0 + LANES]
            acc = jnp.broadcast_to(b_dw_ref[:, l0:l0 + LANES], (rc, LANES))
            for phase in range(SUBLANES):
                offs = [o for o in range(lead, lead + CONV_WIDTH) if o % SUBLANES == phase]
                shifted = win if phase == 0 else pltpu.roll(win, rc + CONV_HALO - phase, axis=0)
                for o in offs:
                    acc = acc + shifted[o - phase:o - phase + rc, :] * w_dw_ref[o - lead:o - lead + 1, l0:l0 + LANES]
            cbuf[pl.ds(base, rc), l0:l0 + LANES] = acc
        return carry

    lax.fori_loop(0, ts // rc, chunk, 0)
    o_ref[0] = _conv_tail(cbuf[...], x, ln_g_ref, ln_b_ref, w_out_ref, b_out_ref, g_post_ref)


def _conv_prompt(x, hist32, g_pre, w_in, b_in, w_dw, b_dw, ln_g, ln_b, w_out, b_out, g_post, ts, rc):
    b, t, d = x.shape
    assert t % ts == 0 and ts % rc == 0 and ts >= CONV_HALO
    tile = pl.BlockSpec((1, ts, d), lambda i, j: (i, j, 0))
    first = pl.BlockSpec((1, CONV_HALO, d), lambda i, j: (i, 0, 0))
    vec = _resident((1, d))
    return pl.pallas_call(
        functools.partial(_conv_prompt_body, ts=ts, rc=rc),
        grid=(b, t // ts),
        in_specs=[tile, first, vec, _resident(w_in.shape), _resident(b_in.shape), _resident(w_dw.shape), vec, vec,
                  vec, _resident(w_out.shape), vec, vec],
        out_specs=[tile, first],
        out_shape=[jax.ShapeDtypeStruct((b, t, d), F32), jax.ShapeDtypeStruct((b, CONV_HALO, d), F32)],
        scratch_shapes=[pltpu.VMEM((CONV_HALO + ts, d), F32), pltpu.VMEM((ts, d), F32)],
        compiler_params=_params("parallel", "arbitrary"),
        name="conv_prompt",
    )(x, hist32, g_pre, w_in, b_in, w_dw, b_dw, ln_g, ln_b, w_out, b_out, g_post)


def _conv_sample_body(x_ref, hist_ref, g_pre_ref, w_in_ref, b_in_ref, w_dw_ref, b_dw_ref, ln_g_ref, ln_b_ref,
                      w_out_ref, b_out_ref, g_post_ref, o_ref, u_ref):
    x = x_ref[...]
    h = _rms(x, g_pre_ref[...]).astype(BF16)
    u = _glu(h, w_in_ref, b_in_ref)
    u_ref[...] = u
    c = b_dw_ref[...] + u * w_dw_ref[CONV_HIST:CONV_WIDTH, :]
    for k in range(CONV_HIST):
        c = c + hist_ref[k] * w_dw_ref[k:k + 1, :]
    o_ref[...] = _conv_tail(c, x, ln_g_ref, ln_b_ref, w_out_ref, b_out_ref, g_post_ref)


def _conv_sample(x, hist_t, g_pre, w_in, b_in, w_dw, b_dw, ln_g, ln_b, w_out, b_out, g_post):
    bd, d = x.shape
    return pl.pallas_call(
        _conv_sample_body,
        out_shape=[jax.ShapeDtypeStruct((bd, d), F32), jax.ShapeDtypeStruct((bd, d), F32)],
        compiler_params=pltpu.CompilerParams(vmem_limit_bytes=VMEM_LIMIT_BYTES),
        name="conv_sample",
    )(x, hist_t, g_pre, w_in, b_in, w_dw, b_dw, ln_g, ln_b, w_out, b_out, g_post)


def _qkv_prompt_body(x_ref, g_pre_ref, w_ref, qkv_ref, kv_ref, *scratch, tm, dil, rows, first):
    t = pl.program_id(1)
    d = x_ref.shape[-1]
    h = _rms(x_ref[0], g_pre_ref[...]).astype(BF16)
    for part in range(3):
        r = _dot(h, w_ref[:, part * d:(part + 1) * d])
        if part == 0:
            r = r * (HEAD_DIM ** -0.5)
        if part > 0:
            @pl.when(t >= first)
            def _(r=r, part=part):
                kv_ref[0, :, (part - 1) * d:part * d] = r[tm - rows:, :]
        if dil == 1:
            qkv_ref[0, 0, :, part * d:(part + 1) * d] = r.astype(BF16)
        else:
            rbuf, rbuf2 = scratch
            n_slab = d // LANES
            for s in range(n_slab):
                rbuf[s] = r[:, s * LANES:(s + 1) * LANES]
            outer = max(dil // REGROUP_STRIDE, 1)
            inner = dil // outer
            for a in range(inner):
                if outer == 1:
                    pieces = [[rbuf[s, pl.ds(a, tm // dil, stride=dil), :] for s in range(n_slab)]]
                else:
                    for s in range(n_slab):
                        rbuf2[s] = rbuf[s, pl.ds(a, tm // inner, stride=inner), :]
                    pieces = [[rbuf2[s, pl.ds(c, tm // dil, stride=outer), :] for s in range(n_slab)]
                              for c in range(outer)]
                for c, piece in enumerate(pieces):
                    qkv_ref[0, a + inner * c, :, part * d:(part + 1) * d] = jnp.concatenate(piece, axis=1).astype(BF16)


def _qkv_prompt(x, g_pre, w_qkv, group, tm):
    b, t, d = x.shape
    win, dil = SWA_CONFIGS[group]
    keep = min(win, t)
    rows = min(keep, tm)
    assert t % tm == 0 and keep % rows == 0 and tm % (dil * 16) == 0
    first = t // tm - (keep // rows if keep >= tm else 1)
    tile = pl.BlockSpec((1, tm, d), lambda i, j: (i, j, 0))
    w_spec = pl.BlockSpec((d, 3 * d), lambda i, j: (0, group), pipeline_mode=pl.Buffered(1))
    return pl.pallas_call(
        functools.partial(_qkv_prompt_body, tm=tm, dil=dil, rows=rows, first=first),
        grid=(b, t // tm),
        in_specs=[tile, _resident((1, d)), w_spec],
        out_specs=[pl.BlockSpec((1, dil, tm // dil, 3 * d), lambda i, j: (i, 0, j, 0)),
                   pl.BlockSpec((1, rows, 2 * d), lambda i, j: (i, jnp.maximum(j - first, 0), 0))],
        out_shape=[jax.ShapeDtypeStruct((b, dil, t // dil, 3 * d), BF16), jax.ShapeDtypeStruct((b, keep, 2 * d), F32)],
        scratch_shapes=[pltpu.VMEM((d // LANES, tm, LANES), F32),
                        pltpu.VMEM((d // LANES, tm // min(dil, REGROUP_STRIDE), LANES), F32)] if dil > 1 else [],
        compiler_params=_params("parallel", "arbitrary"),
        name=f"qkv_prompt_d{dil}",
    )(x, g_pre, w_qkv)


def _qkv_sample_body(x_ref, g_pre_ref, w_ref, o_ref):
    h = _rms(x_ref[...], g_pre_ref[...]).astype(BF16)
    o_ref[...] = _dot(h, w_ref[...])


def _qkv_sample(x, g_pre, w_qkv):
    bd, _ = x.shape
    return pl.pallas_call(
        _qkv_sample_body,
        out_shape=jax.ShapeDtypeStruct((bd, w_qkv.shape[1]), F32),
        compiler_params=pltpu.CompilerParams(vmem_limit_bytes=VMEM_LIMIT_BYTES),
        name="qkv_sample",
    )(x, g_pre, w_qkv)


def _attn_prompt_body(*refs, dil, chained, last):
    refs = list(refs)
    q_ref, kc_ref, vc_ref, kp_ref, vp_ref, bias_ref = refs[:6]
    del refs[:6]
    if chained:
        op_ref, mp_ref, sp_ref = refs[:3]
        del refs[:3]
    o_ref = refs[0]
    if not last:
        m_ref, s_ref = refs[1:3]
    n = pl.program_id(1)
    res = pl.program_id(2)
    rows = pl.ds(res, BLK, stride=dil) if dil > 1 else slice(None)

    kj = lax.broadcasted_iota(jnp.int32, (2 * BLK, 2 * BLK), 0)
    qi = lax.broadcasted_iota(jnp.int32, (2 * BLK, 2 * BLK), 1) % BLK
    delta = qi + BLK - kj
    first_key = jnp.where(n > 0, 0, BLK)
    valid = (delta >= 0) & (delta <= SPAN) & (kj >= first_key)
    lane = lax.broadcasted_iota(jnp.int32, (BLK, HEAD_PAIR), 1)
    low = lane < HEAD_DIM

    q = q_ref[0, 0]
    k = jnp.concatenate([kp_ref[0, 0], kc_ref[0, 0]], axis=0)
    v = jnp.concatenate([vp_ref[0, 0], vc_ref[0, 0]], axis=0)
    n_pairs = q.shape[-1] // HEAD_PAIR
    head_row = lax.broadcasted_iota(jnp.int32, (2 * n_pairs, BLK), 0)
    if chained:
        m_prev_t = mp_ref[0, rows, :].T
        s_prev_t = sp_ref[0, rows, :].T
    m_out_t = jnp.zeros((2 * n_pairs, BLK), F32)
    s_out_t = jnp.zeros((2 * n_pairs, BLK), F32)

    def pair_rows(stat_t):
        return jnp.stack([jnp.concatenate([stat_t[2 * p:2 * p + 1], stat_t[2 * p + 1:2 * p + 2]], axis=1)
                          for p in range(n_pairs)], axis=0)

    sts = []
    for pair in range(n_pairs):
        sl = slice(pair * HEAD_PAIR, (pair + 1) * HEAD_PAIR)
        q2 = q[:, sl]
        qq = jnp.concatenate([jnp.where(low, q2, jnp.zeros_like(q2)), jnp.where(low, jnp.zeros_like(q2), q2)], axis=0)
        sts.append(lax.dot_general(k[:, sl], qq, (((1,), (1,)), ((), ())), preferred_element_type=F32))
    st = jnp.where(valid[None], jnp.stack(sts, axis=0) + bias_ref[...], NEG_INF)
    m_new = jnp.max(st, axis=1, keepdims=True)
    if chained:
        m_prev = pair_rows(m_prev_t)
        m_new = jnp.maximum(m_new, m_prev)
    p = jnp.exp(st - m_new)
    s_new = jnp.sum(p, axis=1, keepdims=True)
    if chained:
        alpha = jnp.exp(m_prev - m_new)
        s_new = pair_rows(s_prev_t) * alpha + s_new
    p = p.astype(BF16)
    for pair in range(n_pairs):
        sl = slice(pair * HEAD_PAIR, (pair + 1) * HEAD_PAIR)
        pv = lax.dot_general(v[:, sl], p[pair], (((0,), (0,)), ((), ())), preferred_element_type=F32)
        o_t = [pv[:HEAD_DIM, :BLK], pv[HEAD_DIM:, BLK:]]
        if chained:
            o_prev_t = op_ref[0, pair, rows, :].T
        for half in range(2):
            cols = slice(half * BLK, (half + 1) * BLK)
            if chained:
                o_t[half] = o_prev_t[half * HEAD_DIM:(half + 1) * HEAD_DIM] * alpha[pair][:, cols] + o_t[half]
            if last:
                o_t[half] = o_t[half] / s_new[pair][:, cols]
            else:
                m_out_t = jnp.where(head_row == 2 * pair + half, m_new[pair][:, cols], m_out_t)
                s_out_t = jnp.where(head_row == 2 * pair + half, s_new[pair][:, cols], s_out_t)
        o_ref[0, pair, rows, :] = jnp.concatenate(o_t, axis=0).T
    if not last:
        pad = jnp.zeros((BLK - 2 * n_pairs, BLK), F32)
        m_ref[0, rows, :] = jnp.concatenate([m_out_t, pad], axis=0).T
        s_ref[0, rows, :] = jnp.concatenate([s_out_t, pad], axis=0).T


def _attn_prompt(qkv, bias, dil, prev, last):
    b, _, sub, d3 = qkv.shape
    d = d3 // 3
    t = sub * dil
    assert sub % BLK == 0 and d % HEAD_PAIR == 0 and d // HEAD_DIM <= LANES
    n_pairs = d // HEAD_PAIR
    span = BLK * dil

    def col(part):
        return pl.BlockSpec((1, 1, BLK, d), lambda i, n, r: (i, r, n, part))

    def col_prev(part):
        return pl.BlockSpec((1, 1, BLK, d), lambda i, n, r: (i, r, jnp.maximum(n - 1, 0), part))

    o_spec = pl.BlockSpec((1, n_pairs, span, HEAD_PAIR), lambda i, n, r: (i, 0, n, 0))
    st_spec = pl.BlockSpec((1, span, LANES), lambda i, n, r: (i, n, 0))
    in_specs = [col(0), col(1), col(2), col_prev(1), col_prev(2), _resident(bias.shape)]
    args = [qkv, qkv, qkv, qkv, qkv, bias]
    if prev is not None:
        in_specs += [o_spec, st_spec, st_spec]
        args += list(prev)
    o_shape = jax.ShapeDtypeStruct((b, n_pairs, t, HEAD_PAIR), F32)
    st_shape = jax.ShapeDtypeStruct((b, t, LANES), F32)
    return pl.pallas_call(
        functools.partial(_attn_prompt_body, dil=dil, chained=prev is not None, last=last),
        grid=(b, sub // BLK, dil),
        in_specs=in_specs,
        out_specs=[o_spec] if last else [o_spec, st_spec, st_spec],
        out_shape=[o_shape] if last else [o_shape, st_shape, st_shape],
        compiler_params=_params("parallel", "arbitrary", "arbitrary"),
        name=f"attn_prompt_d{dil}",
    )(*args)


def _bias_table(rel_bias, g, dil):
    onehot = np.eye(N_BUCKETS, dtype=np.float32)[_t5_bucket(np.arange(SPAN + 1) * dil)]
    return jnp.dot(rel_bias[:, g, :].astype(F32).T, jnp.asarray(onehot).T, precision=lax.Precision.HIGHEST)


def _prompt_bias(table):
    nh = table.shape[0]
    width = 3 * BLK - 1
    rev = jnp.concatenate([jnp.broadcast_to(table[:, SPAN:], (nh, BLK - 1)), table[:, ::-1],
                           jnp.broadcast_to(table[:, :1], (nh, BLK - 1)), jnp.zeros((nh, 1), F32)], axis=1)
    skew = jnp.tile(rev, (1, BLK))[:, :BLK * width].reshape(nh, BLK, width)
    bias = skew[:, :, BLK - 1:]
    bias_t = jnp.transpose(bias, (0, 2, 1)).reshape(nh // 2, 2, 2 * BLK, BLK)
    return jnp.transpose(bias_t, (0, 2, 1, 3)).reshape(nh // 2, 2 * BLK, 2 * BLK)


def _attn_sample_body(qt_ref, qh_ref, kn_ref, vn_ref, kc_ref, vc_ref, bias_c_ref, bias_n_ref, o_ref, lse_ref, *, dil):
    nh, n_cache = bias_c_ref.shape
    pos = lax.broadcasted_iota(jnp.int32, (1, n_cache), 1)
    on_grid = pos % dil == 0
    head = lax.broadcasted_iota(jnp.int32, (1, nh), 1)
    scale = HEAD_DIM ** -0.5

    def one_seq(i, carry):
        q_t = qt_ref[i] * scale
        ln = jnp.sum(kn_ref[i] * (qh_ref[i] * scale), axis=1, keepdims=True) + bias_n_ref[...]
        lc = jnp.concatenate(
            [jnp.sum(kc_ref[i, h] * q_t[:, h:h + 1], axis=0, keepdims=True) for h in range(nh)], axis=0)
        lc = jnp.where(on_grid, lc + bias_c_ref[...], NEG_INF)
        m = jnp.maximum(jnp.max(lc, axis=1, keepdims=True), ln)
        pc = jnp.exp(lc - m)
        pn = jnp.exp(ln - m)
        s = jnp.sum(pc, axis=1, keepdims=True) + pn
        vn_t = vn_ref[i]
        o_all = jnp.zeros_like(q_t)
        for h in range(nh):
            o = jnp.sum(vc_ref[i, h] * pc[h:h + 1, :], axis=1, keepdims=True)
            o = (o + pn[h:h + 1, :] * vn_t[:, h:h + 1]) / s[h:h + 1, :]
            o_all = jnp.where(head == h, o, o_all)
        o_ref[i] = o_all
        lse_ref[i] = m + jnp.log(s)
        return carry

    lax.fori_loop(0, qt_ref.shape[0], one_seq, 0)


def _attn_sample(q, kn, vn, cache, layer, bias_c, bias_n, dil, bb):
    bd, nh, dh = q.shape
    n_cache = cache.shape[2]
    assert n_cache == SPAN * dil and bd % bb == 0
    cache_t = jnp.transpose(cache, (0, 1, 3, 4, 5, 2))
    col = pl.BlockSpec((bb, dh, nh), lambda i: (i, 0, 0))
    row = pl.BlockSpec((bb, nh, dh), lambda i: (i, 0, 0))

    def kv(which):
        return pl.BlockSpec((None, bb, None, nh, dh, n_cache), lambda i: (layer, i, which, 0, 0, 0))

    return pl.pallas_call(
        functools.partial(_attn_sample_body, dil=dil),
        grid=(bd // bb,),
        in_specs=[col, row, row, col, kv(0), kv(1), _resident(bias_c.shape), _resident(bias_n.shape)],
        out_specs=[col, pl.BlockSpec((bb, nh, 1), lambda i: (i, 0, 0))],
        out_shape=[jax.ShapeDtypeStruct((bd, dh, nh), F32), jax.ShapeDtypeStruct((bd, nh, 1), F32)],
        compiler_params=_params("parallel"),
        name=f"attn_sample_d{dil}",
    )(jnp.swapaxes(q, 1, 2), q, kn, jnp.swapaxes(vn, 1, 2), cache_t, cache_t, bias_c, bias_n)


def _merge_sample_body(o0_ref, o1_ref, o2_ref, l0_ref, l1_ref, l2_ref, out_ref):
    lses = (l0_ref[...], l1_ref[...], l2_ref[...])
    outs = (o0_ref[...], o1_ref[...], o2_ref[...])
    mx = jnp.maximum(jnp.maximum(lses[0], lses[1]), lses[2])
    es = [jnp.exp(l - mx) for l in lses]
    tot = es[0] + es[1] + es[2]
    out_ref[...] = sum((e / tot) * o for e, o in zip(es, outs))


def _merge_sample(outs, lses):
    return pl.pallas_call(
        _merge_sample_body,
        out_shape=jax.ShapeDtypeStruct(outs[0].shape, F32),
        name="merge_sample",
    )(*outs, *lses)


def _wo_prompt_body(x_ref, o_ref_in, w_ref, g_post_ref, f0, f1, f2, f3, out_ref):
    o = jnp.concatenate([o_ref_in[0, p] for p in range(o_ref_in.shape[1])], axis=1)
    y = _dot(o.astype(BF16), w_ref[...])
    out_ref[0] = _ffn_tile(x_ref[0] + _rms(y, g_post_ref[...]), f0, f1, f2, f3)


def _wo_prompt(x, o, w_o, g_post, ffn, tm):
    b, t, d = x.shape
    row = pl.BlockSpec((1, tm, d), lambda i, j: (i, j, 0))
    return pl.pallas_call(
        _wo_prompt_body,
        grid=(b, t // tm),
        in_specs=[row, pl.BlockSpec((1, o.shape[1], tm, HEAD_PAIR), lambda i, j: (i, 0, j, 0)),
                  _resident(w_o.shape), _resident((1, d))] + _ffn_specs(ffn),
        out_specs=row,
        out_shape=jax.ShapeDtypeStruct((b, t, d), F32),
        compiler_params=_params("parallel", "parallel"),
        name="wo_ffn_prompt",
    )(x, o, w_o, g_post, *ffn)


def _wo_sample_body(x_ref, o_ref_in, w_ref, g_post_ref, out_ref):
    y = _dot(o_ref_in[...].astype(BF16), w_ref[...])
    out_ref[...] = x_ref[...] + _rms(y, g_post_ref[...])


def _wo_sample(x, o, w_o, g_post):
    return pl.pallas_call(
        _wo_sample_body,
        out_shape=jax.ShapeDtypeStruct(x.shape, F32),
        compiler_params=pltpu.CompilerParams(vmem_limit_bytes=VMEM_LIMIT_BYTES),
        name="wo_sample",
    )(x, o, w_o, g_post)


def kernel(x_prompt, x_sample, state_pool, cache_swa_g0, cache_swa_g1, cache_swa_g2, state_conv,
           norm_g, w_ffn_in, w_ffn_out, pool_w, pool_scale, w_qkv, w_o, rel_bias,
           conv_w_in, conv_b_in, conv_w_dw, conv_b_dw, conv_ln_g, conv_ln_b, conv_w_out, conv_b_out):
    b, t, d = x_prompt.shape
    bd, t_dec, _ = x_sample.shape
    assert t_dec == 1 and d % HEAD_PAIR == 0
    nh = d // HEAD_DIM
    depth = norm_g.shape[0]
    caches = (cache_swa_g0, cache_swa_g1, cache_swa_g2)

    tm = 512 if (b * t) % 512 == 0 else b * t
    ts = min(512, t)
    tq = min(512, t)

    xp = x_prompt
    xs = x_sample.reshape(bd, d)
    vec = lambda a: a.reshape(1, -1).astype(F32)

    pool_p, pool_s, conv_p, conv_s = [], [], [], []
    swa_p = [[] for _ in range(N_GROUPS)]
    swa_s = [[] for _ in range(N_GROUPS)]
    for i in range(depth):
        kind, j = i % 3, i // 3
        g = [vec(norm_g[i, k]) for k in range(4)]
        ffn = (g[2], w_ffn_in[i].astype(BF16), w_ffn_out[i].astype(BF16), g[3])
        if kind == 0:
            pw = pool_w[j].astype(BF16)
            ps = vec(pool_scale[j])
            xp, tail = _pool_prompt(xp, jnp.zeros((b, POOL_HALO, d), F32), g[0], pw, ps, g[1], ffn, ts, 0)
            pool_p.append(tail[:, POOL_HALO - POOL_HIST:])
            hist = state_pool[j]
            xs, hs = _pool_sample(xs, jnp.swapaxes(hist, 0, 1), g[0], pw, ps, g[1], PAST_LEN)
            pool_s.append(jnp.concatenate([hist[:, 1:], hs[:, None, :]], axis=1))
        elif kind == 1:
            wq = w_qkv[j].astype(BF16)
            wo = w_o[j].astype(BF16)
            outs = [_qkv_prompt(xp, g[0], wq, gi, tq) for gi in range(N_GROUPS)]
            tables = [_bias_table(rel_bias, gi, dil) for gi, (_, dil) in enumerate(SWA_CONFIGS)]
            order = sorted(range(N_GROUPS), key=lambda gi: -SWA_CONFIGS[gi][1])
            prev = None
            for k, gi in enumerate(order):
                prev = _attn_prompt(outs[gi][0], _prompt_bias(tables[gi]), SWA_CONFIGS[gi][1], prev, k == N_GROUPS - 1)
            for gi in range(N_GROUPS):
                swa_p[gi].append(outs[gi][1].reshape(b, -1, 2, nh, HEAD_DIM))
            xp = _wo_prompt(xp, prev[0], wo, g[1], ffn, ts)

            qkv_s = _qkv_sample(xs, g[0], wq).reshape(bd, N_GROUPS, 3, nh, HEAD_DIM)
            o_s, lse_s = [], []
            for gi, (_, dil) in enumerate(SWA_CONFIGS):
                bias_c = jnp.repeat(tables[gi][:, :0:-1], dil, axis=1)
                bias_n = tables[gi][:, :1]
                per_seq = d * SPAN * dil * 4
                bb = max(1, min(bd, SAMPLE_KV_BLOCK_BYTES // per_seq))
                o_g, l_g = _attn_sample(qkv_s[:, gi, 0], qkv_s[:, gi, 1], qkv_s[:, gi, 2], caches[gi], j,
                                        bias_c, bias_n, dil, bb)
                o_s.append(o_g)
                lse_s.append(jnp.swapaxes(l_g, 1, 2))
                swa_s[gi].append(qkv_s[:, gi, 1:3].reshape(bd, 1, 2, nh, HEAD_DIM))
            merged = jnp.swapaxes(_merge_sample(o_s, lse_s), 1, 2).reshape(bd, d)
            xs = _wo_sample(xs, merged, wo, g[1])
        else:
            cw = (conv_w_in[j].astype(BF16), vec(conv_b_in[j]),
                  jnp.pad(conv_w_dw[j], ((0, CONV_HALO - CONV_WIDTH), (0, 0))), vec(conv_b_dw[j]),
                  vec(conv_ln_g[j]), vec(conv_ln_b[j]), conv_w_out[j].astype(BF16), vec(conv_b_out[j]))
            xp, tail = _conv_prompt(xp, jnp.zeros((b, CONV_HALO, d), F32), g[0], *cw, g[1], ts, 128)
            conv_p.append(tail[:, CONV_HALO - CONV_HIST:])
            hist = state_conv[j]
            xs, us = _conv_sample(xs, jnp.swapaxes(hist, 0, 1), g[0], *cw, g[1])
            conv_s.append(jnp.concatenate([hist[:, 1:], us[:, None, :]], axis=1))
            xp = _ffn(xp.reshape(b * t, d), ffn, tm).reshape(b, t, d)
        xs = _ffn(xs, ffn, bd)

    return (xp, xs.reshape(bd, 1, d),
            jnp.stack(pool_p), jnp.stack(pool_s),
            jnp.stack(swa_p[0]), jnp.stack(swa_p[1]), jnp.stack(swa_p[2]),
            jnp.stack(swa_s[0]), jnp.stack(swa_s[1]), jnp.stack(swa_s[2]),
            jnp.stack(conv_p), jnp.stack(conv_s))
```

```python
import functools
import math

import numpy as np
import jax
import jax.numpy as jnp
from jax import lax
from jax.experimental import pallas as pl
from jax.experimental.pallas import tpu as pltpu

F32 = jnp.float32
BF16 = jnp.bfloat16

RMS_EPS = 1e-6
LN_EPS = 1e-5
NEG_INF = -1e30
PAST_LEN = 8192
POOL_WINDOWS = (2, 4, 8, 16)
POOL_HIST = max(POOL_WINDOWS) - 1
SWA_CONFIGS = ((128, 1), (512, 4), (2048, 16))
N_GROUPS = len(SWA_CONFIGS)
SPAN = 128
BLK = 128
HEAD_DIM = 64
N_BUCKETS = 32
MAX_DISTANCE = 2048
CONV_WIDTH = 31
CONV_HIST = CONV_WIDTH - 1

LANES = 128
SUBLANES = 8
MXU_DIM = 256
VMEM_LIMIT_BYTES = 56 * 1024 * 1024
REGROUP_STRIDE = 4

SAMPLE_KV_BLOCK_BYTES = 8 * 1024 * 1024

POOL_HALO = 16
CONV_HALO = 32
HEAD_PAIR = 2 * HEAD_DIM
LOG2_E = math.log2(math.e)
ATTN_UNITS = 2


def _params(*sem):
    return pltpu.CompilerParams(dimension_semantics=sem, vmem_limit_bytes=VMEM_LIMIT_BYTES)


def _resident(shape):
    nd = len(shape)
    return pl.BlockSpec(shape, lambda *_: (0,) * nd, pipeline_mode=pl.Buffered(1))


def _rms(x, g):
    return x * lax.rsqrt(jnp.mean(x * x, axis=-1, keepdims=True) + RMS_EPS) * g


def _dot(a, b):
    return jnp.dot(a, b, preferred_element_type=F32)


def _t5_bucket(dist):
    max_exact = N_BUCKETS // 2
    d = np.maximum(np.asarray(dist), 0)
    large = max_exact + (np.log(np.maximum(d, max_exact) / max_exact) / math.log(MAX_DISTANCE / max_exact)
                         * (N_BUCKETS - max_exact)).astype(np.int64)
    large = np.minimum(large, N_BUCKETS - 1)
    return np.where(d < max_exact, d, large).astype(np.int32)


def _ff_chunks(d_ff):
    assert d_ff % MXU_DIM == 0
    n = d_ff // MXU_DIM
    first = (n + 1) // 2 * MXU_DIM
    return ((0, first), (first, d_ff - first)) if d_ff > first else ((0, d_ff),)


def _ffn_tile(x, g_pre_ref, w_in_ref, w_out_ref, g_post_ref):
    d_ff = w_out_ref.shape[0]
    h = _rms(x, g_pre_ref[...]).astype(BF16)
    acc = None
    for c0, cw in _ff_chunks(d_ff):
        gate = _dot(h, w_in_ref[:, c0:c0 + cw])
        up = _dot(h, w_in_ref[:, d_ff + c0:d_ff + c0 + cw])
        act = (gate * jax.nn.sigmoid(gate) * up).astype(BF16)
        part = _dot(act, w_out_ref[c0:c0 + cw, :])
        acc = part if acc is None else acc + part
    return x + _rms(acc, g_post_ref[...])


def _ffn_specs(ffn):
    return [_resident(a.shape) for a in ffn]


def _ffn_body(x_ref, g_pre_ref, w_in_ref, w_out_ref, g_post_ref, o_ref):
    o_ref[...] = _ffn_tile(x_ref[...], g_pre_ref, w_in_ref, w_out_ref, g_post_ref)


def _ffn(x, ffn, tm):
    n, d = x.shape
    assert n % tm == 0
    row = pl.BlockSpec((tm, d), lambda i: (i, 0))
    return pl.pallas_call(
        _ffn_body,
        grid=(n // tm,),
        in_specs=[row] + _ffn_specs(ffn),
        out_specs=row,
        out_shape=jax.ShapeDtypeStruct((n, d), F32),
        compiler_params=_params("parallel"),
        name="ffn",
    )(x, *ffn)


def _pool_groups(h, window_sum, cnt_of, pw_ref, ps_ref):
    gd = h.shape[-1] // len(POOL_WINDOWS)
    outs = []
    for gi, w in enumerate(POOL_WINDOWS):
        c0 = gi * gd
        hg = h[:, c0:c0 + gd]
        dg = window_sum(gi, w, c0, gd, hg) / cnt_of(w) - hg
        outs.append(_dot(dg.astype(BF16), pw_ref[gi]))
    return jnp.concatenate(outs, axis=1) * ps_ref[...]


def _pool_prompt_body(x_ref, xprev_ref, hist_ref, g_pre_ref, pw_ref, ps_ref, g_post_ref, f0, f1, f2, f3,
                      o_ref, tail_ref, buf, *, ts, n_past):
    t = pl.program_id(1)
    x = x_ref[0]
    g_pre = g_pre_ref[...]
    h = _rms(x, g_pre)
    halo = jnp.where(t == 0, hist_ref[0], _rms(xprev_ref[0], g_pre))
    buf[0:POOL_HALO, :] = halo
    buf[POOL_HALO:POOL_HALO + ts, :] = h
    tail_ref[0] = buf[ts:ts + POOL_HALO, :]
    pos1 = n_past + 1 + t * ts + lax.broadcasted_iota(jnp.int32, (ts, 1), 0)

    def window_sum(gi, w, c0, gd, hg):
        assert w & (w - 1) == 0 and w <= POOL_HALO
        acc = buf[:, c0:c0 + gd]
        step = 1
        while step < w:
            acc = acc + pltpu.roll(acc, step, axis=0)
            step *= 2
        return acc[POOL_HALO:, :]

    y = _pool_groups(h, window_sum, lambda w: jnp.minimum(pos1, w).astype(F32), pw_ref, ps_ref)
    o_ref[0] = _ffn_tile(x + _rms(y, g_post_ref[...]), f0, f1, f2, f3)


def _pool_prompt(x, hist16, g_pre, pw, ps, g_post, ffn, ts, n_past):
    b, t, d = x.shape
    assert t % ts == 0 and ts % POOL_HALO == 0
    per = ts // POOL_HALO
    tile = pl.BlockSpec((1, ts, d), lambda i, j: (i, j, 0))
    halo = pl.BlockSpec((1, POOL_HALO, d), lambda i, j: (i, jnp.maximum(j * per - 1, 0), 0))
    first = pl.BlockSpec((1, POOL_HALO, d), lambda i, j: (i, 0, 0))
    return pl.pallas_call(
        functools.partial(_pool_prompt_body, ts=ts, n_past=n_past),
        grid=(b, t // ts),
        in_specs=[tile, halo, first, _resident((1, d)), _resident(pw.shape), _resident((1, d)), _resident((1, d))]
        + _ffn_specs(ffn),
        out_specs=[tile, first],
        out_shape=[jax.ShapeDtypeStruct((b, t, d), F32), jax.ShapeDtypeStruct((b, POOL_HALO, d), F32)],
        scratch_shapes=[pltpu.VMEM((POOL_HALO + ts, d), F32)],
        compiler_params=_params("parallel", "arbitrary"),
        name="pool_ffn_prompt",
    )(x, x, hist16, g_pre, pw, ps, g_post, *ffn)


def _pool_sample_body(x_ref, hist_ref, g_pre_ref, pw_ref, ps_ref, g_post_ref, o_ref, h_ref, *, n_past):
    x = x_ref[...]
    h = _rms(x, g_pre_ref[...])
    h_ref[...] = h

    def window_sum(gi, w, c0, gd, hg):
        win = hg
        for k in range(1, w):
            win = win + hist_ref[POOL_HIST - k, :, c0:c0 + gd]
        return win

    y = _pool_groups(h, window_sum, lambda w: float(min(n_past + 1, w)), pw_ref, ps_ref)
    o_ref[...] = x + _rms(y, g_post_ref[...])


def _pool_sample(x, hist_t, g_pre, pw, ps, g_post, n_past):
    bd, d = x.shape
    return pl.pallas_call(
        functools.partial(_pool_sample_body, n_past=n_past),
        out_shape=[jax.ShapeDtypeStruct((bd, d), F32), jax.ShapeDtypeStruct((bd, d), F32)],
        compiler_params=pltpu.CompilerParams(vmem_limit_bytes=VMEM_LIMIT_BYTES),
        name="pool_sample",
    )(x, hist_t, g_pre, pw, ps, g_post)


def _glu(h, w_in_ref, b_in_ref):
    d = h.shape[-1]
    a = _dot(h, w_in_ref[...]) + b_in_ref[...]
    return a[:, :d] * jax.nn.sigmoid(a[:, d:])


def _conv_tail(c, x, ln_g_ref, ln_b_ref, w_out_ref, b_out_ref, g_post_ref):
    mu = jnp.mean(c, axis=-1, keepdims=True)
    cc = c - mu
    var = jnp.mean(cc * cc, axis=-1, keepdims=True)
    z = cc * lax.rsqrt(var + LN_EPS) * ln_g_ref[...] + ln_b_ref[...]
    y = _dot((z * jax.nn.sigmoid(z)).astype(BF16), w_out_ref[...]) + b_out_ref[...]
    return x + _rms(y, g_post_ref[...])


def _conv_prompt_body(x_ref, hist_ref, g_pre_ref, w_in_ref, b_in_ref, w_dw_ref, b_dw_ref, ln_g_ref, ln_b_ref,
                      w_out_ref, b_out_ref, g_post_ref, o_ref, tail_ref, ubuf, cbuf, *, ts, rc):
    t = pl.program_id(1)
    d = x_ref.shape[-1]

    @pl.when(t == 0)
    def _():
        ubuf[0:CONV_HALO, :] = hist_ref[0]

    @pl.when(t > 0)
    def _():
        ubuf[0:CONV_HALO, :] = ubuf[ts:ts + CONV_HALO, :]

    x = x_ref[0]
    h = _rms(x, g_pre_ref[...]).astype(BF16)
    ubuf[CONV_HALO:CONV_HALO + ts, :] = _glu(h, w_in_ref, b_in_ref)
    tail_ref[0] = ubuf[ts:ts + CONV_HALO, :]

    lead = CONV_HALO - CONV_HIST

    def chunk(i, carry):
        base = pl.multiple_of(i * rc, rc)
        for l0 in range(0, d, LANES):
            win = ubuf[pl.ds(base, rc + CONV_HALO), l0:l0 + LANES]
            acc = jnp.broadcast_to(b_dw_ref[:, l0:l0 + LANES], (rc, LANES))
            for phase in range(SUBLANES):
                offs = [o for o in range(lead, lead + CONV_WIDTH) if o % SUBLANES == phase]
                shifted = win if phase == 0 else pltpu.roll(win, rc + CONV_HALO - phase, axis=0)
                for o in offs:
                    acc = acc + shifted[o - phase:o - phase + rc, :] * w_dw_ref[o - lead:o - lead + 1, l0:l0 + LANES]
            cbuf[pl.ds(base, rc), l0:l0 + LANES] = acc
        return carry

    lax.fori_loop(0, ts // rc, chunk, 0)
    o_ref[0] = _conv_tail(cbuf[...], x, ln_g_ref, ln_b_ref, w_out_ref, b_out_ref, g_post_ref)


def _conv_prompt(x, hist32, g_pre, w_in, b_in, w_dw, b_dw, ln_g, ln_b, w_out, b_out, g_post, ts, rc):
    b, t, d = x.shape
    assert t % ts == 0 and ts % rc == 0 and ts >= CONV_HALO
    tile = pl.BlockSpec((1, ts, d), lambda i, j: (i, j, 0))
    first = pl.BlockSpec((1, CONV_HALO, d), lambda i, j: (i, 0, 0))
    vec = _resident((1, d))
    return pl.pallas_call(
        functools.partial(_conv_prompt_body, ts=ts, rc=rc),
        grid=(b, t // ts),
        in_specs=[tile, first, vec, _resident(w_in.shape), _resident(b_in.shape), _resident(w_dw.shape), vec, vec,
                  vec, _resident(w_out.shape), vec, vec],
        out_specs=[tile, first],
        out_shape=[jax.ShapeDtypeStruct((b, t, d), F32), jax.ShapeDtypeStruct((b, CONV_HALO, d), F32)],
        scratch_shapes=[pltpu.VMEM((CONV_HALO + ts, d), F32), pltpu.VMEM((ts, d), F32)],
        compiler_params=_params("parallel", "arbitrary"),
        name="conv_prompt",
    )(x, hist32, g_pre, w_in, b_in, w_dw, b_dw, ln_g, ln_b, w_out, b_out, g_post)


def _conv_sample_body(x_ref, hist_ref, g_pre_ref, w_in_ref, b_in_ref, w_dw_ref, b_dw_ref, ln_g_ref, ln_b_ref,
                      w_out_ref, b_out_ref, g_post_ref, o_ref, u_ref):
    x = x_ref[...]
    h = _rms(x, g_pre_ref[...]).astype(BF16)
    u = _glu(h, w_in_ref, b_in_ref)
    u_ref[...] = u
    c = b_dw_ref[...] + u * w_dw_ref[CONV_HIST:CONV_WIDTH, :]
    for k in range(CONV_HIST):
        c = c + hist_ref[k] * w_dw_ref[k:k + 1, :]
    o_ref[...] = _conv_tail(c, x, ln_g_ref, ln_b_ref, w_out_ref, b_out_ref, g_post_ref)


def _conv_sample(x, hist_t, g_pre, w_in, b_in, w_dw, b_dw, ln_g, ln_b, w_out, b_out, g_post):
    bd, d = x.shape
    return pl.pallas_call(
        _conv_sample_body,
        out_shape=[jax.ShapeDtypeStruct((bd, d), F32), jax.ShapeDtypeStruct((bd, d), F32)],
        compiler_params=pltpu.CompilerParams(vmem_limit_bytes=VMEM_LIMIT_BYTES),
        name="conv_sample",
    )(x, hist_t, g_pre, w_in, b_in, w_dw, b_dw, ln_g, ln_b, w_out, b_out, g_post)


def _qkv_prompt_body(x_ref, g_pre_ref, w_ref, qkv_ref, kv_ref, *scratch, tm, dil, rows, first):
    t = pl.program_id(1)
    d = x_ref.shape[-1]
    h = _rms(x_ref[0], g_pre_ref[...]).astype(BF16)
    for part in range(3):
        r = _dot(h, w_ref[:, part * d:(part + 1) * d])
        if part == 0:
            r = r * (HEAD_DIM ** -0.5 * LOG2_E)
        if part > 0:
            @pl.when(t >= first)
            def _(r=r, part=part):
                kv_ref[0, :, (part - 1) * d:part * d] = r[tm - rows:, :]
        if dil == 1:
            qkv_ref[0, 0, :, part * d:(part + 1) * d] = r.astype(BF16)
        else:
            rbuf, rbuf2 = scratch
            n_slab = d // LANES
            for s in range(n_slab):
                rbuf[s] = r[:, s * LANES:(s + 1) * LANES]
            outer = max(dil // REGROUP_STRIDE, 1)
            inner = dil // outer
            for a in range(inner):
                if outer == 1:
                    pieces = [[rbuf[s, pl.ds(a, tm // dil, stride=dil), :] for s in range(n_slab)]]
                else:
                    for s in range(n_slab):
                        rbuf2[s] = rbuf[s, pl.ds(a, tm // inner, stride=inner), :]
                    pieces = [[rbuf2[s, pl.ds(c, tm // dil, stride=outer), :] for s in range(n_slab)]
                              for c in range(outer)]
                for c, piece in enumerate(pieces):
                    qkv_ref[0, a + inner * c, :, part * d:(part + 1) * d] = jnp.concatenate(piece, axis=1).astype(BF16)


def _qkv_prompt(x, g_pre, w_qkv, group, tm):
    b, t, d = x.shape
    win, dil = SWA_CONFIGS[group]
    keep = min(win, t)
    rows = min(keep, tm)
    assert t % tm == 0 and keep % rows == 0 and tm % (dil * 16) == 0
    first = t // tm - (keep // rows if keep >= tm else 1)
    tile = pl.BlockSpec((1, tm, d), lambda i, j: (i, j, 0))
    w_spec = pl.BlockSpec((d, 3 * d), lambda i, j: (0, group), pipeline_mode=pl.Buffered(1))
    return pl.pallas_call(
        functools.partial(_qkv_prompt_body, tm=tm, dil=dil, rows=rows, first=first),
        grid=(b, t // tm),
        in_specs=[tile, _resident((1, d)), w_spec],
        out_specs=[pl.BlockSpec((1, dil, tm // dil, 3 * d), lambda i, j: (i, 0, j, 0)),
                   pl.BlockSpec((1, rows, 2 * d), lambda i, j: (i, jnp.maximum(j - first, 0), 0))],
        out_shape=[jax.ShapeDtypeStruct((b, dil, t // dil, 3 * d), BF16), jax.ShapeDtypeStruct((b, keep, 2 * d), F32)],
        scratch_shapes=[pltpu.VMEM((d // LANES, tm, LANES), F32),
                        pltpu.VMEM((d // LANES, tm // min(dil, REGROUP_STRIDE), LANES), F32)] if dil > 1 else [],
        compiler_params=_params("parallel", "arbitrary"),
        name=f"qkv_prompt_d{dil}",
    )(x, g_pre, w_qkv)


def _qkv_sample_body(x_ref, g_pre_ref, w_ref, o_ref):
    h = _rms(x_ref[...], g_pre_ref[...]).astype(BF16)
    o_ref[...] = _dot(h, w_ref[...])


def _qkv_sample(x, g_pre, w_qkv):
    bd, _ = x.shape
    return pl.pallas_call(
        _qkv_sample_body,
        out_shape=jax.ShapeDtypeStruct((bd, w_qkv.shape[1]), F32),
        compiler_params=pltpu.CompilerParams(vmem_limit_bytes=VMEM_LIMIT_BYTES),
        name="qkv_sample",
    )(x, g_pre, w_qkv)


def _attn_prompt_body(*refs, dil, units, chained, last):
    refs = list(refs)
    q_ref, kc_ref, vc_ref, kp_ref, vp_ref, bias_ref = refs[:6]
    del refs[:6]
    if chained:
        op_ref, mp_ref, sp_ref = refs[:3]
        del refs[:3]
    o_ref = refs[0]
    if not last:
        m_ref, s_ref = refs[1:3]
    n = pl.program_id(1)
    res = pl.program_id(2)

    kj = lax.broadcasted_iota(jnp.int32, (2 * BLK, 2 * BLK), 0)
    qi = lax.broadcasted_iota(jnp.int32, (2 * BLK, 2 * BLK), 1) % BLK
    delta = qi + BLK - kj
    band = (delta >= 0) & (delta <= SPAN)
    first_key = jnp.where(n > 0, 0, BLK)
    band_first = band & (kj >= first_key)
    lane = lax.broadcasted_iota(jnp.int32, (BLK, HEAD_PAIR), 1)
    low = lane < HEAD_DIM
    n_pairs = q_ref.shape[-1] // HEAD_PAIR
    head_row = lax.broadcasted_iota(jnp.int32, (2 * n_pairs, BLK), 0)

    def unit_operands(u):
        if dil > 1:
            cat = lambda prev_ref, cur_ref: jnp.concatenate([prev_ref[0, u], cur_ref[0, u]], axis=0)
            return (q_ref[0, u], cat(kp_ref, kc_ref), cat(vp_ref, vc_ref), band_first,
                    pl.ds(res * units + u, BLK, stride=dil))
        own = slice(u * BLK, (u + 1) * BLK)
        if u == 0:
            cat = lambda prev_ref, cur_ref: jnp.concatenate([prev_ref[0, 0], cur_ref[0, 0, own]], axis=0)
            return q_ref[0, 0, own], cat(kp_ref, kc_ref), cat(vp_ref, vc_ref), band_first, own
        both = slice((u - 1) * BLK, (u + 1) * BLK)
        return q_ref[0, 0, own], kc_ref[0, 0, both], vc_ref[0, 0, both], band, own

    def pair_rows(stat_t):
        return jnp.stack([jnp.concatenate([stat_t[2 * p:2 * p + 1], stat_t[2 * p + 1:2 * p + 2]], axis=1)
                          for p in range(n_pairs)], axis=0)

    ops = [unit_operands(u) for u in range(units)]
    sts, m_prevs, s_prevs = [], [], []
    for q, k, _, valid, rows in ops:
        scores = []
        for pair in range(n_pairs):
            sl = slice(pair * HEAD_PAIR, (pair + 1) * HEAD_PAIR)
            q2 = q[:, sl]
            qq = jnp.concatenate([jnp.where(low, q2, jnp.zeros_like(q2)), jnp.where(low, jnp.zeros_like(q2), q2)],
                                 axis=0)
            scores.append(lax.dot_general(k[:, sl], qq, (((1,), (1,)), ((), ())), preferred_element_type=F32))
        sts.append(jnp.where(valid[None], jnp.stack(scores, axis=0) + bias_ref[...], NEG_INF))
        if chained:
            m_prevs.append(pair_rows(mp_ref[0, rows, :].T))
            s_prevs.append(pair_rows(sp_ref[0, rows, :].T))
    st = jnp.concatenate(sts, axis=0)
    m_new = jnp.max(st, axis=1, keepdims=True)
    if chained:
        m_prev = jnp.concatenate(m_prevs, axis=0)
        m_new = jnp.maximum(m_new, m_prev)
    p = jnp.exp2(st - m_new)
    s_new = jnp.sum(p, axis=1, keepdims=True)
    if chained:
        alpha = jnp.exp2(m_prev - m_new)
        s_new = jnp.concatenate(s_prevs, axis=0) * alpha + s_new
    p = p.astype(BF16)
    for u, (_, _, v, _, rows) in enumerate(ops):
        m_out_t = jnp.zeros((2 * n_pairs, BLK), F32)
        s_out_t = jnp.zeros((2 * n_pairs, BLK), F32)
        for pair in range(n_pairs):
            i = u * n_pairs + pair
            sl = slice(pair * HEAD_PAIR, (pair + 1) * HEAD_PAIR)
            pv = lax.dot_general(v[:, sl], p[i], (((0,), (0,)), ((), ())), preferred_element_type=F32)
            o_t = [pv[:HEAD_DIM, :BLK], pv[HEAD_DIM:, BLK:]]
            if chained:
                o_prev_t = op_ref[0, pair, rows, :].T
            for half in range(2):
                cols = slice(half * BLK, (half + 1) * BLK)
                if chained:
                    o_t[half] = o_prev_t[half * HEAD_DIM:(half + 1) * HEAD_DIM] * alpha[i][:, cols] + o_t[half]
                if last:
                    o_t[half] = o_t[half] / s_new[i][:, cols]
                else:
                    m_out_t = jnp.where(head_row == 2 * pair + half, m_new[i][:, cols], m_out_t)
                    s_out_t = jnp.where(head_row == 2 * pair + half, s_new[i][:, cols], s_out_t)
            o_ref[0, pair, rows, :] = jnp.concatenate(o_t, axis=0).T
        if not last:
            pad = jnp.zeros((BLK - 2 * n_pairs, BLK), F32)
            m_ref[0, rows, :] = jnp.concatenate([m_out_t, pad], axis=0).T
            s_ref[0, rows, :] = jnp.concatenate([s_out_t, pad], axis=0).T


def _attn_prompt(qkv, bias, dil, prev, last, units):
    b, _, sub, d3 = qkv.shape
    d = d3 // 3
    t = sub * dil
    n_pairs = d // HEAD_PAIR
    assert sub % BLK == 0 and d % HEAD_PAIR == 0 and d // HEAD_DIM <= LANES
    if dil > 1:
        assert dil % units == 0
        grid = (b, sub // BLK, dil // units)
        span = BLK * dil
        blk = (1, units, BLK, d)
        col = lambda part: pl.BlockSpec(blk, lambda i, n, r: (i, r, n, part))
        col_prev = lambda part: pl.BlockSpec(blk, lambda i, n, r: (i, r, jnp.maximum(n - 1, 0), part))
    else:
        assert (sub // BLK) % units == 0
        grid = (b, sub // (BLK * units), 1)
        span = BLK * units
        col = lambda part: pl.BlockSpec((1, 1, span, d), lambda i, n, r: (i, 0, n, part))
        col_prev = lambda part: pl.BlockSpec((1, 1, BLK, d), lambda i, n, r: (i, 0, jnp.maximum(n * units - 1, 0), part))

    o_spec = pl.BlockSpec((1, n_pairs, span, HEAD_PAIR), lambda i, n, r: (i, 0, n, 0))
    st_spec = pl.BlockSpec((1, span, LANES), lambda i, n, r: (i, n, 0))
    in_specs = [col(0), col(1), col(2), col_prev(1), col_prev(2), _resident(bias.shape)]
    args = [qkv, qkv, qkv, qkv, qkv, bias]
    if prev is not None:
        in_specs += [o_spec, st_spec, st_spec]
        args += list(prev)
    o_shape = jax.ShapeDtypeStruct((b, n_pairs, t, HEAD_PAIR), F32)
    st_shape = jax.ShapeDtypeStruct((b, t, LANES), F32)
    return pl.pallas_call(
        functools.partial(_attn_prompt_body, dil=dil, units=units, chained=prev is not None, last=last),
        grid=grid,
        in_specs=in_specs,
        out_specs=[o_spec] if last else [o_spec, st_spec, st_spec],
        out_shape=[o_shape] if last else [o_shape, st_shape, st_shape],
        compiler_params=_params("parallel", "arbitrary", "arbitrary"),
        name=f"attn_prompt_d{dil}",
    )(*args)


def _bias_table(rel_bias, g, dil):
    onehot = np.eye(N_BUCKETS, dtype=np.float32)[_t5_bucket(np.arange(SPAN + 1) * dil)]
    return jnp.dot(rel_bias[:, g, :].astype(F32).T, jnp.asarray(onehot).T, precision=lax.Precision.HIGHEST)


def _prompt_bias(table):
    nh = table.shape[0]
    width = 3 * BLK - 1
    rev = jnp.concatenate([jnp.broadcast_to(table[:, SPAN:], (nh, BLK - 1)), table[:, ::-1],
                           jnp.broadcast_to(table[:, :1], (nh, BLK - 1)), jnp.zeros((nh, 1), F32)], axis=1)
    skew = jnp.tile(rev, (1, BLK))[:, :BLK * width].reshape(nh, BLK, width)
    bias = skew[:, :, BLK - 1:]
    bias_t = jnp.transpose(bias, (0, 2, 1)).reshape(nh // 2, 2, 2 * BLK, BLK)
    return jnp.transpose(bias_t, (0, 2, 1, 3)).reshape(nh // 2, 2 * BLK, 2 * BLK)


def _attn_sample_body(qt_ref, qh_ref, kn_ref, vn_ref, kc_ref, vc_ref, bias_c_ref, bias_n_ref, o_ref, lse_ref, *, dil):
    nh, n_cache = bias_c_ref.shape
    pos = lax.broadcasted_iota(jnp.int32, (1, n_cache), 1)
    on_grid = pos % dil == 0
    head = lax.broadcasted_iota(jnp.int32, (1, nh), 1)
    scale = HEAD_DIM ** -0.5

    def one_seq(i, carry):
        q_t = qt_ref[i] * scale
        ln = jnp.sum(kn_ref[i] * (qh_ref[i] * scale), axis=1, keepdims=True) + bias_n_ref[...]
        lc = jnp.concatenate(
            [jnp.sum(kc_ref[i, h] * q_t[:, h:h + 1], axis=0, keepdims=True) for h in range(nh)], axis=0)
        lc = jnp.where(on_grid, lc + bias_c_ref[...], NEG_INF)
        m = jnp.maximum(jnp.max(lc, axis=1, keepdims=True), ln)
        pc = jnp.exp(lc - m)
        pn = jnp.exp(ln - m)
        s = jnp.sum(pc, axis=1, keepdims=True) + pn
        vn_t = vn_ref[i]
        o_all = jnp.zeros_like(q_t)
        for h in range(nh):
            o = jnp.sum(vc_ref[i, h] * pc[h:h + 1, :], axis=1, keepdims=True)
            o = (o + pn[h:h + 1, :] * vn_t[:, h:h + 1]) / s[h:h + 1, :]
            o_all = jnp.where(head == h, o, o_all)
        o_ref[i] = o_all
        lse_ref[i] = m + jnp.log(s)
        return carry

    lax.fori_loop(0, qt_ref.shape[0], one_seq, 0)


def _attn_sample(q, kn, vn, cache, layer, bias_c, bias_n, dil, bb):
    bd, nh, dh = q.shape
    n_cache = cache.shape[2]
    assert n_cache == SPAN * dil and bd % bb == 0
    cache_t = jnp.transpose(cache, (0, 1, 3, 4, 5, 2))
    col = pl.BlockSpec((bb, dh, nh), lambda i: (i, 0, 0))
    row = pl.BlockSpec((bb, nh, dh), lambda i: (i, 0, 0))

    def kv(which):
        return pl.BlockSpec((None, bb, None, nh, dh, n_cache), lambda i: (layer, i, which, 0, 0, 0))

    return pl.pallas_call(
        functools.partial(_attn_sample_body, dil=dil),
        grid=(bd // bb,),
        in_specs=[col, row, row, col, kv(0), kv(1), _resident(bias_c.shape), _resident(bias_n.shape)],
        out_specs=[col, pl.BlockSpec((bb, nh, 1), lambda i: (i, 0, 0))],
        out_shape=[jax.ShapeDtypeStruct((bd, dh, nh), F32), jax.ShapeDtypeStruct((bd, nh, 1), F32)],
        compiler_params=_params("parallel"),
        name=f"attn_sample_d{dil}",
    )(jnp.swapaxes(q, 1, 2), q, kn, jnp.swapaxes(vn, 1, 2), cache_t, cache_t, bias_c, bias_n)


def _merge_sample_body(o0_ref, o1_ref, o2_ref, l0_ref, l1_ref, l2_ref, out_ref):
    lses = (l0_ref[...], l1_ref[...], l2_ref[...])
    outs = (o0_ref[...], o1_ref[...], o2_ref[...])
    mx = jnp.maximum(jnp.maximum(lses[0], lses[1]), lses[2])
    es = [jnp.exp(l - mx) for l in lses]
    tot = es[0] + es[1] + es[2]
    out_ref[...] = sum((e / tot) * o for e, o in zip(es, outs))


def _merge_sample(outs, lses):
    return pl.pallas_call(
        _merge_sample_body,
        out_shape=jax.ShapeDtypeStruct(outs[0].shape, F32),
        name="merge_sample",
    )(*outs, *lses)


def _wo_prompt_body(x_ref, o_ref_in, w_ref, g_post_ref, f0, f1, f2, f3, out_ref):
    o = jnp.concatenate([o_ref_in[0, p] for p in range(o_ref_in.shape[1])], axis=1)
    y = _dot(o.astype(BF16), w_ref[...])
    out_ref[0] = _ffn_tile(x_ref[0] + _rms(y, g_post_ref[...]), f0, f1, f2, f3)


def _wo_prompt(x, o, w_o, g_post, ffn, tm):
    b, t, d = x.shape
    row = pl.BlockSpec((1, tm, d), lambda i, j: (i, j, 0))
    return pl.pallas_call(
        _wo_prompt_body,
        grid=(b, t // tm),
        in_specs=[row, pl.BlockSpec((1, o.shape[1], tm, HEAD_PAIR), lambda i, j: (i, 0, j, 0)),
                  _resident(w_o.shape), _resident((1, d))] + _ffn_specs(ffn),
        out_specs=row,
        out_shape=jax.ShapeDtypeStruct((b, t, d), F32),
        compiler_params=_params("parallel", "parallel"),
        name="wo_ffn_prompt",
    )(x, o, w_o, g_post, *ffn)


def _wo_sample_body(x_ref, o_ref_in, w_ref, g_post_ref, out_ref):
    y = _dot(o_ref_in[...].astype(BF16), w_ref[...])
    out_ref[...] = x_ref[...] + _rms(y, g_post_ref[...])


def _wo_sample(x, o, w_o, g_post):
    return pl.pallas_call(
        _wo_sample_body,
        out_shape=jax.ShapeDtypeStruct(x.shape, F32),
        compiler_params=pltpu.CompilerParams(vmem_limit_bytes=VMEM_LIMIT_BYTES),
        name="wo_sample",
    )(x, o, w_o, g_post)


def kernel(x_prompt, x_sample, state_pool, cache_swa_g0, cache_swa_g1, cache_swa_g2, state_conv,
           norm_g, w_ffn_in, w_ffn_out, pool_w, pool_scale, w_qkv, w_o, rel_bias,
           conv_w_in, conv_b_in, conv_w_dw, conv_b_dw, conv_ln_g, conv_ln_b, conv_w_out, conv_b_out):
    b, t, d = x_prompt.shape
    bd, t_dec, _ = x_sample.shape
    assert t_dec == 1 and d % HEAD_PAIR == 0
    nh = d // HEAD_DIM
    depth = norm_g.shape[0]
    caches = (cache_swa_g0, cache_swa_g1, cache_swa_g2)

    tm = 512 if (b * t) % 512 == 0 else b * t
    ts = min(512, t)
    tq = min(512, t)

    xp = x_prompt
    xs = x_sample.reshape(bd, d)
    vec = lambda a: a.reshape(1, -1).astype(F32)

    pool_p, pool_s, conv_p, conv_s = [], [], [], []
    swa_p = [[] for _ in range(N_GROUPS)]
    swa_s = [[] for _ in range(N_GROUPS)]
    for i in range(depth):
        kind, j = i % 3, i // 3
        g = [vec(norm_g[i, k]) for k in range(4)]
        ffn = (g[2], w_ffn_in[i].astype(BF16), w_ffn_out[i].astype(BF16), g[3])
        if kind == 0:
            pw = pool_w[j].astype(BF16)
            ps = vec(pool_scale[j])
            xp, tail = _pool_prompt(xp, jnp.zeros((b, POOL_HALO, d), F32), g[0], pw, ps, g[1], ffn, ts, 0)
            pool_p.append(tail[:, POOL_HALO - POOL_HIST:])
            hist = state_pool[j]
            xs, hs = _pool_sample(xs, jnp.swapaxes(hist, 0, 1), g[0], pw, ps, g[1], PAST_LEN)
            pool_s.append(jnp.concatenate([hist[:, 1:], hs[:, None, :]], axis=1))
        elif kind == 1:
            wq = w_qkv[j].astype(BF16)
            wo = w_o[j].astype(BF16)
            outs = [_qkv_prompt(xp, g[0], wq, gi, tq) for gi in range(N_GROUPS)]
            tables = [_bias_table(rel_bias, gi, dil) for gi, (_, dil) in enumerate(SWA_CONFIGS)]
            order = sorted(range(N_GROUPS), key=lambda gi: -SWA_CONFIGS[gi][1])
            prev = None
            for k, gi in enumerate(order):
                prev = _attn_prompt(outs[gi][0], _prompt_bias(tables[gi] * LOG2_E), SWA_CONFIGS[gi][1], prev,
                                    k == N_GROUPS - 1, ATTN_UNITS)
            for gi in range(N_GROUPS):
                swa_p[gi].append(outs[gi][1].reshape(b, -1, 2, nh, HEAD_DIM))
            xp = _wo_prompt(xp, prev[0], wo, g[1], ffn, ts)

            qkv_s = _qkv_sample(xs, g[0], wq).reshape(bd, N_GROUPS, 3, nh, HEAD_DIM)
            o_s, lse_s = [], []
            for gi, (_, dil) in enumerate(SWA_CONFIGS):
                bias_c = jnp.repeat(tables[gi][:, :0:-1], dil, axis=1)
                bias_n = tables[gi][:, :1]
                per_seq = d * SPAN * dil * 4
                bb = max(1, min(bd, SAMPLE_KV_BLOCK_BYTES // per_seq))
                o_g, l_g = _attn_sample(qkv_s[:, gi, 0], qkv_s[:, gi, 1], qkv_s[:, gi, 2], caches[gi], j,
                                        bias_c, bias_n, dil, bb)
                o_s.append(o_g)
                lse_s.append(jnp.swapaxes(l_g, 1, 2))
                swa_s[gi].append(qkv_s[:, gi, 1:3].reshape(bd, 1, 2, nh, HEAD_DIM))
            merged = jnp.swapaxes(_merge_sample(o_s, lse_s), 1, 2).reshape(bd, d)
            xs = _wo_sample(xs, merged, wo, g[1])
        else:
            cw = (conv_w_in[j].astype(BF16), vec(conv_b_in[j]),
                  jnp.pad(conv_w_dw[j], ((0, CONV_HALO - CONV_WIDTH), (0, 0))), vec(conv_b_dw[j]),
                  vec(conv_ln_g[j]), vec(conv_ln_b[j]), conv_w_out[j].astype(BF16), vec(conv_b_out[j]))
            xp, tail = _conv_prompt(xp, jnp.zeros((b, CONV_HALO, d), F32), g[0], *cw, g[1], ts, 128)
            conv_p.append(tail[:, CONV_HALO - CONV_HIST:])
            hist = state_conv[j]
            xs, us = _conv_sample(xs, jnp.swapaxes(hist, 0, 1), g[0], *cw, g[1])
            conv_s.append(jnp.concatenate([hist[:, 1:], us[:, None, :]], axis=1))
            xp = _ffn(xp.reshape(b * t, d), ffn, tm).reshape(b, t, d)
        xs = _ffn(xs, ffn, bd)

    return (xp, xs.reshape(bd, 1, d),
            jnp.stack(pool_p), jnp.stack(pool_s),
            jnp.stack(swa_p[0]), jnp.stack(swa_p[1]), jnp.stack(swa_p[2]),
            jnp.stack(swa_s[0]), jnp.stack(swa_s[1]), jnp.stack(swa_s[2]),
            jnp.stack(conv_p), jnp.stack(conv_s))
```

```python
import functools
import math

import numpy as np
import jax
import jax.numpy as jnp
from jax import lax
from jax.experimental import pallas as pl
from jax.experimental.pallas import tpu as pltpu

F32 = jnp.float32
BF16 = jnp.bfloat16

RMS_EPS = 1e-6
LN_EPS = 1e-5
NEG_INF = -1e30
PAST_LEN = 8192
POOL_WINDOWS = (2, 4, 8, 16)
POOL_HIST = max(POOL_WINDOWS) - 1
SWA_CONFIGS = ((128, 1), (512, 4), (2048, 16))
N_GROUPS = len(SWA_CONFIGS)
SPAN = 128
BLK = 128
HEAD_DIM = 64
N_BUCKETS = 32
MAX_DISTANCE = 2048
CONV_WIDTH = 31
CONV_HIST = CONV_WIDTH - 1

LANES = 128
SUBLANES = 8
MXU_DIM = 256
VMEM_LIMIT_BYTES = 56 * 1024 * 1024
REGROUP_STRIDE = 4


POOL_HALO = 16
CONV_HALO = 32
HEAD_PAIR = 2 * HEAD_DIM
LOG2_E = math.log2(math.e)
ATTN_UNITS = 2


def _params(*sem):
    return pltpu.CompilerParams(dimension_semantics=sem, vmem_limit_bytes=VMEM_LIMIT_BYTES)


def _resident(shape):
    nd = len(shape)
    return pl.BlockSpec(shape, lambda *_: (0,) * nd, pipeline_mode=pl.Buffered(1))


def _rms(x, g):
    return x * lax.rsqrt(jnp.mean(x * x, axis=-1, keepdims=True) + RMS_EPS) * g


def _dot(a, b):
    return jnp.dot(a, b, preferred_element_type=F32)


def _t5_bucket(dist):
    max_exact = N_BUCKETS // 2
    d = np.maximum(np.asarray(dist), 0)
    large = max_exact + (np.log(np.maximum(d, max_exact) / max_exact) / math.log(MAX_DISTANCE / max_exact)
                         * (N_BUCKETS - max_exact)).astype(np.int64)
    large = np.minimum(large, N_BUCKETS - 1)
    return np.where(d < max_exact, d, large).astype(np.int32)


def _ff_chunks(d_ff):
    assert d_ff % MXU_DIM == 0
    n = d_ff // MXU_DIM
    first = (n + 1) // 2 * MXU_DIM
    return ((0, first), (first, d_ff - first)) if d_ff > first else ((0, d_ff),)


def _ffn_tile(x, g_pre_ref, w_in_ref, w_out_ref, g_post_ref):
    d_ff = w_out_ref.shape[0]
    h = _rms(x, g_pre_ref[...]).astype(BF16)
    acc = None
    for c0, cw in _ff_chunks(d_ff):
        gate = _dot(h, w_in_ref[:, c0:c0 + cw])
        up = _dot(h, w_in_ref[:, d_ff + c0:d_ff + c0 + cw])
        act = (gate * jax.nn.sigmoid(gate) * up).astype(BF16)
        part = _dot(act, w_out_ref[c0:c0 + cw, :])
        acc = part if acc is None else acc + part
    return x + _rms(acc, g_post_ref[...])


def _ffn_specs(ffn):
    return [_resident(a.shape) for a in ffn]


def _ffn_body(x_ref, g_pre_ref, w_in_ref, w_out_ref, g_post_ref, o_ref):
    o_ref[...] = _ffn_tile(x_ref[...], g_pre_ref, w_in_ref, w_out_ref, g_post_ref)


def _ffn(x, ffn, tm):
    n, d = x.shape
    assert n % tm == 0
    row = pl.BlockSpec((tm, d), lambda i: (i, 0))
    return pl.pallas_call(
        _ffn_body,
        grid=(n // tm,),
        in_specs=[row] + _ffn_specs(ffn),
        out_specs=row,
        out_shape=jax.ShapeDtypeStruct((n, d), F32),
        compiler_params=_params("parallel"),
        name="ffn",
    )(x, *ffn)


def _pool_groups(h, window_sum, cnt_of, pw_ref, ps_ref):
    gd = h.shape[-1] // len(POOL_WINDOWS)
    outs = []
    for gi, w in enumerate(POOL_WINDOWS):
        c0 = gi * gd
        hg = h[:, c0:c0 + gd]
        dg = window_sum(gi, w, c0, gd, hg) / cnt_of(w) - hg
        outs.append(_dot(dg.astype(BF16), pw_ref[gi]))
    return jnp.concatenate(outs, axis=1) * ps_ref[...]


def _pool_prompt_body(x_ref, xprev_ref, hist_ref, g_pre_ref, pw_ref, ps_ref, g_post_ref, f0, f1, f2, f3,
                      o_ref, tail_ref, buf, *, ts, n_past):
    t = pl.program_id(1)
    x = x_ref[0]
    g_pre = g_pre_ref[...]
    h = _rms(x, g_pre)
    halo = jnp.where(t == 0, hist_ref[0], _rms(xprev_ref[0], g_pre))
    buf[0:POOL_HALO, :] = halo
    buf[POOL_HALO:POOL_HALO + ts, :] = h
    tail_ref[0] = buf[ts:ts + POOL_HALO, :]
    pos1 = n_past + 1 + t * ts + lax.broadcasted_iota(jnp.int32, (ts, 1), 0)

    def window_sum(gi, w, c0, gd, hg):
        assert w & (w - 1) == 0 and w <= POOL_HALO
        acc = buf[:, c0:c0 + gd]
        step = 1
        while step < w:
            acc = acc + pltpu.roll(acc, step, axis=0)
            step *= 2
        return acc[POOL_HALO:, :]

    y = _pool_groups(h, window_sum, lambda w: jnp.minimum(pos1, w).astype(F32), pw_ref, ps_ref)
    o_ref[0] = _ffn_tile(x + _rms(y, g_post_ref[...]), f0, f1, f2, f3)


def _pool_prompt(x, hist16, g_pre, pw, ps, g_post, ffn, ts, n_past):
    b, t, d = x.shape
    assert t % ts == 0 and ts % POOL_HALO == 0
    per = ts // POOL_HALO
    tile = pl.BlockSpec((1, ts, d), lambda i, j: (i, j, 0))
    halo = pl.BlockSpec((1, POOL_HALO, d), lambda i, j: (i, jnp.maximum(j * per - 1, 0), 0))
    first = pl.BlockSpec((1, POOL_HALO, d), lambda i, j: (i, 0, 0))
    return pl.pallas_call(
        functools.partial(_pool_prompt_body, ts=ts, n_past=n_past),
        grid=(b, t // ts),
        in_specs=[tile, halo, first, _resident((1, d)), _resident(pw.shape), _resident((1, d)), _resident((1, d))]
        + _ffn_specs(ffn),
        out_specs=[tile, first],
        out_shape=[jax.ShapeDtypeStruct((b, t, d), F32), jax.ShapeDtypeStruct((b, POOL_HALO, d), F32)],
        scratch_shapes=[pltpu.VMEM((POOL_HALO + ts, d), F32)],
        compiler_params=_params("parallel", "arbitrary"),
        name="pool_ffn_prompt",
    )(x, x, hist16, g_pre, pw, ps, g_post, *ffn)


def _pool_sample_body(x_ref, hist_ref, g_pre_ref, pw_ref, ps_ref, g_post_ref, o_ref, h_ref, *, n_past):
    x = x_ref[...]
    h = _rms(x, g_pre_ref[...])
    h_ref[...] = h

    def window_sum(gi, w, c0, gd, hg):
        win = hg
        for k in range(1, w):
            win = win + hist_ref[POOL_HIST - k, :, c0:c0 + gd]
        return win

    y = _pool_groups(h, window_sum, lambda w: float(min(n_past + 1, w)), pw_ref, ps_ref)
    o_ref[...] = x + _rms(y, g_post_ref[...])


def _pool_sample(x, hist_t, g_pre, pw, ps, g_post, n_past):
    bd, d = x.shape
    return pl.pallas_call(
        functools.partial(_pool_sample_body, n_past=n_past),
        out_shape=[jax.ShapeDtypeStruct((bd, d), F32), jax.ShapeDtypeStruct((bd, d), F32)],
        compiler_params=pltpu.CompilerParams(vmem_limit_bytes=VMEM_LIMIT_BYTES),
        name="pool_sample",
    )(x, hist_t, g_pre, pw, ps, g_post)


def _glu(h, w_in_ref, b_in_ref):
    d = h.shape[-1]
    a = _dot(h, w_in_ref[...]) + b_in_ref[...]
    return a[:, :d] * jax.nn.sigmoid(a[:, d:])


def _conv_tail(c, x, ln_g_ref, ln_b_ref, w_out_ref, b_out_ref, g_post_ref):
    mu = jnp.mean(c, axis=-1, keepdims=True)
    cc = c - mu
    var = jnp.mean(cc * cc, axis=-1, keepdims=True)
    z = cc * lax.rsqrt(var + LN_EPS) * ln_g_ref[...] + ln_b_ref[...]
    y = _dot((z * jax.nn.sigmoid(z)).astype(BF16), w_out_ref[...]) + b_out_ref[...]
    return x + _rms(y, g_post_ref[...])


def _conv_prompt_body(x_ref, hist_ref, g_pre_ref, w_in_ref, b_in_ref, w_dw_ref, b_dw_ref, ln_g_ref, ln_b_ref,
                      w_out_ref, b_out_ref, g_post_ref, o_ref, tail_ref, ubuf, cbuf, *, ts, rc):
    t = pl.program_id(1)
    d = x_ref.shape[-1]

    @pl.when(t == 0)
    def _():
        ubuf[0:CONV_HALO, :] = hist_ref[0]

    @pl.when(t > 0)
    def _():
        ubuf[0:CONV_HALO, :] = ubuf[ts:ts + CONV_HALO, :]

    x = x_ref[0]
    h = _rms(x, g_pre_ref[...]).astype(BF16)
    ubuf[CONV_HALO:CONV_HALO + ts, :] = _glu(h, w_in_ref, b_in_ref)
    tail_ref[0] = ubuf[ts:ts + CONV_HALO, :]

    lead = CONV_HALO - CONV_HIST

    def chunk(i, carry):
        base = pl.multiple_of(i * rc, rc)
        for l0 in range(0, d, LANES):
            win = ubuf[pl.ds(base, rc + CONV_HALO), l0:l0 + LANES]
            acc = jnp.broadcast_to(b_dw_ref[:, l0:l0 + LANES], (rc, LANES))
            for phase in range(SUBLANES):
                offs = [o for o in range(lead, lead + CONV_WIDTH) if o % SUBLANES == phase]
                shifted = win if phase == 0 else pltpu.roll(win, rc + CONV_HALO - phase, axis=0)
                for o in offs:
                    acc = acc + shifted[o - phase:o - phase + rc, :] * w_dw_ref[o - lead:o - lead + 1, l0:l0 + LANES]
            cbuf[pl.ds(base, rc), l0:l0 + LANES] = acc
        return carry

    lax.fori_loop(0, ts // rc, chunk, 0)
    o_ref[0] = _conv_tail(cbuf[...], x, ln_g_ref, ln_b_ref, w_out_ref, b_out_ref, g_post_ref)


def _conv_prompt(x, hist32, g_pre, w_in, b_in, w_dw, b_dw, ln_g, ln_b, w_out, b_out, g_post, ts, rc):
    b, t, d = x.shape
    assert t % ts == 0 and ts % rc == 0 and ts >= CONV_HALO
    tile = pl.BlockSpec((1, ts, d), lambda i, j: (i, j, 0))
    first = pl.BlockSpec((1, CONV_HALO, d), lambda i, j: (i, 0, 0))
    vec = _resident((1, d))
    return pl.pallas_call(
        functools.partial(_conv_prompt_body, ts=ts, rc=rc),
        grid=(b, t // ts),
        in_specs=[tile, first, vec, _resident(w_in.shape), _resident(b_in.shape), _resident(w_dw.shape), vec, vec,
                  vec, _resident(w_out.shape), vec, vec],
        out_specs=[tile, first],
        out_shape=[jax.ShapeDtypeStruct((b, t, d), F32), jax.ShapeDtypeStruct((b, CONV_HALO, d), F32)],
        scratch_shapes=[pltpu.VMEM((CONV_HALO + ts, d), F32), pltpu.VMEM((ts, d), F32)],
        compiler_params=_params("parallel", "arbitrary"),
        name="conv_prompt",
    )(x, hist32, g_pre, w_in, b_in, w_dw, b_dw, ln_g, ln_b, w_out, b_out, g_post)


def _conv_sample_body(x_ref, hist_ref, g_pre_ref, w_in_ref, b_in_ref, w_dw_ref, b_dw_ref, ln_g_ref, ln_b_ref,
                      w_out_ref, b_out_ref, g_post_ref, o_ref, u_ref):
    x = x_ref[...]
    h = _rms(x, g_pre_ref[...]).astype(BF16)
    u = _glu(h, w_in_ref, b_in_ref)
    u_ref[...] = u
    c = b_dw_ref[...] + u * w_dw_ref[CONV_HIST:CONV_WIDTH, :]
    for k in range(CONV_HIST):
        c = c + hist_ref[k] * w_dw_ref[k:k + 1, :]
    o_ref[...] = _conv_tail(c, x, ln_g_ref, ln_b_ref, w_out_ref, b_out_ref, g_post_ref)


def _conv_sample(x, hist_t, g_pre, w_in, b_in, w_dw, b_dw, ln_g, ln_b, w_out, b_out, g_post):
    bd, d = x.shape
    return pl.pallas_call(
        _conv_sample_body,
        out_shape=[jax.ShapeDtypeStruct((bd, d), F32), jax.ShapeDtypeStruct((bd, d), F32)],
        compiler_params=pltpu.CompilerParams(vmem_limit_bytes=VMEM_LIMIT_BYTES),
        name="conv_sample",
    )(x, hist_t, g_pre, w_in, b_in, w_dw, b_dw, ln_g, ln_b, w_out, b_out, g_post)


def _qkv_prompt_body(x_ref, g_pre_ref, w_ref, qkv_ref, kv_ref, *scratch, tm, dil, rows, first):
    t = pl.program_id(1)
    d = x_ref.shape[-1]
    h = _rms(x_ref[0], g_pre_ref[...]).astype(BF16)
    for part in range(3):
        r = _dot(h, w_ref[:, part * d:(part + 1) * d])
        if part == 0:
            r = r * (HEAD_DIM ** -0.5 * LOG2_E)
        if part > 0:
            @pl.when(t >= first)
            def _(r=r, part=part):
                kv_ref[0, :, (part - 1) * d:part * d] = r[tm - rows:, :]
        if dil == 1:
            qkv_ref[0, 0, :, part * d:(part + 1) * d] = r.astype(BF16)
        else:
            rbuf, rbuf2 = scratch
            n_slab = d // LANES
            for s in range(n_slab):
                rbuf[s] = r[:, s * LANES:(s + 1) * LANES]
            outer = max(dil // REGROUP_STRIDE, 1)
            inner = dil // outer
            for a in range(inner):
                if outer == 1:
                    pieces = [[rbuf[s, pl.ds(a, tm // dil, stride=dil), :] for s in range(n_slab)]]
                else:
                    for s in range(n_slab):
                        rbuf2[s] = rbuf[s, pl.ds(a, tm // inner, stride=inner), :]
                    pieces = [[rbuf2[s, pl.ds(c, tm // dil, stride=outer), :] for s in range(n_slab)]
                              for c in range(outer)]
                for c, piece in enumerate(pieces):
                    qkv_ref[0, a + inner * c, :, part * d:(part + 1) * d] = jnp.concatenate(piece, axis=1).astype(BF16)


def _qkv_prompt(x, g_pre, w_qkv, group, tm):
    b, t, d = x.shape
    win, dil = SWA_CONFIGS[group]
    keep = min(win, t)
    rows = min(keep, tm)
    assert t % tm == 0 and keep % rows == 0 and tm % (dil * 16) == 0
    first = t // tm - (keep // rows if keep >= tm else 1)
    tile = pl.BlockSpec((1, tm, d), lambda i, j: (i, j, 0))
    w_spec = pl.BlockSpec((d, 3 * d), lambda i, j: (0, group), pipeline_mode=pl.Buffered(1))
    return pl.pallas_call(
        functools.partial(_qkv_prompt_body, tm=tm, dil=dil, rows=rows, first=first),
        grid=(b, t // tm),
        in_specs=[tile, _resident((1, d)), w_spec],
        out_specs=[pl.BlockSpec((1, dil, tm // dil, 3 * d), lambda i, j: (i, 0, j, 0)),
                   pl.BlockSpec((1, rows, 2 * d), lambda i, j: (i, jnp.maximum(j - first, 0), 0))],
        out_shape=[jax.ShapeDtypeStruct((b, dil, t // dil, 3 * d), BF16), jax.ShapeDtypeStruct((b, keep, 2 * d), F32)],
        scratch_shapes=[pltpu.VMEM((d // LANES, tm, LANES), F32),
                        pltpu.VMEM((d // LANES, tm // min(dil, REGROUP_STRIDE), LANES), F32)] if dil > 1 else [],
        compiler_params=_params("parallel", "arbitrary"),
        name=f"qkv_prompt_d{dil}",
    )(x, g_pre, w_qkv)


def _qkv_sample_body(x_ref, g_pre_ref, w_ref, o_ref):
    h = _rms(x_ref[...], g_pre_ref[...]).astype(BF16)
    o_ref[...] = _dot(h, w_ref[...])


def _qkv_sample(x, g_pre, w_qkv):
    bd, _ = x.shape
    return pl.pallas_call(
        _qkv_sample_body,
        out_shape=jax.ShapeDtypeStruct((bd, w_qkv.shape[1]), F32),
        compiler_params=pltpu.CompilerParams(vmem_limit_bytes=VMEM_LIMIT_BYTES),
        name="qkv_sample",
    )(x, g_pre, w_qkv)


def _attn_prompt_body(*refs, dil, units, chained, last, rider_dil):
    refs = list(refs)
    q_ref, kc_ref, vc_ref, kp_ref, vp_ref, bias_ref = refs[:6]
    del refs[:6]
    if chained:
        op_ref, mp_ref, sp_ref = refs[:3]
        del refs[:3]
    if rider_dil is not None:
        rider_in = refs[:8]
        del refs[:8]
    o_ref = refs[0]
    del refs[:1]
    if not last:
        m_ref, s_ref = refs[:2]
        del refs[:2]
    n = pl.program_id(1)
    res = pl.program_id(2)

    if rider_dil is not None:
        qt_r, qh_r, kn_r, vn_r, kcache_r, vcache_r, bias_c_r, bias_n_r = rider_in
        ro_ref, rl_ref = refs[:2]
        step = (pl.program_id(0) * pl.num_programs(1) + n) * pl.num_programs(2) + res
        half = step % bias_c_r.shape[0]

        @pl.when(step >= 0)
        def _():
            ro_ref[0, 0], rl_ref[0, 0] = _sample_attn_heads(
                qt_r[0, 0], qh_r[0, 0], kn_r[0, 0], vn_r[0, 0], kcache_r.at[0], vcache_r.at[0],
                bias_c_r[half], bias_n_r[half], rider_dil)

    kj = lax.broadcasted_iota(jnp.int32, (2 * BLK, 2 * BLK), 0)
    qi = lax.broadcasted_iota(jnp.int32, (2 * BLK, 2 * BLK), 1) % BLK
    delta = qi + BLK - kj
    band = (delta >= 0) & (delta <= SPAN)
    first_key = jnp.where(n > 0, 0, BLK)
    band_first = band & (kj >= first_key)
    lane = lax.broadcasted_iota(jnp.int32, (BLK, HEAD_PAIR), 1)
    low = lane < HEAD_DIM
    n_pairs = q_ref.shape[-1] // HEAD_PAIR
    head_row = lax.broadcasted_iota(jnp.int32, (2 * n_pairs, BLK), 0)

    def unit_operands(u):
        if dil > 1:
            cat = lambda prev_ref, cur_ref: jnp.concatenate([prev_ref[0, u], cur_ref[0, u]], axis=0)
            return (q_ref[0, u], cat(kp_ref, kc_ref), cat(vp_ref, vc_ref), band_first,
                    pl.ds(res * units + u, BLK, stride=dil))
        own = slice(u * BLK, (u + 1) * BLK)
        if u == 0:
            cat = lambda prev_ref, cur_ref: jnp.concatenate([prev_ref[0, 0], cur_ref[0, 0, own]], axis=0)
            return q_ref[0, 0, own], cat(kp_ref, kc_ref), cat(vp_ref, vc_ref), band_first, own
        both = slice((u - 1) * BLK, (u + 1) * BLK)
        return q_ref[0, 0, own], kc_ref[0, 0, both], vc_ref[0, 0, both], band, own

    def pair_rows(stat_t):
        return jnp.stack([jnp.concatenate([stat_t[2 * p:2 * p + 1], stat_t[2 * p + 1:2 * p + 2]], axis=1)
                          for p in range(n_pairs)], axis=0)

    ops = [unit_operands(u) for u in range(units)]
    items = [(u, pair) for u in range(units) for pair in range(n_pairs)]
    st = []
    for u, pair in items:
        q, k, _, valid, _ = ops[u]
        sl = slice(pair * HEAD_PAIR, (pair + 1) * HEAD_PAIR)
        q2 = q[:, sl]
        qq = jnp.concatenate([jnp.where(low, q2, jnp.zeros_like(q2)), jnp.where(low, jnp.zeros_like(q2), q2)], axis=0)
        score = lax.dot_general(k[:, sl], qq, (((1,), (1,)), ((), ())), preferred_element_type=F32)
        st.append(jnp.where(valid, score + bias_ref[pair], NEG_INF))
    m_new = [jnp.max(x, axis=0, keepdims=True) for x in st]
    if chained:
        m_prev, s_prev = [], []
        for u in range(units):
            rows = ops[u][4]
            m_prev += list(pair_rows(mp_ref[0, rows, :].T))
            s_prev += list(pair_rows(sp_ref[0, rows, :].T))
        m_new = [jnp.maximum(a, b) for a, b in zip(m_new, m_prev)]
    p = [jnp.exp2(x - m) for x, m in zip(st, m_new)]
    s_new = [jnp.sum(x, axis=0, keepdims=True) for x in p]
    if chained:
        alpha = [jnp.exp2(a - b) for a, b in zip(m_prev, m_new)]
        s_new = [sp * a + sn for sp, a, sn in zip(s_prev, alpha, s_new)]
    p = [x.astype(BF16) for x in p]
    for u, (_, _, v, _, rows) in enumerate(ops):
        m_out_t = jnp.zeros((2 * n_pairs, BLK), F32)
        s_out_t = jnp.zeros((2 * n_pairs, BLK), F32)
        for pair in range(n_pairs):
            i = u * n_pairs + pair
            sl = slice(pair * HEAD_PAIR, (pair + 1) * HEAD_PAIR)
            pv = lax.dot_general(v[:, sl], p[i], (((0,), (0,)), ((), ())), preferred_element_type=F32)
            o_t = [pv[:HEAD_DIM, :BLK], pv[HEAD_DIM:, BLK:]]
            if chained:
                o_prev_t = op_ref[0, pair, rows, :].T
            for half in range(2):
                cols = slice(half * BLK, (half + 1) * BLK)
                if chained:
                    o_t[half] = o_prev_t[half * HEAD_DIM:(half + 1) * HEAD_DIM] * alpha[i][:, cols] + o_t[half]
                if last:
                    o_t[half] = o_t[half] / s_new[i][:, cols]
                else:
                    m_out_t = jnp.where(head_row == 2 * pair + half, m_new[i][:, cols], m_out_t)
                    s_out_t = jnp.where(head_row == 2 * pair + half, s_new[i][:, cols], s_out_t)
            o_ref[0, pair, rows, :] = jnp.concatenate(o_t, axis=0).T
        if not last:
            pad = jnp.zeros((BLK - 2 * n_pairs, BLK), F32)
            m_ref[0, rows, :] = jnp.concatenate([m_out_t, pad], axis=0).T
            s_ref[0, rows, :] = jnp.concatenate([s_out_t, pad], axis=0).T


def _attn_prompt(qkv, bias, dil, prev, last, units, rider=None):
    b, _, sub, d3 = qkv.shape
    d = d3 // 3
    t = sub * dil
    n_pairs = d // HEAD_PAIR
    assert sub % BLK == 0 and d % HEAD_PAIR == 0 and d // HEAD_DIM <= LANES
    if dil > 1:
        assert dil % units == 0
        grid = (b, sub // BLK, dil // units)
        span = BLK * dil
        blk = (1, units, BLK, d)
        col = lambda part: pl.BlockSpec(blk, lambda i, n, r: (i, r, n, part))
        col_prev = lambda part: pl.BlockSpec(blk, lambda i, n, r: (i, r, jnp.maximum(n - 1, 0), part))
    else:
        assert (sub // BLK) % units == 0
        grid = (b, sub // (BLK * units), 1)
        span = BLK * units
        col = lambda part: pl.BlockSpec((1, 1, span, d), lambda i, n, r: (i, 0, n, part))
        col_prev = lambda part: pl.BlockSpec((1, 1, BLK, d), lambda i, n, r: (i, 0, jnp.maximum(n * units - 1, 0), part))

    o_spec = pl.BlockSpec((1, n_pairs, span, HEAD_PAIR), lambda i, n, r: (i, 0, n, 0))
    st_spec = pl.BlockSpec((1, span, LANES), lambda i, n, r: (i, n, 0))
    in_specs = [col(0), col(1), col(2), col_prev(1), col_prev(2), _resident(bias.shape)]
    args = [qkv, qkv, qkv, qkv, qkv, bias]
    if prev is not None:
        in_specs += [o_spec, st_spec, st_spec]
        args += list(prev)
    o_shape = jax.ShapeDtypeStruct((b, n_pairs, t, HEAD_PAIR), F32)
    st_shape = jax.ShapeDtypeStruct((b, t, LANES), F32)
    out_specs = [o_spec] if last else [o_spec, st_spec, st_spec]
    out_shape = [o_shape] if last else [o_shape, st_shape, st_shape]
    n_own = len(out_shape)
    if rider is not None:
        bd, hh, dh, n_cache, layer, n_split = (rider[k] for k in ("bd", "hh", "dh", "n_cache", "layer", "n_split"))
        assert grid[0] * grid[1] * grid[2] == bd * n_split
        step = lambda i, n, r: (i * grid[1] + n) * grid[2] + r
        seq = lambda i, n, r: step(i, n, r) // n_split
        half = lambda i, n, r: step(i, n, r) % n_split
        small = lambda rows, cols: pl.BlockSpec((1, 1, rows, cols), lambda i, n, r: (seq(i, n, r), half(i, n, r), 0, 0))
        kv = lambda which: pl.BlockSpec((None, 1, None, hh, dh, n_cache),
                                        lambda i, n, r: (layer, seq(i, n, r), which, half(i, n, r), 0, 0))
        qt, qh, kn, vn, cache_t, bias_c, bias_n = rider["arrays"]
        in_specs += [small(dh, hh), small(hh, dh), small(hh, dh), small(dh, hh), kv(0), kv(1),
                     _resident(bias_c.shape), _resident(bias_n.shape)]
        args += [qt, qh, kn, vn, cache_t, cache_t, bias_c, bias_n]
        out_specs += [small(dh, hh), small(hh, 1)]
        out_shape += [jax.ShapeDtypeStruct((bd, n_split, dh, hh), F32), jax.ShapeDtypeStruct((bd, n_split, hh, 1), F32)]
    outs = pl.pallas_call(
        functools.partial(_attn_prompt_body, dil=dil, units=units, chained=prev is not None, last=last,
                          rider_dil=None if rider is None else rider["dil"]),
        grid=grid,
        in_specs=in_specs,
        out_specs=out_specs,
        out_shape=out_shape,
        compiler_params=_params("parallel", "arbitrary", "arbitrary"),
        name=f"attn_prompt_d{dil}",
    )(*args)
    return outs[:n_own], outs[n_own:]


def _bias_table(rel_bias, g, dil):
    onehot = np.eye(N_BUCKETS, dtype=np.float32)[_t5_bucket(np.arange(SPAN + 1) * dil)]
    return jnp.dot(rel_bias[:, g, :].astype(F32).T, jnp.asarray(onehot).T, precision=lax.Precision.HIGHEST)


def _prompt_bias(table):
    nh = table.shape[0]
    width = 3 * BLK - 1
    rev = jnp.concatenate([jnp.broadcast_to(table[:, SPAN:], (nh, BLK - 1)), table[:, ::-1],
                           jnp.broadcast_to(table[:, :1], (nh, BLK - 1)), jnp.zeros((nh, 1), F32)], axis=1)
    skew = jnp.tile(rev, (1, BLK))[:, :BLK * width].reshape(nh, BLK, width)
    bias = skew[:, :, BLK - 1:]
    bias_t = jnp.transpose(bias, (0, 2, 1)).reshape(nh // 2, 2, 2 * BLK, BLK)
    return jnp.transpose(bias_t, (0, 2, 1, 3)).reshape(nh // 2, 2 * BLK, 2 * BLK)


def _sample_attn_heads(q_t, q_h, kn, vn_t, kc_ref, vc_ref, bias_c, bias_n, dil):
    hh, n_cache = bias_c.shape
    pos = lax.broadcasted_iota(jnp.int32, (1, n_cache), 1)
    on_grid = pos % dil == 0
    head = lax.broadcasted_iota(jnp.int32, (1, hh), 1)
    scale = HEAD_DIM ** -0.5
    q_t = q_t * scale
    ln = jnp.sum(kn * (q_h * scale), axis=1, keepdims=True) + bias_n
    lc = jnp.concatenate([jnp.sum(kc_ref[h] * q_t[:, h:h + 1], axis=0, keepdims=True) for h in range(hh)], axis=0)
    lc = jnp.where(on_grid, lc + bias_c, NEG_INF)
    m = jnp.maximum(jnp.max(lc, axis=1, keepdims=True), ln)
    pc = jnp.exp(lc - m)
    pn = jnp.exp(ln - m)
    s = jnp.sum(pc, axis=1, keepdims=True) + pn
    o_all = jnp.zeros_like(q_t)
    for h in range(hh):
        o = jnp.sum(vc_ref[h] * pc[h:h + 1, :], axis=1, keepdims=True)
        o = (o + pn[h:h + 1, :] * vn_t[:, h:h + 1]) / s[h:h + 1, :]
        o_all = jnp.where(head == h, o, o_all)
    return o_all, m + jnp.log(s)


def _sample_rider(q, kn, vn, cache, layer, table, dil, n_split):
    bd, nh, dh = q.shape
    n_cache = cache.shape[2]
    assert n_cache == SPAN * dil and nh % n_split == 0
    hh = nh // n_split
    split = lambda a: a.reshape(bd, n_split, hh, dh)
    bias_c = jnp.repeat(table[:, :0:-1], dil, axis=1).reshape(n_split, hh, n_cache)
    bias_n = table[:, :1].reshape(n_split, hh, 1)
    return dict(arrays=(jnp.swapaxes(split(q), 2, 3), split(q), split(kn), jnp.swapaxes(split(vn), 2, 3),
                        jnp.transpose(cache, (0, 1, 3, 4, 5, 2)), bias_c, bias_n),
                layer=layer, dil=dil, bd=bd, hh=hh, dh=dh, n_cache=n_cache, n_split=n_split)


def _merge_sample_body(o0_ref, o1_ref, o2_ref, l0_ref, l1_ref, l2_ref, out_ref):
    lses = (l0_ref[...], l1_ref[...], l2_ref[...])
    outs = (o0_ref[...], o1_ref[...], o2_ref[...])
    mx = jnp.maximum(jnp.maximum(lses[0], lses[1]), lses[2])
    es = [jnp.exp(l - mx) for l in lses]
    tot = es[0] + es[1] + es[2]
    out_ref[...] = sum((e / tot) * o for e, o in zip(es, outs))


def _merge_sample(outs, lses):
    return pl.pallas_call(
        _merge_sample_body,
        out_shape=jax.ShapeDtypeStruct(outs[0].shape, F32),
        name="merge_sample",
    )(*outs, *lses)


def _wo_prompt_body(x_ref, o_ref_in, w_ref, g_post_ref, f0, f1, f2, f3, out_ref):
    o = jnp.concatenate([o_ref_in[0, p] for p in range(o_ref_in.shape[1])], axis=1)
    y = _dot(o.astype(BF16), w_ref[...])
    out_ref[0] = _ffn_tile(x_ref[0] + _rms(y, g_post_ref[...]), f0, f1, f2, f3)


def _wo_prompt(x, o, w_o, g_post, ffn, tm):
    b, t, d = x.shape
    row = pl.BlockSpec((1, tm, d), lambda i, j: (i, j, 0))
    return pl.pallas_call(
        _wo_prompt_body,
        grid=(b, t // tm),
        in_specs=[row, pl.BlockSpec((1, o.shape[1], tm, HEAD_PAIR), lambda i, j: (i, 0, j, 0)),
                  _resident(w_o.shape), _resident((1, d))] + _ffn_specs(ffn),
        out_specs=row,
        out_shape=jax.ShapeDtypeStruct((b, t, d), F32),
        compiler_params=_params("parallel", "parallel"),
        name="wo_ffn_prompt",
    )(x, o, w_o, g_post, *ffn)


def _wo_sample_body(x_ref, o_ref_in, w_ref, g_post_ref, out_ref):
    y = _dot(o_ref_in[...].astype(BF16), w_ref[...])
    out_ref[...] = x_ref[...] + _rms(y, g_post_ref[...])


def _wo_sample(x, o, w_o, g_post):
    return pl.pallas_call(
        _wo_sample_body,
        out_shape=jax.ShapeDtypeStruct(x.shape, F32),
        compiler_params=pltpu.CompilerParams(vmem_limit_bytes=VMEM_LIMIT_BYTES),
        name="wo_sample",
    )(x, o, w_o, g_post)


def kernel(x_prompt, x_sample, state_pool, cache_swa_g0, cache_swa_g1, cache_swa_g2, state_conv,
           norm_g, w_ffn_in, w_ffn_out, pool_w, pool_scale, w_qkv, w_o, rel_bias,
           conv_w_in, conv_b_in, conv_w_dw, conv_b_dw, conv_ln_g, conv_ln_b, conv_w_out, conv_b_out):
    b, t, d = x_prompt.shape
    bd, t_dec, _ = x_sample.shape
    assert t_dec == 1 and d % HEAD_PAIR == 0
    nh = d // HEAD_DIM
    depth = norm_g.shape[0]
    caches = (cache_swa_g0, cache_swa_g1, cache_swa_g2)

    tm = 512 if (b * t) % 512 == 0 else b * t
    ts = min(512, t)
    tq = min(1024, t)

    xp = x_prompt
    xs = x_sample.reshape(bd, d)
    vec = lambda a: a.reshape(1, -1).astype(F32)

    pool_p, pool_s, conv_p, conv_s = [], [], [], []
    swa_p = [[] for _ in range(N_GROUPS)]
    swa_s = [[] for _ in range(N_GROUPS)]
    for i in range(depth):
        kind, j = i % 3, i // 3
        g = [vec(norm_g[i, k]) for k in range(4)]
        ffn = (g[2], w_ffn_in[i].astype(BF16), w_ffn_out[i].astype(BF16), g[3])
        if kind == 0:
            pw = pool_w[j].astype(BF16)
            ps = vec(pool_scale[j])
            xp, tail = _pool_prompt(xp, jnp.zeros((b, POOL_HALO, d), F32), g[0], pw, ps, g[1], ffn, ts, 0)
            pool_p.append(tail[:, POOL_HALO - POOL_HIST:])
            hist = state_pool[j]
            xs, hs = _pool_sample(xs, jnp.swapaxes(hist, 0, 1), g[0], pw, ps, g[1], PAST_LEN)
            pool_s.append(jnp.concatenate([hist[:, 1:], hs[:, None, :]], axis=1))
        elif kind == 1:
            wq = w_qkv[j].astype(BF16)
            wo = w_o[j].astype(BF16)
            outs = [_qkv_prompt(xp, g[0], wq, gi, tq) for gi in range(N_GROUPS)]
            tables = [_bias_table(rel_bias, gi, dil) for gi, (_, dil) in enumerate(SWA_CONFIGS)]
            qkv_s = _qkv_sample(xs, g[0], wq).reshape(bd, N_GROUPS, 3, nh, HEAD_DIM)
            order = sorted(range(N_GROUPS), key=lambda gi: -SWA_CONFIGS[gi][1])
            prev = None
            o_s, lse_s = [None] * N_GROUPS, [None] * N_GROUPS
            attn_steps = b * (t // BLK) // ATTN_UNITS
            assert attn_steps % bd == 0
            for k, gi in enumerate(order):
                sg = N_GROUPS - 1 - gi
                rider = _sample_rider(qkv_s[:, sg, 0], qkv_s[:, sg, 1], qkv_s[:, sg, 2], caches[sg], j, tables[sg],
                                      SWA_CONFIGS[sg][1], attn_steps // bd)
                prev, (o_s[sg], l_g) = _attn_prompt(outs[gi][0], _prompt_bias(tables[gi] * LOG2_E), SWA_CONFIGS[gi][1],
                                                    prev, k == N_GROUPS - 1, ATTN_UNITS, rider)
                lse_s[sg] = jnp.swapaxes(l_g, 2, 3)
            for gi in range(N_GROUPS):
                swa_p[gi].append(outs[gi][1].reshape(b, -1, 2, nh, HEAD_DIM))
                swa_s[gi].append(qkv_s[:, gi, 1:3].reshape(bd, 1, 2, nh, HEAD_DIM))
            xp = _wo_prompt(xp, prev[0], wo, g[1], ffn, ts)
            merged = jnp.swapaxes(_merge_sample(o_s, lse_s), 2, 3).reshape(bd, d)
            xs = _wo_sample(xs, merged, wo, g[1])
        else:
            cw = (conv_w_in[j].astype(BF16), vec(conv_b_in[j]),
                  jnp.pad(conv_w_dw[j], ((0, CONV_HALO - CONV_WIDTH), (0, 0))), vec(conv_b_dw[j]),
                  vec(conv_ln_g[j]), vec(conv_ln_b[j]), conv_w_out[j].astype(BF16), vec(conv_b_out[j]))
            xp, tail = _conv_prompt(xp, jnp.zeros((b, CONV_HALO, d), F32), g[0], *cw, g[1], ts, 128)
            conv_p.append(tail[:, CONV_HALO - CONV_HIST:])
            hist = state_conv[j]
            xs, us = _conv_sample(xs, jnp.swapaxes(hist, 0, 1), g[0], *cw, g[1])
            conv_s.append(jnp.concatenate([hist[:, 1:], us[:, None, :]], axis=1))
            xp = _ffn(xp.reshape(b * t, d), ffn, tm).reshape(b, t, d)
        xs = _ffn(xs, ffn, bd)

    return (xp, xs.reshape(bd, 1, d),
            jnp.stack(pool_p), jnp.stack(pool_s),
            jnp.stack(swa_p[0]), jnp.stack(swa_p[1]), jnp.stack(swa_p[2]),
            jnp.stack(swa_s[0]), jnp.stack(swa_s[1]), jnp.stack(swa_s[2]),
            jnp.stack(conv_p), jnp.stack(conv_s))
```

```python
import functools
import math

import numpy as np
import jax
import jax.numpy as jnp
from jax import lax
from jax.experimental import pallas as pl
from jax.experimental.pallas import tpu as pltpu

F32 = jnp.float32
BF16 = jnp.bfloat16

RMS_EPS = 1e-6
LN_EPS = 1e-5
NEG_INF = -1e30
PAST_LEN = 8192
POOL_WINDOWS = (2, 4, 8, 16)
POOL_HIST = max(POOL_WINDOWS) - 1
SWA_CONFIGS = ((128, 1), (512, 4), (2048, 16))
N_GROUPS = len(SWA_CONFIGS)
SPAN = 128
BLK = 128
HEAD_DIM = 64
N_BUCKETS = 32
MAX_DISTANCE = 2048
CONV_WIDTH = 31
CONV_HIST = CONV_WIDTH - 1

LANES = 128
SUBLANES = 8
MXU_DIM = 256
VMEM_LIMIT_BYTES = 56 * 1024 * 1024
REGROUP_STRIDE = 4


POOL_HALO = 16
CONV_HALO = 32
HEAD_PAIR = 2 * HEAD_DIM
LOG2_E = math.log2(math.e)
ATTN_UNITS = 2


def _params(*sem):
    return pltpu.CompilerParams(dimension_semantics=sem, vmem_limit_bytes=VMEM_LIMIT_BYTES)


def _resident(shape):
    nd = len(shape)
    return pl.BlockSpec(shape, lambda *_: (0,) * nd, pipeline_mode=pl.Buffered(1))


def _rms(x, g):
    return x * lax.rsqrt(jnp.mean(x * x, axis=-1, keepdims=True) + RMS_EPS) * g


def _dot(a, b):
    return jnp.dot(a, b, preferred_element_type=F32)


def _t5_bucket(dist):
    max_exact = N_BUCKETS // 2
    d = np.maximum(np.asarray(dist), 0)
    large = max_exact + (np.log(np.maximum(d, max_exact) / max_exact) / math.log(MAX_DISTANCE / max_exact)
                         * (N_BUCKETS - max_exact)).astype(np.int64)
    large = np.minimum(large, N_BUCKETS - 1)
    return np.where(d < max_exact, d, large).astype(np.int32)


def _ff_chunks(d_ff):
    assert d_ff % MXU_DIM == 0
    n = d_ff // MXU_DIM
    first = (n + 1) // 2 * MXU_DIM
    return ((0, first), (first, d_ff - first)) if d_ff > first else ((0, d_ff),)


def _ffn_tile(x, g_pre_ref, w_in_ref, w_out_ref, g_post_ref):
    d_ff = w_out_ref.shape[0]
    h = _rms(x, g_pre_ref[...]).astype(BF16)
    acc = None
    for c0, cw in _ff_chunks(d_ff):
        gate = _dot(h, w_in_ref[:, c0:c0 + cw])
        up = _dot(h, w_in_ref[:, d_ff + c0:d_ff + c0 + cw])
        act = (gate * jax.nn.sigmoid(gate) * up).astype(BF16)
        part = _dot(act, w_out_ref[c0:c0 + cw, :])
        acc = part if acc is None else acc + part
    return x + _rms(acc, g_post_ref[...])


def _layer_resident(shape, layer):
    nd = len(shape)
    return pl.BlockSpec((None,) + tuple(shape[1:]), lambda *_: (layer,) + (0,) * (nd - 1), pipeline_mode=pl.Buffered(1))


def _ffn_specs(ffn):
    g_pre, w_in, w_out, g_post, layer = ffn
    return [_resident(g_pre.shape), _layer_resident(w_in.shape, layer), _layer_resident(w_out.shape, layer),
            _resident(g_post.shape)]


def _ffn_body(x_ref, g_pre_ref, w_in_ref, w_out_ref, g_post_ref, o_ref):
    o_ref[...] = _ffn_tile(x_ref[...], g_pre_ref, w_in_ref, w_out_ref, g_post_ref)


def _ffn(x, ffn, tm):
    n, d = x.shape
    assert n % tm == 0
    row = pl.BlockSpec((tm, d), lambda i: (i, 0))
    return pl.pallas_call(
        _ffn_body,
        grid=(n // tm,),
        in_specs=[row] + _ffn_specs(ffn),
        out_specs=row,
        out_shape=jax.ShapeDtypeStruct((n, d), F32),
        compiler_params=_params("parallel"),
        name="ffn",
    )(x, *ffn[:4])


def _pool_groups(h, window_sum, cnt_of, pw_ref, ps_ref):
    gd = h.shape[-1] // len(POOL_WINDOWS)
    outs = []
    for gi, w in enumerate(POOL_WINDOWS):
        c0 = gi * gd
        hg = h[:, c0:c0 + gd]
        dg = window_sum(gi, w, c0, gd, hg) / cnt_of(w) - hg
        outs.append(_dot(dg.astype(BF16), pw_ref[gi]))
    return jnp.concatenate(outs, axis=1) * ps_ref[...]


def _pool_prompt_body(x_ref, xprev_ref, hist_ref, g_pre_ref, pw_ref, ps_ref, g_post_ref, f0, f1, f2, f3,
                      o_ref, tail_ref, buf, *, ts, n_past):
    t = pl.program_id(1)
    x = x_ref[0]
    g_pre = g_pre_ref[...]
    h = _rms(x, g_pre)
    halo = jnp.where(t == 0, hist_ref[0], _rms(xprev_ref[0], g_pre))
    buf[0:POOL_HALO, :] = halo
    buf[POOL_HALO:POOL_HALO + ts, :] = h
    tail_ref[0] = buf[ts:ts + POOL_HALO, :]
    pos1 = n_past + 1 + t * ts + lax.broadcasted_iota(jnp.int32, (ts, 1), 0)

    def window_sum(gi, w, c0, gd, hg):
        assert w & (w - 1) == 0 and w <= POOL_HALO
        acc = buf[:, c0:c0 + gd]
        step = 1
        while step < w:
            acc = acc + pltpu.roll(acc, step, axis=0)
            step *= 2
        return acc[POOL_HALO:, :]

    y = _pool_groups(h, window_sum, lambda w: jnp.minimum(pos1, w).astype(F32), pw_ref, ps_ref)
    o_ref[0] = _ffn_tile(x + _rms(y, g_post_ref[...]), f0, f1, f2, f3)


def _pool_prompt(x, hist16, g_pre, pw, ps, g_post, ffn, ts, n_past):
    b, t, d = x.shape
    assert t % ts == 0 and ts % POOL_HALO == 0
    per = ts // POOL_HALO
    tile = pl.BlockSpec((1, ts, d), lambda i, j: (i, j, 0))
    halo = pl.BlockSpec((1, POOL_HALO, d), lambda i, j: (i, jnp.maximum(j * per - 1, 0), 0))
    first = pl.BlockSpec((1, POOL_HALO, d), lambda i, j: (i, 0, 0))
    return pl.pallas_call(
        functools.partial(_pool_prompt_body, ts=ts, n_past=n_past),
        grid=(b, t // ts),
        in_specs=[tile, halo, first, _resident((1, d)), _resident(pw.shape), _resident((1, d)), _resident((1, d))]
        + _ffn_specs(ffn),
        out_specs=[tile, first],
        out_shape=[jax.ShapeDtypeStruct((b, t, d), F32), jax.ShapeDtypeStruct((b, POOL_HALO, d), F32)],
        scratch_shapes=[pltpu.VMEM((POOL_HALO + ts, d), F32)],
        compiler_params=_params("parallel", "arbitrary"),
        name="pool_ffn_prompt",
    )(x, x, hist16, g_pre, pw, ps, g_post, *ffn[:4])


def _pool_sample_body(x_ref, hist_ref, g_pre_ref, pw_ref, ps_ref, g_post_ref, o_ref, h_ref, *, n_past):
    x = x_ref[...]
    h = _rms(x, g_pre_ref[...])
    h_ref[...] = h

    def window_sum(gi, w, c0, gd, hg):
        win = hg
        for k in range(1, w):
            win = win + hist_ref[POOL_HIST - k, :, c0:c0 + gd]
        return win

    y = _pool_groups(h, window_sum, lambda w: float(min(n_past + 1, w)), pw_ref, ps_ref)
    o_ref[...] = x + _rms(y, g_post_ref[...])


def _pool_sample(x, hist_t, g_pre, pw, ps, g_post, n_past):
    bd, d = x.shape
    return pl.pallas_call(
        functools.partial(_pool_sample_body, n_past=n_past),
        out_shape=[jax.ShapeDtypeStruct((bd, d), F32), jax.ShapeDtypeStruct((bd, d), F32)],
        compiler_params=pltpu.CompilerParams(vmem_limit_bytes=VMEM_LIMIT_BYTES),
        name="pool_sample",
    )(x, hist_t, g_pre, pw, ps, g_post)


def _glu(h, w_in_ref, b_in_ref):
    d = h.shape[-1]
    a = _dot(h, w_in_ref[...]) + b_in_ref[...]
    return a[:, :d] * jax.nn.sigmoid(a[:, d:])


def _conv_tail(c, x, ln_g_ref, ln_b_ref, w_out_ref, b_out_ref, g_post_ref):
    mu = jnp.mean(c, axis=-1, keepdims=True)
    cc = c - mu
    var = jnp.mean(cc * cc, axis=-1, keepdims=True)
    z = cc * lax.rsqrt(var + LN_EPS) * ln_g_ref[...] + ln_b_ref[...]
    y = _dot((z * jax.nn.sigmoid(z)).astype(BF16), w_out_ref[...]) + b_out_ref[...]
    return x + _rms(y, g_post_ref[...])


def _conv_prompt_body(x_ref, hist_ref, g_pre_ref, w_in_ref, b_in_ref, w_dw_ref, b_dw_ref, ln_g_ref, ln_b_ref,
                      w_out_ref, b_out_ref, g_post_ref, o_ref, tail_ref, ubuf, cbuf, *, ts, rc):
    t = pl.program_id(1)
    d = x_ref.shape[-1]

    @pl.when(t == 0)
    def _():
        ubuf[0:CONV_HALO, :] = hist_ref[0]

    @pl.when(t > 0)
    def _():
        ubuf[0:CONV_HALO, :] = ubuf[ts:ts + CONV_HALO, :]

    x = x_ref[0]
    h = _rms(x, g_pre_ref[...]).astype(BF16)
    ubuf[CONV_HALO:CONV_HALO + ts, :] = _glu(h, w_in_ref, b_in_ref)
    tail_ref[0] = ubuf[ts:ts + CONV_HALO, :]

    lead = CONV_HALO - CONV_HIST

    def chunk(i, carry):
        base = pl.multiple_of(i * rc, rc)
        for l0 in range(0, d, LANES):
            win = ubuf[pl.ds(base, rc + CONV_HALO), l0:l0 + LANES]
            acc = jnp.broadcast_to(b_dw_ref[:, l0:l0 + LANES], (rc, LANES))
            for phase in range(SUBLANES):
                offs = [o for o in range(lead, lead + CONV_WIDTH) if o % SUBLANES == phase]
                shifted = win if phase == 0 else pltpu.roll(win, rc + CONV_HALO - phase, axis=0)
                for o in offs:
                    acc = acc + shifted[o - phase:o - phase + rc, :] * w_dw_ref[o - lead:o - lead + 1, l0:l0 + LANES]
            cbuf[pl.ds(base, rc), l0:l0 + LANES] = acc
        return carry

    lax.fori_loop(0, ts // rc, chunk, 0)
    o_ref[0] = _conv_tail(cbuf[...], x, ln_g_ref, ln_b_ref, w_out_ref, b_out_ref, g_post_ref)


def _conv_prompt(x, hist32, g_pre, w_in, b_in, w_dw, b_dw, ln_g, ln_b, w_out, b_out, g_post, ts, rc):
    b, t, d = x.shape
    assert t % ts == 0 and ts % rc == 0 and ts >= CONV_HALO
    tile = pl.BlockSpec((1, ts, d), lambda i, j: (i, j, 0))
    first = pl.BlockSpec((1, CONV_HALO, d), lambda i, j: (i, 0, 0))
    vec = _resident((1, d))
    return pl.pallas_call(
        functools.partial(_conv_prompt_body, ts=ts, rc=rc),
        grid=(b, t // ts),
        in_specs=[tile, first, vec, _resident(w_in.shape), _resident(b_in.shape), _resident(w_dw.shape), vec, vec,
                  vec, _resident(w_out.shape), vec, vec],
        out_specs=[tile, first],
        out_shape=[jax.ShapeDtypeStruct((b, t, d), F32), jax.ShapeDtypeStruct((b, CONV_HALO, d), F32)],
        scratch_shapes=[pltpu.VMEM((CONV_HALO + ts, d), F32), pltpu.VMEM((ts, d), F32)],
        compiler_params=_params("parallel", "arbitrary"),
        name="conv_prompt",
    )(x, hist32, g_pre, w_in, b_in, w_dw, b_dw, ln_g, ln_b, w_out, b_out, g_post)


def _conv_sample_body(x_ref, hist_ref, g_pre_ref, w_in_ref, b_in_ref, w_dw_ref, b_dw_ref, ln_g_ref, ln_b_ref,
                      w_out_ref, b_out_ref, g_post_ref, o_ref, u_ref):
    x = x_ref[...]
    h = _rms(x, g_pre_ref[...]).astype(BF16)
    u = _glu(h, w_in_ref, b_in_ref)
    u_ref[...] = u
    c = b_dw_ref[...] + u * w_dw_ref[CONV_HIST:CONV_WIDTH, :]
    for k in range(CONV_HIST):
        c = c + hist_ref[k] * w_dw_ref[k:k + 1, :]
    o_ref[...] = _conv_tail(c, x, ln_g_ref, ln_b_ref, w_out_ref, b_out_ref, g_post_ref)


def _conv_sample(x, hist_t, g_pre, w_in, b_in, w_dw, b_dw, ln_g, ln_b, w_out, b_out, g_post):
    bd, d = x.shape
    return pl.pallas_call(
        _conv_sample_body,
        out_shape=[jax.ShapeDtypeStruct((bd, d), F32), jax.ShapeDtypeStruct((bd, d), F32)],
        compiler_params=pltpu.CompilerParams(vmem_limit_bytes=VMEM_LIMIT_BYTES),
        name="conv_sample",
    )(x, hist_t, g_pre, w_in, b_in, w_dw, b_dw, ln_g, ln_b, w_out, b_out, g_post)


def _qkv_prompt_body(x_ref, g_pre_ref, w_ref, qkv_ref, kv_ref, *scratch, tm, dil, rows, first):
    t = pl.program_id(1)
    d = x_ref.shape[-1]
    h = _rms(x_ref[0], g_pre_ref[...]).astype(BF16)
    for part in range(3):
        r = _dot(h, w_ref[:, part * d:(part + 1) * d])
        if part == 0:
            r = r * (HEAD_DIM ** -0.5 * LOG2_E)
        if part > 0:
            @pl.when(t >= first)
            def _(r=r, part=part):
                kv_ref[0, :, (part - 1) * d:part * d] = r[tm - rows:, :]
        if dil == 1:
            qkv_ref[0, 0, :, part * d:(part + 1) * d] = r.astype(BF16)
        else:
            rbuf, rbuf2 = scratch
            n_slab = d // LANES
            for s in range(n_slab):
                rbuf[s] = r[:, s * LANES:(s + 1) * LANES]
            outer = max(dil // REGROUP_STRIDE, 1)
            inner = dil // outer
            for a in range(inner):
                if outer == 1:
                    pieces = [[rbuf[s, pl.ds(a, tm // dil, stride=dil), :] for s in range(n_slab)]]
                else:
                    for s in range(n_slab):
                        rbuf2[s] = rbuf[s, pl.ds(a, tm // inner, stride=inner), :]
                    pieces = [[rbuf2[s, pl.ds(c, tm // dil, stride=outer), :] for s in range(n_slab)]
                              for c in range(outer)]
                for c, piece in enumerate(pieces):
                    qkv_ref[0, a + inner * c, :, part * d:(part + 1) * d] = jnp.concatenate(piece, axis=1).astype(BF16)


def _qkv_prompt(x, g_pre, w_qkv, group, tm):
    b, t, d = x.shape
    win, dil = SWA_CONFIGS[group]
    keep = min(win, t)
    rows = min(keep, tm)
    assert t % tm == 0 and keep % rows == 0 and tm % (dil * 16) == 0
    first = t // tm - (keep // rows if keep >= tm else 1)
    tile = pl.BlockSpec((1, tm, d), lambda i, j: (i, j, 0))
    w_spec = pl.BlockSpec((d, 3 * d), lambda i, j: (0, group), pipeline_mode=pl.Buffered(1))
    return pl.pallas_call(
        functools.partial(_qkv_prompt_body, tm=tm, dil=dil, rows=rows, first=first),
        grid=(b, t // tm),
        in_specs=[tile, _resident((1, d)), w_spec],
        out_specs=[pl.BlockSpec((1, dil, tm // dil, 3 * d), lambda i, j: (i, 0, j, 0)),
                   pl.BlockSpec((1, rows, 2 * d), lambda i, j: (i, jnp.maximum(j - first, 0), 0))],
        out_shape=[jax.ShapeDtypeStruct((b, dil, t // dil, 3 * d), BF16), jax.ShapeDtypeStruct((b, keep, 2 * d), F32)],
        scratch_shapes=[pltpu.VMEM((d // LANES, tm, LANES), F32),
                        pltpu.VMEM((d // LANES, tm // min(dil, REGROUP_STRIDE), LANES), F32)] if dil > 1 else [],
        compiler_params=_params("parallel", "arbitrary"),
        name=f"qkv_prompt_d{dil}",
    )(x, g_pre, w_qkv)


def _qkv_sample_body(x_ref, g_pre_ref, w_ref, o_ref):
    h = _rms(x_ref[...], g_pre_ref[...]).astype(BF16)
    o_ref[...] = _dot(h, w_ref[...])


def _qkv_sample(x, g_pre, w_qkv):
    bd, _ = x.shape
    return pl.pallas_call(
        _qkv_sample_body,
        out_shape=jax.ShapeDtypeStruct((bd, w_qkv.shape[1]), F32),
        compiler_params=pltpu.CompilerParams(vmem_limit_bytes=VMEM_LIMIT_BYTES),
        name="qkv_sample",
    )(x, g_pre, w_qkv)


def _attn_prompt_body(*refs, dil, units, chained, last, rider_dils):
    refs = list(refs)
    q_ref, kc_ref, vc_ref, kp_ref, vp_ref, bias_ref = refs[:6]
    del refs[:6]
    if chained:
        op_ref, mp_ref, sp_ref = refs[:3]
        del refs[:3]
    rider_in = [refs[8 * i:8 * (i + 1)] for i in range(len(rider_dils))]
    del refs[:8 * len(rider_dils)]
    o_ref = refs[0]
    del refs[:1]
    if not last:
        m_ref, s_ref = refs[:2]
        del refs[:2]
    n = pl.program_id(1)
    res = pl.program_id(2)

    if rider_dils:
        step = (pl.program_id(0) * pl.num_programs(1) + n) * pl.num_programs(2) + res

        @pl.when(step >= 0)
        def _():
            for i, rider_dil in enumerate(rider_dils):
                qt_r, qh_r, kn_r, vn_r, kcache_r, vcache_r, bias_c_r, bias_n_r = rider_in[i]
                ro_ref, rl_ref = refs[2 * i:2 * i + 2]
                half = step % bias_c_r.shape[0]
                ro_ref[0, 0], rl_ref[0, 0] = _sample_attn_heads(
                    qt_r[0, 0], qh_r[0, 0], kn_r[0, 0], vn_r[0, 0], kcache_r.at[0], vcache_r.at[0],
                    bias_c_r[half], bias_n_r[half], rider_dil)

    kj = lax.broadcasted_iota(jnp.int32, (2 * BLK, 2 * BLK), 0)
    qi = lax.broadcasted_iota(jnp.int32, (2 * BLK, 2 * BLK), 1) % BLK
    delta = qi + BLK - kj
    band = (delta >= 0) & (delta <= SPAN)
    first_key = jnp.where(n > 0, 0, BLK)
    band_first = band & (kj >= first_key)
    lane = lax.broadcasted_iota(jnp.int32, (BLK, HEAD_PAIR), 1)
    low = lane < HEAD_DIM
    n_pairs = q_ref.shape[-1] // HEAD_PAIR
    head_row = lax.broadcasted_iota(jnp.int32, (2 * n_pairs, BLK), 0)

    def unit_operands(u):
        if dil > 1:
            cat = lambda prev_ref, cur_ref: jnp.concatenate([prev_ref[0, u], cur_ref[0, u]], axis=0)
            return (q_ref[0, u], cat(kp_ref, kc_ref), cat(vp_ref, vc_ref), band_first,
                    pl.ds(res * units + u, BLK, stride=dil))
        own = slice(u * BLK, (u + 1) * BLK)
        if u == 0:
            cat = lambda prev_ref, cur_ref: jnp.concatenate([prev_ref[0, 0], cur_ref[0, 0, own]], axis=0)
            return q_ref[0, 0, own], cat(kp_ref, kc_ref), cat(vp_ref, vc_ref), band_first, own
        both = slice((u - 1) * BLK, (u + 1) * BLK)
        return q_ref[0, 0, own], kc_ref[0, 0, both], vc_ref[0, 0, both], band, own

    def pair_rows(stat_t):
        return jnp.stack([jnp.concatenate([stat_t[2 * p:2 * p + 1], stat_t[2 * p + 1:2 * p + 2]], axis=1)
                          for p in range(n_pairs)], axis=0)

    ops = [unit_operands(u) for u in range(units)]
    items = [(u, pair) for u in range(units) for pair in range(n_pairs)]
    st = []
    for u, pair in items:
        q, k, _, valid, _ = ops[u]
        sl = slice(pair * HEAD_PAIR, (pair + 1) * HEAD_PAIR)
        q2 = q[:, sl]
        qq = jnp.concatenate([jnp.where(low, q2, jnp.zeros_like(q2)), jnp.where(low, jnp.zeros_like(q2), q2)], axis=0)
        score = lax.dot_general(k[:, sl], qq, (((1,), (1,)), ((), ())), preferred_element_type=F32)
        st.append(jnp.where(valid, score + bias_ref[pair], NEG_INF))
    m_new = [jnp.max(x, axis=0, keepdims=True) for x in st]
    if chained:
        m_prev, s_prev = [], []
        for u in range(units):
            rows = ops[u][4]
            m_prev += list(pair_rows(mp_ref[0, rows, :].T))
            s_prev += list(pair_rows(sp_ref[0, rows, :].T))
        m_new = [jnp.maximum(a, b) for a, b in zip(m_new, m_prev)]
    p = [jnp.exp2(x - m) for x, m in zip(st, m_new)]
    s_new = [jnp.sum(x, axis=0, keepdims=True) for x in p]
    if chained:
        alpha = [jnp.exp2(a - b) for a, b in zip(m_prev, m_new)]
        s_new = [sp * a + sn for sp, a, sn in zip(s_prev, alpha, s_new)]
    p = [x.astype(BF16) for x in p]
    for u, (_, _, v, _, rows) in enumerate(ops):
        m_out_t = jnp.zeros((2 * n_pairs, BLK), F32)
        s_out_t = jnp.zeros((2 * n_pairs, BLK), F32)
        for pair in range(n_pairs):
            i = u * n_pairs + pair
            sl = slice(pair * HEAD_PAIR, (pair + 1) * HEAD_PAIR)
            pv = lax.dot_general(v[:, sl], p[i], (((0,), (0,)), ((), ())), preferred_element_type=F32)
            o_t = [pv[:HEAD_DIM, :BLK], pv[HEAD_DIM:, BLK:]]
            if chained:
                o_prev_t = op_ref[0, pair, rows, :].T
            for half in range(2):
                cols = slice(half * BLK, (half + 1) * BLK)
                if chained:
                    o_t[half] = o_prev_t[half * HEAD_DIM:(half + 1) * HEAD_DIM] * alpha[i][:, cols] + o_t[half]
                if last:
                    o_t[half] = o_t[half] / s_new[i][:, cols]
                else:
                    m_out_t = jnp.where(head_row == 2 * pair + half, m_new[i][:, cols], m_out_t)
                    s_out_t = jnp.where(head_row == 2 * pair + half, s_new[i][:, cols], s_out_t)
            o_ref[0, pair, rows, :] = jnp.concatenate(o_t, axis=0).T
        if not last:
            pad = jnp.zeros((BLK - 2 * n_pairs, BLK), F32)
            m_ref[0, rows, :] = jnp.concatenate([m_out_t, pad], axis=0).T
            s_ref[0, rows, :] = jnp.concatenate([s_out_t, pad], axis=0).T


def _rider_specs(rider, grid):
    bd, hh, dh, n_cache, layer, n_split = (rider[k] for k in ("bd", "hh", "dh", "n_cache", "layer", "n_split"))
    assert grid[0] * grid[1] * grid[2] == bd * n_split

    def item(i, n, r):
        step = (i * grid[1] + n) * grid[2] + r
        return step // n_split, step % n_split

    small = lambda rows, cols: pl.BlockSpec((1, 1, rows, cols), lambda i, n, r: (*item(i, n, r), 0, 0))

    def kv(which):
        def index(i, n, r):
            seq, sub = item(i, n, r)
            return layer, seq, which, sub, 0, 0
        return pl.BlockSpec((None, 1, None, hh, dh, n_cache), index)

    qt, qh, kn, vn, cache_t, bias_c, bias_n = rider["arrays"]
    in_specs = [small(dh, hh), small(hh, dh), small(hh, dh), small(dh, hh), kv(0), kv(1),
                _resident(bias_c.shape), _resident(bias_n.shape)]
    out_shape = [jax.ShapeDtypeStruct((bd, n_split, dh, hh), F32), jax.ShapeDtypeStruct((bd, n_split, hh, 1), F32)]
    return in_specs, [qt, qh, kn, vn, cache_t, cache_t, bias_c, bias_n], [small(dh, hh), small(hh, 1)], out_shape


def _attn_prompt(qkv, bias, dil, prev, last, units, riders=()):
    b, _, sub, d3 = qkv.shape
    d = d3 // 3
    t = sub * dil
    n_pairs = d // HEAD_PAIR
    assert sub % BLK == 0 and d % HEAD_PAIR == 0 and d // HEAD_DIM <= LANES
    if dil > 1:
        assert dil % units == 0
        grid = (b, sub // BLK, dil // units)
        span = BLK * dil
        blk = (1, units, BLK, d)
        col = lambda part: pl.BlockSpec(blk, lambda i, n, r: (i, r, n, part))
        col_prev = lambda part: pl.BlockSpec(blk, lambda i, n, r: (i, r, jnp.maximum(n - 1, 0), part))
    else:
        assert (sub // BLK) % units == 0
        grid = (b, sub // (BLK * units), 1)
        span = BLK * units
        col = lambda part: pl.BlockSpec((1, 1, span, d), lambda i, n, r: (i, 0, n, part))
        col_prev = lambda part: pl.BlockSpec((1, 1, BLK, d), lambda i, n, r: (i, 0, jnp.maximum(n * units - 1, 0), part))

    o_spec = pl.BlockSpec((1, n_pairs, span, HEAD_PAIR), lambda i, n, r: (i, 0, n, 0))
    st_spec = pl.BlockSpec((1, span, LANES), lambda i, n, r: (i, n, 0))
    in_specs = [col(0), col(1), col(2), col_prev(1), col_prev(2), _resident(bias.shape)]
    args = [qkv, qkv, qkv, qkv, qkv, bias]
    if prev is not None:
        in_specs += [o_spec, st_spec, st_spec]
        args += list(prev)
    o_shape = jax.ShapeDtypeStruct((b, n_pairs, t, HEAD_PAIR), F32)
    st_shape = jax.ShapeDtypeStruct((b, t, LANES), F32)
    out_specs = [o_spec] if last else [o_spec, st_spec, st_spec]
    out_shape = [o_shape] if last else [o_shape, st_shape, st_shape]
    n_own = len(out_shape)
    for rider in riders:
        r_in, r_args, r_out, r_shape = _rider_specs(rider, grid)
        in_specs += r_in
        args += r_args
        out_specs += r_out
        out_shape += r_shape
    outs = pl.pallas_call(
        functools.partial(_attn_prompt_body, dil=dil, units=units, chained=prev is not None, last=last,
                          rider_dils=tuple(rider["dil"] for rider in riders)),
        grid=grid,
        in_specs=in_specs,
        out_specs=out_specs,
        out_shape=out_shape,
        compiler_params=_params("parallel", "arbitrary", "arbitrary"),
        name=f"attn_prompt_d{dil}",
    )(*args)
    return outs[:n_own], outs[n_own:]


def _bias_table(rel_bias, g, dil):
    onehot = np.eye(N_BUCKETS, dtype=np.float32)[_t5_bucket(np.arange(SPAN + 1) * dil)]
    return jnp.dot(rel_bias[:, g, :].astype(F32).T, jnp.asarray(onehot).T, precision=lax.Precision.HIGHEST)


def _prompt_bias(table):
    nh = table.shape[0]
    width = 3 * BLK - 1
    rev = jnp.concatenate([jnp.broadcast_to(table[:, SPAN:], (nh, BLK - 1)), table[:, ::-1],
                           jnp.broadcast_to(table[:, :1], (nh, BLK - 1)), jnp.zeros((nh, 1), F32)], axis=1)
    skew = jnp.tile(rev, (1, BLK))[:, :BLK * width].reshape(nh, BLK, width)
    bias = skew[:, :, BLK - 1:]
    bias_t = jnp.transpose(bias, (0, 2, 1)).reshape(nh // 2, 2, 2 * BLK, BLK)
    return jnp.transpose(bias_t, (0, 2, 1, 3)).reshape(nh // 2, 2 * BLK, 2 * BLK)


def _sample_attn_heads(q_t, q_h, kn, vn_t, kc_ref, vc_ref, bias_c, bias_n, dil):
    hh, n_cache = bias_c.shape
    pos = lax.broadcasted_iota(jnp.int32, (1, n_cache), 1)
    on_grid = pos % dil == 0
    head = lax.broadcasted_iota(jnp.int32, (1, hh), 1)
    scale = HEAD_DIM ** -0.5
    q_t = q_t * scale
    ln = jnp.sum(kn * (q_h * scale), axis=1, keepdims=True) + bias_n
    lc = jnp.concatenate([jnp.sum(kc_ref[h] * q_t[:, h:h + 1], axis=0, keepdims=True) for h in range(hh)], axis=0)
    lc = jnp.where(on_grid, lc + bias_c, NEG_INF)
    m = jnp.maximum(jnp.max(lc, axis=1, keepdims=True), ln)
    pc = jnp.exp(lc - m)
    pn = jnp.exp(ln - m)
    s = jnp.sum(pc, axis=1, keepdims=True) + pn
    o_all = jnp.zeros_like(q_t)
    for h in range(hh):
        o = jnp.sum(vc_ref[h] * pc[h:h + 1, :], axis=1, keepdims=True)
        o = (o + pn[h:h + 1, :] * vn_t[:, h:h + 1]) / s[h:h + 1, :]
        o_all = jnp.where(head == h, o, o_all)
    return o_all, m + jnp.log(s)


def _sample_rider(q, kn, vn, cache, layer, table, dil, n_split):
    bd, nh, dh = q.shape
    n_cache = cache.shape[2]
    assert n_cache == SPAN * dil and nh % n_split == 0
    hh = nh // n_split
    split = lambda a: a.reshape(bd, n_split, hh, dh)
    bias_c = jnp.repeat(table[:, :0:-1], dil, axis=1).reshape(n_split, hh, n_cache)
    bias_n = table[:, :1].reshape(n_split, hh, 1)
    return dict(arrays=(jnp.swapaxes(split(q), 2, 3), split(q), split(kn), jnp.swapaxes(split(vn), 2, 3),
                        jnp.transpose(cache, (0, 1, 3, 4, 5, 2)), bias_c, bias_n),
                layer=layer, dil=dil, bd=bd, hh=hh, dh=dh, n_cache=n_cache, n_split=n_split)


def _merge_sample_body(o0_ref, o1_ref, o2_ref, l0_ref, l1_ref, l2_ref, out_ref):
    lses = (l0_ref[...], l1_ref[...], l2_ref[...])
    outs = (o0_ref[...], o1_ref[...], o2_ref[...])
    mx = jnp.maximum(jnp.maximum(lses[0], lses[1]), lses[2])
    es = [jnp.exp(l - mx) for l in lses]
    tot = es[0] + es[1] + es[2]
    out_ref[...] = sum((e / tot) * o for e, o in zip(es, outs))


def _merge_sample(outs, lses):
    return pl.pallas_call(
        _merge_sample_body,
        out_shape=jax.ShapeDtypeStruct(outs[0].shape, F32),
        name="merge_sample",
    )(*outs, *lses)


def _wo_prompt_body(x_ref, o_ref_in, w_ref, g_post_ref, f0, f1, f2, f3, out_ref):
    o = jnp.concatenate([o_ref_in[0, p] for p in range(o_ref_in.shape[1])], axis=1)
    y = _dot(o.astype(BF16), w_ref[...])
    out_ref[0] = _ffn_tile(x_ref[0] + _rms(y, g_post_ref[...]), f0, f1, f2, f3)


def _wo_prompt(x, o, w_o, g_post, ffn, tm):
    b, t, d = x.shape
    row = pl.BlockSpec((1, tm, d), lambda i, j: (i, j, 0))
    return pl.pallas_call(
        _wo_prompt_body,
        grid=(b, t // tm),
        in_specs=[row, pl.BlockSpec((1, o.shape[1], tm, HEAD_PAIR), lambda i, j: (i, 0, j, 0)),
                  _resident(w_o.shape), _resident((1, d))] + _ffn_specs(ffn),
        out_specs=row,
        out_shape=jax.ShapeDtypeStruct((b, t, d), F32),
        compiler_params=_params("parallel", "parallel"),
        name="wo_ffn_prompt",
    )(x, o, w_o, g_post, *ffn[:4])


def _wo_sample_body(x_ref, o_ref_in, w_ref, g_post_ref, out_ref):
    y = _dot(o_ref_in[...].astype(BF16), w_ref[...])
    out_ref[...] = x_ref[...] + _rms(y, g_post_ref[...])


def _wo_sample(x, o, w_o, g_post):
    return pl.pallas_call(
        _wo_sample_body,
        out_shape=jax.ShapeDtypeStruct(x.shape, F32),
        compiler_params=pltpu.CompilerParams(vmem_limit_bytes=VMEM_LIMIT_BYTES),
        name="wo_sample",
    )(x, o, w_o, g_post)


def kernel(x_prompt, x_sample, state_pool, cache_swa_g0, cache_swa_g1, cache_swa_g2, state_conv,
           norm_g, w_ffn_in, w_ffn_out, pool_w, pool_scale, w_qkv, w_o, rel_bias,
           conv_w_in, conv_b_in, conv_w_dw, conv_b_dw, conv_ln_g, conv_ln_b, conv_w_out, conv_b_out):
    b, t, d = x_prompt.shape
    bd, t_dec, _ = x_sample.shape
    assert t_dec == 1 and d % HEAD_PAIR == 0
    nh = d // HEAD_DIM
    depth = norm_g.shape[0]
    caches = (cache_swa_g0, cache_swa_g1, cache_swa_g2)

    tm = 512 if (b * t) % 512 == 0 else b * t
    ts = min(512, t)
    tq = min(1024, t)

    xp = x_prompt
    xs = x_sample.reshape(bd, d)
    vec = lambda a: a.reshape(1, -1).astype(F32)

    w_ffn_in_bf16 = w_ffn_in.astype(BF16)
    w_ffn_out_bf16 = w_ffn_out.astype(BF16)
    pool_p, pool_s, conv_p, conv_s = [], [], [], []
    swa_p = [[] for _ in range(N_GROUPS)]
    swa_s = [[] for _ in range(N_GROUPS)]
    for i in range(depth):
        kind, j = i % 3, i // 3
        g = [vec(norm_g[i, k]) for k in range(4)]
        ffn = (g[2], w_ffn_in_bf16, w_ffn_out_bf16, g[3], i)
        if kind == 0:
            pw = pool_w[j].astype(BF16)
            ps = vec(pool_scale[j])
            xp, tail = _pool_prompt(xp, jnp.zeros((b, POOL_HALO, d), F32), g[0], pw, ps, g[1], ffn, ts, 0)
            pool_p.append(tail[:, POOL_HALO - POOL_HIST:])
            hist = state_pool[j]
            xs, hs = _pool_sample(xs, jnp.swapaxes(hist, 0, 1), g[0], pw, ps, g[1], PAST_LEN)
            pool_s.append(jnp.concatenate([hist[:, 1:], hs[:, None, :]], axis=1))
        elif kind == 1:
            wq = w_qkv[j].astype(BF16)
            wo = w_o[j].astype(BF16)
            outs = [_qkv_prompt(xp, g[0], wq, gi, tq) for gi in range(N_GROUPS)]
            tables = [_bias_table(rel_bias, gi, dil) for gi, (_, dil) in enumerate(SWA_CONFIGS)]
            qkv_s = _qkv_sample(xs, g[0], wq).reshape(bd, N_GROUPS, 3, nh, HEAD_DIM)
            order = sorted(range(N_GROUPS), key=lambda gi: -SWA_CONFIGS[gi][1])
            prev = None
            o_s, lse_s = [None] * N_GROUPS, [None] * N_GROUPS
            attn_steps = b * (t // BLK) // ATTN_UNITS
            assert attn_steps % bd == 0
            by_size = sorted(range(N_GROUPS), key=lambda sg: SWA_CONFIGS[sg][1])
            riding = {order[0]: by_size[:-1], order[-1]: by_size[-1:]}
            for k, gi in enumerate(order):
                riders = [_sample_rider(qkv_s[:, sg, 0], qkv_s[:, sg, 1], qkv_s[:, sg, 2], caches[sg], j, tables[sg],
                                        SWA_CONFIGS[sg][1], attn_steps // bd) for sg in riding.get(gi, ())]
                prev, rider_outs = _attn_prompt(outs[gi][0], _prompt_bias(tables[gi] * LOG2_E), SWA_CONFIGS[gi][1],
                                                prev, k == N_GROUPS - 1, ATTN_UNITS, riders)
                for i, sg in enumerate(riding.get(gi, ())):
                    o_s[sg] = rider_outs[2 * i]
                    lse_s[sg] = jnp.swapaxes(rider_outs[2 * i + 1], 2, 3)
            for gi in range(N_GROUPS):
                swa_p[gi].append(outs[gi][1].reshape(b, -1, 2, nh, HEAD_DIM))
                swa_s[gi].append(qkv_s[:, gi, 1:3].reshape(bd, 1, 2, nh, HEAD_DIM))
            xp = _wo_prompt(xp, prev[0], wo, g[1], ffn, ts)
            merged = jnp.swapaxes(_merge_sample(o_s, lse_s), 2, 3).reshape(bd, d)
            xs = _wo_sample(xs, merged, wo, g[1])
        else:
            cw = (conv_w_in[j].astype(BF16), vec(conv_b_in[j]),
                  jnp.pad(conv_w_dw[j], ((0, CONV_HALO - CONV_WIDTH), (0, 0))), vec(conv_b_dw[j]),
                  vec(conv_ln_g[j]), vec(conv_ln_b[j]), conv_w_out[j].astype(BF16), vec(conv_b_out[j]))
            xp, tail = _conv_prompt(xp, jnp.zeros((b, CONV_HALO, d), F32), g[0], *cw, g[1], ts, 128)
            conv_p.append(tail[:, CONV_HALO - CONV_HIST:])
            hist = state_conv[j]
            xs, us = _conv_sample(xs, jnp.swapaxes(hist, 0, 1), g[0], *cw, g[1])
            conv_s.append(jnp.concatenate([hist[:, 1:], us[:, None, :]], axis=1))
            xp = _ffn(xp.reshape(b * t, d), ffn, tm).reshape(b, t, d)
        xs = _ffn(xs, ffn, bd)

    return (xp, xs.reshape(bd, 1, d),
            jnp.stack(pool_p), jnp.stack(pool_s),
            jnp.stack(swa_p[0]), jnp.stack(swa_p[1]), jnp.stack(swa_p[2]),
            jnp.stack(swa_s[0]), jnp.stack(swa_s[1]), jnp.stack(swa_s[2]),
            jnp.stack(conv_p), jnp.stack(conv_s))
```

```python
import functools
import math

import numpy as np
import jax
import jax.numpy as jnp
from jax import lax
from jax.experimental import pallas as pl
from jax.experimental.pallas import tpu as pltpu

F32 = jnp.float32
BF16 = jnp.bfloat16

RMS_EPS = 1e-6
LN_EPS = 1e-5
NEG_INF = -1e30
PAST_LEN = 8192
POOL_WINDOWS = (2, 4, 8, 16)
POOL_HIST = max(POOL_WINDOWS) - 1
SWA_CONFIGS = ((128, 1), (512, 4), (2048, 16))
N_GROUPS = len(SWA_CONFIGS)
SPAN = 128
BLK = 128
HEAD_DIM = 64
N_BUCKETS = 32
MAX_DISTANCE = 2048
CONV_WIDTH = 31
CONV_HIST = CONV_WIDTH - 1

LANES = 128
SUBLANES = 8
MXU_DIM = 256
VMEM_LIMIT_BYTES = 56 * 1024 * 1024
REGROUP_STRIDE = 4


POOL_HALO = 16
CONV_HALO = 32
HEAD_PAIR = 2 * HEAD_DIM
LOG2_E = math.log2(math.e)
ATTN_UNITS = 2
RIDER_INPUTS = 7


def _params(*sem):
    return pltpu.CompilerParams(dimension_semantics=sem, vmem_limit_bytes=VMEM_LIMIT_BYTES)


def _resident(shape):
    nd = len(shape)
    return pl.BlockSpec(shape, lambda *_: (0,) * nd, pipeline_mode=pl.Buffered(1))


def _rms(x, g):
    return x * lax.rsqrt(jnp.mean(x * x, axis=-1, keepdims=True) + RMS_EPS) * g


def _dot(a, b):
    return jnp.dot(a, b, preferred_element_type=F32)


def _t5_bucket(dist):
    max_exact = N_BUCKETS // 2
    d = np.maximum(np.asarray(dist), 0)
    large = max_exact + (np.log(np.maximum(d, max_exact) / max_exact) / math.log(MAX_DISTANCE / max_exact)
                         * (N_BUCKETS - max_exact)).astype(np.int64)
    large = np.minimum(large, N_BUCKETS - 1)
    return np.where(d < max_exact, d, large).astype(np.int32)


def _ff_chunks(d_ff):
    assert d_ff % MXU_DIM == 0
    n = d_ff // MXU_DIM
    first = (n + 1) // 2 * MXU_DIM
    return ((0, first), (first, d_ff - first)) if d_ff > first else ((0, d_ff),)


def _ffn_tile(x, g_pre_ref, w_in_ref, w_out_ref, g_post_ref):
    d_ff = w_out_ref.shape[0]
    h = _rms(x, g_pre_ref[...]).astype(BF16)
    acc = None
    for c0, cw in _ff_chunks(d_ff):
        gate = _dot(h, w_in_ref[:, c0:c0 + cw])
        up = _dot(h, w_in_ref[:, d_ff + c0:d_ff + c0 + cw])
        act = (gate * jax.nn.sigmoid(gate) * up).astype(BF16)
        part = _dot(act, w_out_ref[c0:c0 + cw, :])
        acc = part if acc is None else acc + part
    return x + _rms(acc, g_post_ref[...])


def _layer_resident(shape, layer):
    nd = len(shape)
    return pl.BlockSpec((None,) + tuple(shape[1:]), lambda *_: (layer,) + (0,) * (nd - 1), pipeline_mode=pl.Buffered(1))


def _ffn_specs(ffn):
    g_pre, w_in, w_out, g_post, layer = ffn
    return [_resident(g_pre.shape), _layer_resident(w_in.shape, layer), _layer_resident(w_out.shape, layer),
            _resident(g_post.shape)]


def _ffn_body(x_ref, g_pre_ref, w_in_ref, w_out_ref, g_post_ref, o_ref):
    o_ref[...] = _ffn_tile(x_ref[...], g_pre_ref, w_in_ref, w_out_ref, g_post_ref)


def _ffn(x, ffn, tm):
    n, d = x.shape
    assert n % tm == 0
    row = pl.BlockSpec((tm, d), lambda i: (i, 0))
    return pl.pallas_call(
        _ffn_body,
        grid=(n // tm,),
        in_specs=[row] + _ffn_specs(ffn),
        out_specs=row,
        out_shape=jax.ShapeDtypeStruct((n, d), F32),
        compiler_params=_params("parallel"),
        name="ffn",
    )(x, *ffn[:4])


def _pool_groups(h, window_sum, cnt_of, pw_ref, ps_ref):
    gd = h.shape[-1] // len(POOL_WINDOWS)
    outs = []
    for gi, w in enumerate(POOL_WINDOWS):
        c0 = gi * gd
        hg = h[:, c0:c0 + gd]
        dg = window_sum(gi, w, c0, gd, hg) / cnt_of(w) - hg
        outs.append(_dot(dg.astype(BF16), pw_ref[gi]))
    return jnp.concatenate(outs, axis=1) * ps_ref[...]


def _pool_prompt_body(x_ref, xprev_ref, hist_ref, g_pre_ref, pw_ref, ps_ref, g_post_ref, f0, f1, f2, f3,
                      o_ref, tail_ref, buf, *, ts, n_past):
    t = pl.program_id(1)
    x = x_ref[0]
    g_pre = g_pre_ref[...]
    h = _rms(x, g_pre)
    halo = jnp.where(t == 0, hist_ref[0], _rms(xprev_ref[0], g_pre))
    buf[0:POOL_HALO, :] = halo
    buf[POOL_HALO:POOL_HALO + ts, :] = h
    tail_ref[0] = buf[ts:ts + POOL_HALO, :]
    pos1 = n_past + 1 + t * ts + lax.broadcasted_iota(jnp.int32, (ts, 1), 0)

    def window_sum(gi, w, c0, gd, hg):
        assert w & (w - 1) == 0 and w <= POOL_HALO
        acc = buf[:, c0:c0 + gd]
        step = 1
        while step < w:
            acc = acc + pltpu.roll(acc, step, axis=0)
            step *= 2
        return acc[POOL_HALO:, :]

    y = _pool_groups(h, window_sum, lambda w: jnp.minimum(pos1, w).astype(F32), pw_ref, ps_ref)
    o_ref[0] = _ffn_tile(x + _rms(y, g_post_ref[...]), f0, f1, f2, f3)


def _pool_prompt(x, hist16, g_pre, pw, ps, g_post, ffn, ts, n_past):
    b, t, d = x.shape
    assert t % ts == 0 and ts % POOL_HALO == 0
    per = ts // POOL_HALO
    tile = pl.BlockSpec((1, ts, d), lambda i, j: (i, j, 0))
    halo = pl.BlockSpec((1, POOL_HALO, d), lambda i, j: (i, jnp.maximum(j * per - 1, 0), 0))
    first = pl.BlockSpec((1, POOL_HALO, d), lambda i, j: (i, 0, 0))
    return pl.pallas_call(
        functools.partial(_pool_prompt_body, ts=ts, n_past=n_past),
        grid=(b, t // ts),
        in_specs=[tile, halo, first, _resident((1, d)), _resident(pw.shape), _resident((1, d)), _resident((1, d))]
        + _ffn_specs(ffn),
        out_specs=[tile, first],
        out_shape=[jax.ShapeDtypeStruct((b, t, d), F32), jax.ShapeDtypeStruct((b, POOL_HALO, d), F32)],
        scratch_shapes=[pltpu.VMEM((POOL_HALO + ts, d), F32)],
        compiler_params=_params("parallel", "arbitrary"),
        name="pool_ffn_prompt",
    )(x, x, hist16, g_pre, pw, ps, g_post, *ffn[:4])


def _pool_sample_body(x_ref, hist_ref, g_pre_ref, pw_ref, ps_ref, g_post_ref, o_ref, h_ref, *, n_past):
    x = x_ref[...]
    h = _rms(x, g_pre_ref[...])
    h_ref[...] = h

    def window_sum(gi, w, c0, gd, hg):
        win = hg
        for k in range(1, w):
            win = win + hist_ref[POOL_HIST - k, :, c0:c0 + gd]
        return win

    y = _pool_groups(h, window_sum, lambda w: float(min(n_past + 1, w)), pw_ref, ps_ref)
    o_ref[...] = x + _rms(y, g_post_ref[...])


def _pool_sample(x, hist_t, g_pre, pw, ps, g_post, n_past):
    bd, d = x.shape
    return pl.pallas_call(
        functools.partial(_pool_sample_body, n_past=n_past),
        out_shape=[jax.ShapeDtypeStruct((bd, d), F32), jax.ShapeDtypeStruct((bd, d), F32)],
        compiler_params=pltpu.CompilerParams(vmem_limit_bytes=VMEM_LIMIT_BYTES),
        name="pool_sample",
    )(x, hist_t, g_pre, pw, ps, g_post)


def _glu(h, w_in_ref, b_in_ref):
    d = h.shape[-1]
    a = _dot(h, w_in_ref[...]) + b_in_ref[...]
    return a[:, :d] * jax.nn.sigmoid(a[:, d:])


def _conv_tail(c, x, ln_g_ref, ln_b_ref, w_out_ref, b_out_ref, g_post_ref):
    mu = jnp.mean(c, axis=-1, keepdims=True)
    cc = c - mu
    var = jnp.mean(cc * cc, axis=-1, keepdims=True)
    z = cc * lax.rsqrt(var + LN_EPS) * ln_g_ref[...] + ln_b_ref[...]
    y = _dot((z * jax.nn.sigmoid(z)).astype(BF16), w_out_ref[...]) + b_out_ref[...]
    return x + _rms(y, g_post_ref[...])


def _conv_prompt_body(x_ref, hist_ref, g_pre_ref, w_in_ref, b_in_ref, w_dw_ref, b_dw_ref, ln_g_ref, ln_b_ref,
                      w_out_ref, b_out_ref, g_post_ref, o_ref, tail_ref, ubuf, cbuf, *, ts, rc):
    t = pl.program_id(1)
    d = x_ref.shape[-1]

    @pl.when(t == 0)
    def _():
        ubuf[0:CONV_HALO, :] = hist_ref[0]

    @pl.when(t > 0)
    def _():
        ubuf[0:CONV_HALO, :] = ubuf[ts:ts + CONV_HALO, :]

    x = x_ref[0]
    h = _rms(x, g_pre_ref[...]).astype(BF16)
    ubuf[CONV_HALO:CONV_HALO + ts, :] = _glu(h, w_in_ref, b_in_ref)
    tail_ref[0] = ubuf[ts:ts + CONV_HALO, :]

    lead = CONV_HALO - CONV_HIST

    def chunk(i, carry):
        base = pl.multiple_of(i * rc, rc)
        for l0 in range(0, d, LANES):
            win = ubuf[pl.ds(base, rc + CONV_HALO), l0:l0 + LANES]
            acc = jnp.broadcast_to(b_dw_ref[:, l0:l0 + LANES], (rc, LANES))
            for phase in range(SUBLANES):
                offs = [o for o in range(lead, lead + CONV_WIDTH) if o % SUBLANES == phase]
                shifted = win if phase == 0 else pltpu.roll(win, rc + CONV_HALO - phase, axis=0)
                for o in offs:
                    acc = acc + shifted[o - phase:o - phase + rc, :] * w_dw_ref[o - lead:o - lead + 1, l0:l0 + LANES]
            cbuf[pl.ds(base, rc), l0:l0 + LANES] = acc
        return carry

    lax.fori_loop(0, ts // rc, chunk, 0)
    o_ref[0] = _conv_tail(cbuf[...], x, ln_g_ref, ln_b_ref, w_out_ref, b_out_ref, g_post_ref)


def _conv_prompt(x, hist32, g_pre, w_in, b_in, w_dw, b_dw, ln_g, ln_b, w_out, b_out, g_post, ts, rc):
    b, t, d = x.shape
    assert t % ts == 0 and ts % rc == 0 and ts >= CONV_HALO
    tile = pl.BlockSpec((1, ts, d), lambda i, j: (i, j, 0))
    first = pl.BlockSpec((1, CONV_HALO, d), lambda i, j: (i, 0, 0))
    vec = _resident((1, d))
    return pl.pallas_call(
        functools.partial(_conv_prompt_body, ts=ts, rc=rc),
        grid=(b, t // ts),
        in_specs=[tile, first, vec, _resident(w_in.shape), _resident(b_in.shape), _resident(w_dw.shape), vec, vec,
                  vec, _resident(w_out.shape), vec, vec],
        out_specs=[tile, first],
        out_shape=[jax.ShapeDtypeStruct((b, t, d), F32), jax.ShapeDtypeStruct((b, CONV_HALO, d), F32)],
        scratch_shapes=[pltpu.VMEM((CONV_HALO + ts, d), F32), pltpu.VMEM((ts, d), F32)],
        compiler_params=_params("parallel", "arbitrary"),
        name="conv_prompt",
    )(x, hist32, g_pre, w_in, b_in, w_dw, b_dw, ln_g, ln_b, w_out, b_out, g_post)


def _conv_sample_body(x_ref, hist_ref, g_pre_ref, w_in_ref, b_in_ref, w_dw_ref, b_dw_ref, ln_g_ref, ln_b_ref,
                      w_out_ref, b_out_ref, g_post_ref, o_ref, u_ref):
    x = x_ref[...]
    h = _rms(x, g_pre_ref[...]).astype(BF16)
    u = _glu(h, w_in_ref, b_in_ref)
    u_ref[...] = u
    c = b_dw_ref[...] + u * w_dw_ref[CONV_HIST:CONV_WIDTH, :]
    for k in range(CONV_HIST):
        c = c + hist_ref[k] * w_dw_ref[k:k + 1, :]
    o_ref[...] = _conv_tail(c, x, ln_g_ref, ln_b_ref, w_out_ref, b_out_ref, g_post_ref)


def _conv_sample(x, hist_t, g_pre, w_in, b_in, w_dw, b_dw, ln_g, ln_b, w_out, b_out, g_post):
    bd, d = x.shape
    return pl.pallas_call(
        _conv_sample_body,
        out_shape=[jax.ShapeDtypeStruct((bd, d), F32), jax.ShapeDtypeStruct((bd, d), F32)],
        compiler_params=pltpu.CompilerParams(vmem_limit_bytes=VMEM_LIMIT_BYTES),
        name="conv_sample",
    )(x, hist_t, g_pre, w_in, b_in, w_dw, b_dw, ln_g, ln_b, w_out, b_out, g_post)


def _qkv_prompt_body(x_ref, g_pre_ref, w_ref, qkv_ref, kv_ref, *scratch, tm, dil, rows, first):
    t = pl.program_id(1)
    d = x_ref.shape[-1]
    h = _rms(x_ref[0], g_pre_ref[...]).astype(BF16)
    for part in range(3):
        r = _dot(h, w_ref[:, part * d:(part + 1) * d])
        if part == 0:
            r = r * (HEAD_DIM ** -0.5 * LOG2_E)
        if part > 0:
            @pl.when(t >= first)
            def _(r=r, part=part):
                kv_ref[0, :, (part - 1) * d:part * d] = r[tm - rows:, :]
        if dil == 1:
            qkv_ref[0, 0, :, part * d:(part + 1) * d] = r.astype(BF16)
        else:
            rbuf, rbuf2 = scratch
            n_slab = d // LANES
            for s in range(n_slab):
                rbuf[s] = r[:, s * LANES:(s + 1) * LANES]
            outer = max(dil // REGROUP_STRIDE, 1)
            inner = dil // outer
            for a in range(inner):
                if outer == 1:
                    pieces = [[rbuf[s, pl.ds(a, tm // dil, stride=dil), :] for s in range(n_slab)]]
                else:
                    for s in range(n_slab):
                        rbuf2[s] = rbuf[s, pl.ds(a, tm // inner, stride=inner), :]
                    pieces = [[rbuf2[s, pl.ds(c, tm // dil, stride=outer), :] for s in range(n_slab)]
                              for c in range(outer)]
                for c, piece in enumerate(pieces):
                    qkv_ref[0, a + inner * c, :, part * d:(part + 1) * d] = jnp.concatenate(piece, axis=1).astype(BF16)


def _qkv_prompt(x, g_pre, w_qkv, group, tm):
    b, t, d = x.shape
    win, dil = SWA_CONFIGS[group]
    keep = min(win, t)
    rows = min(keep, tm)
    assert t % tm == 0 and keep % rows == 0 and tm % (dil * 16) == 0
    first = t // tm - (keep // rows if keep >= tm else 1)
    tile = pl.BlockSpec((1, tm, d), lambda i, j: (i, j, 0))
    w_spec = pl.BlockSpec((d, 3 * d), lambda i, j: (0, group), pipeline_mode=pl.Buffered(1))
    return pl.pallas_call(
        functools.partial(_qkv_prompt_body, tm=tm, dil=dil, rows=rows, first=first),
        grid=(b, t // tm),
        in_specs=[tile, _resident((1, d)), w_spec],
        out_specs=[pl.BlockSpec((1, dil, tm // dil, 3 * d), lambda i, j: (i, 0, j, 0)),
                   pl.BlockSpec((1, rows, 2 * d), lambda i, j: (i, jnp.maximum(j - first, 0), 0))],
        out_shape=[jax.ShapeDtypeStruct((b, dil, t // dil, 3 * d), BF16), jax.ShapeDtypeStruct((b, keep, 2 * d), F32)],
        scratch_shapes=[pltpu.VMEM((d // LANES, tm, LANES), F32),
                        pltpu.VMEM((d // LANES, tm // min(dil, REGROUP_STRIDE), LANES), F32)] if dil > 1 else [],
        compiler_params=_params("parallel", "arbitrary"),
        name=f"qkv_prompt_d{dil}",
    )(x, g_pre, w_qkv)


def _qkv_sample_body(x_ref, g_pre_ref, w_ref, o_ref):
    h = _rms(x_ref[...], g_pre_ref[...]).astype(BF16)
    o_ref[...] = _dot(h, w_ref[...])


def _qkv_sample(x, g_pre, w_qkv):
    bd, _ = x.shape
    return pl.pallas_call(
        _qkv_sample_body,
        out_shape=jax.ShapeDtypeStruct((bd, w_qkv.shape[1]), F32),
        compiler_params=pltpu.CompilerParams(vmem_limit_bytes=VMEM_LIMIT_BYTES),
        name="qkv_sample",
    )(x, g_pre, w_qkv)


def _attn_prompt_body(*refs, dil, units, chained, last, rider_dils):
    refs = list(refs)
    q_ref, kc_ref, vc_ref, kp_ref, vp_ref, bias_ref = refs[:6]
    del refs[:6]
    if chained:
        op_ref, mp_ref, sp_ref = refs[:3]
        del refs[:3]
    rider_in = [refs[RIDER_INPUTS * i:RIDER_INPUTS * (i + 1)] for i in range(len(rider_dils))]
    del refs[:RIDER_INPUTS * len(rider_dils)]
    o_ref = refs[0]
    del refs[:1]
    if not last:
        m_ref, s_ref = refs[:2]
        del refs[:2]
    n = pl.program_id(1)
    res = pl.program_id(2)

    if rider_dils:
        step = (pl.program_id(0) * pl.num_programs(1) + n) * pl.num_programs(2) + res

        @pl.when(step >= 0)
        def _():
            for i, rider_dil in enumerate(rider_dils):
                qh_r, kn_r, vn_r, kcache_r, vcache_r, bias_c_r, bias_n_r = rider_in[i]
                ro_ref, rl_ref = refs[2 * i:2 * i + 2]
                half = step % bias_c_r.shape[0]
                ro_ref[0, 0], rl_ref[0, 0] = _sample_attn_heads(
                    qh_r[0, 0], kn_r[0, 0], vn_r[0, 0], kcache_r.at[0], vcache_r.at[0],
                    bias_c_r[half], bias_n_r[half], rider_dil)

    kj = lax.broadcasted_iota(jnp.int32, (2 * BLK, 2 * BLK), 0)
    qi = lax.broadcasted_iota(jnp.int32, (2 * BLK, 2 * BLK), 1) % BLK
    delta = qi + BLK - kj
    band = (delta >= 0) & (delta <= SPAN)
    first_key = jnp.where(n > 0, 0, BLK)
    band_first = band & (kj >= first_key)
    lane = lax.broadcasted_iota(jnp.int32, (BLK, HEAD_PAIR), 1)
    low = lane < HEAD_DIM
    n_pairs = q_ref.shape[-1] // HEAD_PAIR
    head_row = lax.broadcasted_iota(jnp.int32, (2 * n_pairs, BLK), 0)

    def unit_operands(u):
        if dil > 1:
            cat = lambda prev_ref, cur_ref: jnp.concatenate([prev_ref[0, u], cur_ref[0, u]], axis=0)
            return (q_ref[0, u], cat(kp_ref, kc_ref), cat(vp_ref, vc_ref), band_first,
                    pl.ds(res * units + u, BLK, stride=dil))
        own = slice(u * BLK, (u + 1) * BLK)
        if u == 0:
            cat = lambda prev_ref, cur_ref: jnp.concatenate([prev_ref[0, 0], cur_ref[0, 0, own]], axis=0)
            return q_ref[0, 0, own], cat(kp_ref, kc_ref), cat(vp_ref, vc_ref), band_first, own
        both = slice((u - 1) * BLK, (u + 1) * BLK)
        return q_ref[0, 0, own], kc_ref[0, 0, both], vc_ref[0, 0, both], band, own

    def pair_rows(stat_t):
        return jnp.stack([jnp.concatenate([stat_t[2 * p:2 * p + 1], stat_t[2 * p + 1:2 * p + 2]], axis=1)
                          for p in range(n_pairs)], axis=0)

    ops = [unit_operands(u) for u in range(units)]
    items = [(u, pair) for u in range(units) for pair in range(n_pairs)]
    st = []
    for u, pair in items:
        q, k, _, valid, _ = ops[u]
        sl = slice(pair * HEAD_PAIR, (pair + 1) * HEAD_PAIR)
        q2 = q[:, sl]
        qq = jnp.concatenate([jnp.where(low, q2, jnp.zeros_like(q2)), jnp.where(low, jnp.zeros_like(q2), q2)], axis=0)
        score = lax.dot_general(k[:, sl], qq, (((1,), (1,)), ((), ())), preferred_element_type=F32)
        st.append(jnp.where(valid, score + bias_ref[pair], NEG_INF))
    m_new = [jnp.max(x, axis=0, keepdims=True) for x in st]
    if chained:
        m_prev, s_prev = [], []
        for u in range(units):
            rows = ops[u][4]
            m_prev += list(pair_rows(mp_ref[0, rows, :].T))
            s_prev += list(pair_rows(sp_ref[0, rows, :].T))
        m_new = [jnp.maximum(a, b) for a, b in zip(m_new, m_prev)]
    p = [jnp.exp2(x - m) for x, m in zip(st, m_new)]
    s_new = [jnp.sum(x, axis=0, keepdims=True) for x in p]
    if chained:
        alpha = [jnp.exp2(a - b) for a, b in zip(m_prev, m_new)]
        s_new = [sp * a + sn for sp, a, sn in zip(s_prev, alpha, s_new)]
    p = [x.astype(BF16) for x in p]
    for u, (_, _, v, _, rows) in enumerate(ops):
        m_out_t = jnp.zeros((2 * n_pairs, BLK), F32)
        s_out_t = jnp.zeros((2 * n_pairs, BLK), F32)
        for pair in range(n_pairs):
            i = u * n_pairs + pair
            sl = slice(pair * HEAD_PAIR, (pair + 1) * HEAD_PAIR)
            pv = lax.dot_general(v[:, sl], p[i], (((0,), (0,)), ((), ())), preferred_element_type=F32)
            o_t = [pv[:HEAD_DIM, :BLK], pv[HEAD_DIM:, BLK:]]
            if chained:
                o_prev_t = op_ref[0, pair, rows, :].T
            for half in range(2):
                cols = slice(half * BLK, (half + 1) * BLK)
                if chained:
                    o_t[half] = o_prev_t[half * HEAD_DIM:(half + 1) * HEAD_DIM] * alpha[i][:, cols] + o_t[half]
                if last:
                    o_t[half] = o_t[half] / s_new[i][:, cols]
                else:
                    m_out_t = jnp.where(head_row == 2 * pair + half, m_new[i][:, cols], m_out_t)
                    s_out_t = jnp.where(head_row == 2 * pair + half, s_new[i][:, cols], s_out_t)
            o_ref[0, pair, rows, :] = jnp.concatenate(o_t, axis=0).T
        if not last:
            pad = jnp.zeros((BLK - 2 * n_pairs, BLK), F32)
            m_ref[0, rows, :] = jnp.concatenate([m_out_t, pad], axis=0).T
            s_ref[0, rows, :] = jnp.concatenate([s_out_t, pad], axis=0).T


def _rider_specs(rider, grid):
    bd, hh, dh, n_cache, layer, n_split = (rider[k] for k in ("bd", "hh", "dh", "n_cache", "layer", "n_split"))
    assert grid[0] * grid[1] * grid[2] == bd * n_split

    def item(i, n, r):
        step = (i * grid[1] + n) * grid[2] + r
        return step // n_split, step % n_split

    small = lambda rows, cols: pl.BlockSpec((1, 1, rows, cols), lambda i, n, r: (*item(i, n, r), 0, 0))

    def kv(which):
        def index(i, n, r):
            seq, sub = item(i, n, r)
            return layer, seq, which, sub, 0, 0
        return pl.BlockSpec((None, 1, None, hh, dh, n_cache), index)

    qh, kn, vn, cache_t, bias_c, bias_n = rider["arrays"]
    in_specs = [small(hh, dh), small(hh, dh), small(dh, hh), kv(0), kv(1),
                _resident(bias_c.shape), _resident(bias_n.shape)]
    assert len(in_specs) == RIDER_INPUTS
    out_shape = [jax.ShapeDtypeStruct((bd, n_split, dh, hh), F32), jax.ShapeDtypeStruct((bd, n_split, hh, 1), F32)]
    return in_specs, [qh, kn, vn, cache_t, cache_t, bias_c, bias_n], [small(dh, hh), small(hh, 1)], out_shape


def _attn_prompt(qkv, bias, dil, prev, last, units, riders=()):
    b, _, sub, d3 = qkv.shape
    d = d3 // 3
    t = sub * dil
    n_pairs = d // HEAD_PAIR
    assert sub % BLK == 0 and d % HEAD_PAIR == 0 and d // HEAD_DIM <= LANES
    if dil > 1:
        assert dil % units == 0
        grid = (b, sub // BLK, dil // units)
        span = BLK * dil
        blk = (1, units, BLK, d)
        col = lambda part: pl.BlockSpec(blk, lambda i, n, r: (i, r, n, part))
        col_prev = lambda part: pl.BlockSpec(blk, lambda i, n, r: (i, r, jnp.maximum(n - 1, 0), part))
    else:
        assert (sub // BLK) % units == 0
        grid = (b, sub // (BLK * units), 1)
        span = BLK * units
        col = lambda part: pl.BlockSpec((1, 1, span, d), lambda i, n, r: (i, 0, n, part))
        col_prev = lambda part: pl.BlockSpec((1, 1, BLK, d), lambda i, n, r: (i, 0, jnp.maximum(n * units - 1, 0), part))

    o_spec = pl.BlockSpec((1, n_pairs, span, HEAD_PAIR), lambda i, n, r: (i, 0, n, 0))
    st_spec = pl.BlockSpec((1, span, LANES), lambda i, n, r: (i, n, 0))
    in_specs = [col(0), col(1), col(2), col_prev(1), col_prev(2), _resident(bias.shape)]
    args = [qkv, qkv, qkv, qkv, qkv, bias]
    if prev is not None:
        in_specs += [o_spec, st_spec, st_spec]
        args += list(prev)
    o_shape = jax.ShapeDtypeStruct((b, n_pairs, t, HEAD_PAIR), F32)
    st_shape = jax.ShapeDtypeStruct((b, t, LANES), F32)
    out_specs = [o_spec] if last else [o_spec, st_spec, st_spec]
    out_shape = [o_shape] if last else [o_shape, st_shape, st_shape]
    n_own = len(out_shape)
    for rider in riders:
        r_in, r_args, r_out, r_shape = _rider_specs(rider, grid)
        in_specs += r_in
        args += r_args
        out_specs += r_out
        out_shape += r_shape
    outs = pl.pallas_call(
        functools.partial(_attn_prompt_body, dil=dil, units=units, chained=prev is not None, last=last,
                          rider_dils=tuple(rider["dil"] for rider in riders)),
        grid=grid,
        in_specs=in_specs,
        out_specs=out_specs,
        out_shape=out_shape,
        compiler_params=_params("parallel", "arbitrary", "arbitrary"),
        name=f"attn_prompt_d{dil}",
    )(*args)
    return outs[:n_own], outs[n_own:]


def _bias_table(rel_bias, g, dil):
    onehot = np.eye(N_BUCKETS, dtype=np.float32)[_t5_bucket(np.arange(SPAN + 1) * dil)]
    return jnp.dot(rel_bias[:, g, :].astype(F32).T, jnp.asarray(onehot).T, precision=lax.Precision.HIGHEST)


def _prompt_bias(table):
    nh = table.shape[0]
    width = 3 * BLK - 1
    rev = jnp.concatenate([jnp.broadcast_to(table[:, SPAN:], (nh, BLK - 1)), table[:, ::-1],
                           jnp.broadcast_to(table[:, :1], (nh, BLK - 1)), jnp.zeros((nh, 1), F32)], axis=1)
    skew = jnp.tile(rev, (1, BLK))[:, :BLK * width].reshape(nh, BLK, width)
    bias = skew[:, :, BLK - 1:]
    bias_t = jnp.transpose(bias, (0, 2, 1)).reshape(nh // 2, 2, 2 * BLK, BLK)
    return jnp.transpose(bias_t, (0, 2, 1, 3)).reshape(nh // 2, 2 * BLK, 2 * BLK)


def _sample_attn_heads(q_h, kn, vn_t, kc_ref, vc_ref, bias_c, bias_n, dil):
    hh, n_cache = bias_c.shape
    pos = lax.broadcasted_iota(jnp.int32, (1, n_cache), 1)
    on_grid = pos % dil == 0
    head = lax.broadcasted_iota(jnp.int32, (1, hh), 1)
    q_h = q_h * (HEAD_DIM ** -0.5)
    ln = jnp.sum(kn * q_h, axis=1, keepdims=True) + bias_n
    q_b = q_h.astype(BF16)
    lc = jnp.concatenate([_dot(q_b, kc_ref[h].astype(BF16))[h:h + 1] for h in range(hh)], axis=0)
    lc = jnp.where(on_grid, lc + bias_c, NEG_INF)
    m = jnp.maximum(jnp.max(lc, axis=1, keepdims=True), ln)
    pc = jnp.exp(lc - m)
    pn = jnp.exp(ln - m)
    s = jnp.sum(pc, axis=1, keepdims=True) + pn
    p_b = pc.astype(BF16)
    o_all = jnp.zeros_like(vn_t)
    for h in range(hh):
        o = lax.dot_general(vc_ref[h].astype(BF16), p_b, (((1,), (1,)), ((), ())), preferred_element_type=F32)
        o_all = jnp.where(head == h, (o + pn[h:h + 1, :] * vn_t) / s[h:h + 1, :], o_all)
    return o_all, m + jnp.log(s)


def _sample_rider(q, kn, vn, cache, layer, table, dil, n_split):
    bd, nh, dh = q.shape
    n_cache = cache.shape[2]
    assert n_cache == SPAN * dil and nh % n_split == 0
    hh = nh // n_split
    split = lambda a: a.reshape(bd, n_split, hh, dh)
    bias_c = jnp.repeat(table[:, :0:-1], dil, axis=1).reshape(n_split, hh, n_cache)
    bias_n = table[:, :1].reshape(n_split, hh, 1)
    return dict(arrays=(split(q), split(kn), jnp.swapaxes(split(vn), 2, 3),
                        jnp.transpose(cache, (0, 1, 3, 4, 5, 2)), bias_c, bias_n),
                layer=layer, dil=dil, bd=bd, hh=hh, dh=dh, n_cache=n_cache, n_split=n_split)


def _merge_sample_body(o0_ref, o1_ref, o2_ref, l0_ref, l1_ref, l2_ref, out_ref):
    lses = (l0_ref[...], l1_ref[...], l2_ref[...])
    outs = (o0_ref[...], o1_ref[...], o2_ref[...])
    mx = jnp.maximum(jnp.maximum(lses[0], lses[1]), lses[2])
    es = [jnp.exp(l - mx) for l in lses]
    tot = es[0] + es[1] + es[2]
    out_ref[...] = sum((e / tot) * o for e, o in zip(es, outs))


def _merge_sample(outs, lses):
    return pl.pallas_call(
        _merge_sample_body,
        out_shape=jax.ShapeDtypeStruct(outs[0].shape, F32),
        name="merge_sample",
    )(*outs, *lses)


def _wo_prompt_body(x_ref, o_ref_in, w_ref, g_post_ref, f0, f1, f2, f3, out_ref):
    o = jnp.concatenate([o_ref_in[0, p] for p in range(o_ref_in.shape[1])], axis=1)
    y = _dot(o.astype(BF16), w_ref[...])
    out_ref[0] = _ffn_tile(x_ref[0] + _rms(y, g_post_ref[...]), f0, f1, f2, f3)


def _wo_prompt(x, o, w_o, g_post, ffn, tm):
    b, t, d = x.shape
    row = pl.BlockSpec((1, tm, d), lambda i, j: (i, j, 0))
    return pl.pallas_call(
        _wo_prompt_body,
        grid=(b, t // tm),
        in_specs=[row, pl.BlockSpec((1, o.shape[1], tm, HEAD_PAIR), lambda i, j: (i, 0, j, 0)),
                  _resident(w_o.shape), _resident((1, d))] + _ffn_specs(ffn),
        out_specs=row,
        out_shape=jax.ShapeDtypeStruct((b, t, d), F32),
        compiler_params=_params("parallel", "parallel"),
        name="wo_ffn_prompt",
    )(x, o, w_o, g_post, *ffn[:4])


def _wo_sample_body(x_ref, o_ref_in, w_ref, g_post_ref, out_ref):
    y = _dot(o_ref_in[...].astype(BF16), w_ref[...])
    out_ref[...] = x_ref[...] + _rms(y, g_post_ref[...])


def _wo_sample(x, o, w_o, g_post):
    return pl.pallas_call(
        _wo_sample_body,
        out_shape=jax.ShapeDtypeStruct(x.shape, F32),
        compiler_params=pltpu.CompilerParams(vmem_limit_bytes=VMEM_LIMIT_BYTES),
        name="wo_sample",
    )(x, o, w_o, g_post)


def kernel(x_prompt, x_sample, state_pool, cache_swa_g0, cache_swa_g1, cache_swa_g2, state_conv,
           norm_g, w_ffn_in, w_ffn_out, pool_w, pool_scale, w_qkv, w_o, rel_bias,
           conv_w_in, conv_b_in, conv_w_dw, conv_b_dw, conv_ln_g, conv_ln_b, conv_w_out, conv_b_out):
    b, t, d = x_prompt.shape
    bd, t_dec, _ = x_sample.shape
    assert t_dec == 1 and d % HEAD_PAIR == 0
    nh = d // HEAD_DIM
    depth = norm_g.shape[0]
    caches = (cache_swa_g0, cache_swa_g1, cache_swa_g2)

    tm = 512 if (b * t) % 512 == 0 else b * t
    ts = min(512, t)
    tq = min(1024, t)

    xp = x_prompt
    xs = x_sample.reshape(bd, d)
    vec = lambda a: a.reshape(1, -1).astype(F32)

    w_ffn_in_bf16 = w_ffn_in.astype(BF16)
    w_ffn_out_bf16 = w_ffn_out.astype(BF16)
    pool_p, pool_s, conv_p, conv_s = [], [], [], []
    swa_p = [[] for _ in range(N_GROUPS)]
    swa_s = [[] for _ in range(N_GROUPS)]
    for i in range(depth):
        kind, j = i % 3, i // 3
        g = [vec(norm_g[i, k]) for k in range(4)]
        ffn = (g[2], w_ffn_in_bf16, w_ffn_out_bf16, g[3], i)
        if kind == 0:
            pw = pool_w[j].astype(BF16)
            ps = vec(pool_scale[j])
            xp, tail = _pool_prompt(xp, jnp.zeros((b, POOL_HALO, d), F32), g[0], pw, ps, g[1], ffn, ts, 0)
            pool_p.append(tail[:, POOL_HALO - POOL_HIST:])
            hist = state_pool[j]
            xs, hs = _pool_sample(xs, jnp.swapaxes(hist, 0, 1), g[0], pw, ps, g[1], PAST_LEN)
            pool_s.append(jnp.concatenate([hist[:, 1:], hs[:, None, :]], axis=1))
        elif kind == 1:
            wq = w_qkv[j].astype(BF16)
            wo = w_o[j].astype(BF16)
            outs = [_qkv_prompt(xp, g[0], wq, gi, tq) for gi in range(N_GROUPS)]
            tables = [_bias_table(rel_bias, gi, dil) for gi, (_, dil) in enumerate(SWA_CONFIGS)]
            qkv_s = _qkv_sample(xs, g[0], wq).reshape(bd, N_GROUPS, 3, nh, HEAD_DIM)
            order = sorted(range(N_GROUPS), key=lambda gi: -SWA_CONFIGS[gi][1])
            prev = None
            o_s, lse_s = [None] * N_GROUPS, [None] * N_GROUPS
            attn_steps = b * (t // BLK) // ATTN_UNITS
            assert attn_steps % bd == 0
            by_size = sorted(range(N_GROUPS), key=lambda sg: SWA_CONFIGS[sg][1])
            riding = {order[0]: by_size[:-1], order[-1]: by_size[-1:]}
            for k, gi in enumerate(order):
                riders = [_sample_rider(qkv_s[:, sg, 0], qkv_s[:, sg, 1], qkv_s[:, sg, 2], caches[sg], j, tables[sg],
                                        SWA_CONFIGS[sg][1], attn_steps // bd) for sg in riding.get(gi, ())]
                prev, rider_outs = _attn_prompt(outs[gi][0], _prompt_bias(tables[gi] * LOG2_E), SWA_CONFIGS[gi][1],
                                                prev, k == N_GROUPS - 1, ATTN_UNITS, riders)
                for i, sg in enumerate(riding.get(gi, ())):
                    o_s[sg] = rider_outs[2 * i]
                    lse_s[sg] = jnp.swapaxes(rider_outs[2 * i + 1], 2, 3)
            for gi in range(N_GROUPS):
                swa_p[gi].append(outs[gi][1].reshape(b, -1, 2, nh, HEAD_DIM))
                swa_s[gi].append(qkv_s[:, gi, 1:3].reshape(bd, 1, 2, nh, HEAD_DIM))
            xp = _wo_prompt(xp, prev[0], wo, g[1], ffn, ts)
            merged = jnp.swapaxes(_merge_sample(o_s, lse_s), 2, 3).reshape(bd, d)
            xs = _wo_sample(xs, merged, wo, g[1])
        else:
            cw = (conv_w_in[j].astype(BF16), vec(conv_b_in[j]),
                  jnp.pad(conv_w_dw[j], ((0, CONV_HALO - CONV_WIDTH), (0, 0))), vec(conv_b_dw[j]),
                  vec(conv_ln_g[j]), vec(conv_ln_b[j]), conv_w_out[j].astype(BF16), vec(conv_b_out[j]))
            xp, tail = _conv_prompt(xp, jnp.zeros((b, CONV_HALO, d), F32), g[0], *cw, g[1], ts, 128)
            conv_p.append(tail[:, CONV_HALO - CONV_HIST:])
            hist = state_conv[j]
            xs, us = _conv_sample(xs, jnp.swapaxes(hist, 0, 1), g[0], *cw, g[1])
            conv_s.append(jnp.concatenate([hist[:, 1:], us[:, None, :]], axis=1))
            xp = _ffn(xp.reshape(b * t, d), ffn, tm).reshape(b, t, d)
        xs = _ffn(xs, ffn, bd)

    return (xp, xs.reshape(bd, 1, d),
            jnp.stack(pool_p), jnp.stack(pool_s),
            jnp.stack(swa_p[0]), jnp.stack(swa_p[1]), jnp.stack(swa_p[2]),
            jnp.stack(swa_s[0]), jnp.stack(swa_s[1]), jnp.stack(swa_s[2]),
            jnp.stack(conv_p), jnp.stack(conv_s))
```

```python
import functools
import math

import numpy as np
import jax
import jax.numpy as jnp
from jax import lax
from jax.experimental import pallas as pl
from jax.experimental.pallas import tpu as pltpu

F32 = jnp.float32
BF16 = jnp.bfloat16

RMS_EPS = 1e-6
LN_EPS = 1e-5
NEG_INF = -1e30
PAST_LEN = 8192
POOL_WINDOWS = (2, 4, 8, 16)
POOL_HIST = max(POOL_WINDOWS) - 1
SWA_CONFIGS = ((128, 1), (512, 4), (2048, 16))
N_GROUPS = len(SWA_CONFIGS)
SPAN = 128
BLK = 128
HEAD_DIM = 64
N_BUCKETS = 32
MAX_DISTANCE = 2048
CONV_WIDTH = 31
CONV_HIST = CONV_WIDTH - 1

LANES = 128
SUBLANES = 8
MXU_DIM = 256
VMEM_LIMIT_BYTES = 56 * 1024 * 1024
REGROUP_STRIDE = 4


POOL_HALO = 16
CONV_HALO = 32
HEAD_PAIR = 2 * HEAD_DIM
LOG2_E = math.log2(math.e)
BF16_SUBLANES = 16
ROW_TILE = 512
QKV_ROW_TILE = 1024
CONV_ROW_CHUNK = 128
ATTN_UNITS = 2
ATTN_UNITS_FIRST = 4
RIDER_INPUTS = 7


def _params(*sem):
    return pltpu.CompilerParams(dimension_semantics=sem, vmem_limit_bytes=VMEM_LIMIT_BYTES)


def _resident(shape):
    nd = len(shape)
    return pl.BlockSpec(shape, lambda *_: (0,) * nd, pipeline_mode=pl.Buffered(1))


def _rms(x, g):
    return x * lax.rsqrt(jnp.mean(x * x, axis=-1, keepdims=True) + RMS_EPS) * g


def _dot(a, b):
    return jnp.dot(a, b, preferred_element_type=F32)


def _t5_bucket(dist):
    max_exact = N_BUCKETS // 2
    d = np.maximum(np.asarray(dist), 0)
    large = max_exact + (np.log(np.maximum(d, max_exact) / max_exact) / math.log(MAX_DISTANCE / max_exact)
                         * (N_BUCKETS - max_exact)).astype(np.int64)
    large = np.minimum(large, N_BUCKETS - 1)
    return np.where(d < max_exact, d, large).astype(np.int32)


def _ff_chunks(d_ff):
    assert d_ff % MXU_DIM == 0
    n = d_ff // MXU_DIM
    first = (n + 1) // 2 * MXU_DIM
    return ((0, first), (first, d_ff - first)) if d_ff > first else ((0, d_ff),)


def _ffn_tile(x, g_pre_ref, w_in_ref, w_out_ref, g_post_ref):
    d_ff = w_out_ref.shape[0]
    h = _rms(x, g_pre_ref[...]).astype(BF16)
    acc = None
    for c0, cw in _ff_chunks(d_ff):
        gate = _dot(h, w_in_ref[:, c0:c0 + cw])
        up = _dot(h, w_in_ref[:, d_ff + c0:d_ff + c0 + cw])
        act = (gate * jax.nn.sigmoid(gate) * up).astype(BF16)
        part = _dot(act, w_out_ref[c0:c0 + cw, :])
        acc = part if acc is None else acc + part
    return x + _rms(acc, g_post_ref[...])


def _layer_resident(shape, layer):
    nd = len(shape)
    return pl.BlockSpec((None,) + tuple(shape[1:]), lambda *_: (layer,) + (0,) * (nd - 1), pipeline_mode=pl.Buffered(1))


def _ffn_specs(ffn):
    g_pre, w_in, w_out, g_post, layer = ffn
    return [_resident(g_pre.shape), _layer_resident(w_in.shape, layer), _layer_resident(w_out.shape, layer),
            _resident(g_post.shape)]


def _ffn_body(x_ref, g_pre_ref, w_in_ref, w_out_ref, g_post_ref, o_ref):
    o_ref[...] = _ffn_tile(x_ref[...], g_pre_ref, w_in_ref, w_out_ref, g_post_ref)


def _ffn(x, ffn, tm):
    n, d = x.shape
    assert n % tm == 0
    row = pl.BlockSpec((tm, d), lambda i: (i, 0))
    return pl.pallas_call(
        _ffn_body,
        grid=(n // tm,),
        in_specs=[row] + _ffn_specs(ffn),
        out_specs=row,
        out_shape=jax.ShapeDtypeStruct((n, d), F32),
        compiler_params=_params("parallel"),
        name="ffn",
    )(x, *ffn[:4])


def _pool_groups(h, window_sum, cnt_of, pw_ref, ps_ref):
    gd = h.shape[-1] // len(POOL_WINDOWS)
    outs = []
    for gi, w in enumerate(POOL_WINDOWS):
        c0 = gi * gd
        hg = h[:, c0:c0 + gd]
        dg = window_sum(gi, w, c0, gd, hg) / cnt_of(w) - hg
        outs.append(_dot(dg.astype(BF16), pw_ref[gi]))
    return jnp.concatenate(outs, axis=1) * ps_ref[...]


def _pool_prompt_body(x_ref, xprev_ref, hist_ref, g_pre_ref, pw_ref, ps_ref, g_post_ref, f0, f1, f2, f3,
                      o_ref, tail_ref, buf, *, ts, n_past):
    t = pl.program_id(1)
    x = x_ref[0]
    g_pre = g_pre_ref[...]
    h = _rms(x, g_pre)
    halo = jnp.where(t == 0, hist_ref[0], _rms(xprev_ref[0], g_pre))
    buf[0:POOL_HALO, :] = halo
    buf[POOL_HALO:POOL_HALO + ts, :] = h
    tail_ref[0] = buf[ts:ts + POOL_HALO, :]
    pos1 = n_past + 1 + t * ts + lax.broadcasted_iota(jnp.int32, (ts, 1), 0)

    def window_sum(gi, w, c0, gd, hg):
        assert w & (w - 1) == 0 and w <= POOL_HALO
        acc = buf[:, c0:c0 + gd]
        step = 1
        while step < w:
            acc = acc + pltpu.roll(acc, step, axis=0)
            step *= 2
        return acc[POOL_HALO:, :]

    y = _pool_groups(h, window_sum, lambda w: jnp.minimum(pos1, w).astype(F32), pw_ref, ps_ref)
    o_ref[0] = _ffn_tile(x + _rms(y, g_post_ref[...]), f0, f1, f2, f3)


def _pool_prompt(x, hist16, g_pre, pw, ps, g_post, ffn, ts, n_past):
    b, t, d = x.shape
    assert t % ts == 0 and ts % POOL_HALO == 0
    per = ts // POOL_HALO
    tile = pl.BlockSpec((1, ts, d), lambda i, j: (i, j, 0))
    halo = pl.BlockSpec((1, POOL_HALO, d), lambda i, j: (i, jnp.maximum(j * per - 1, 0), 0))
    first = pl.BlockSpec((1, POOL_HALO, d), lambda i, j: (i, 0, 0))
    return pl.pallas_call(
        functools.partial(_pool_prompt_body, ts=ts, n_past=n_past),
        grid=(b, t // ts),
        in_specs=[tile, halo, first, _resident((1, d)), _resident(pw.shape), _resident((1, d)), _resident((1, d))]
        + _ffn_specs(ffn),
        out_specs=[tile, first],
        out_shape=[jax.ShapeDtypeStruct((b, t, d), F32), jax.ShapeDtypeStruct((b, POOL_HALO, d), F32)],
        scratch_shapes=[pltpu.VMEM((POOL_HALO + ts, d), F32)],
        compiler_params=_params("parallel", "arbitrary"),
        name="pool_ffn_prompt",
    )(x, x, hist16, g_pre, pw, ps, g_post, *ffn[:4])


def _pool_sample_body(x_ref, hist_ref, g_pre_ref, pw_ref, ps_ref, g_post_ref, o_ref, h_ref, *, n_past):
    x = x_ref[...]
    h = _rms(x, g_pre_ref[...])
    h_ref[...] = h

    def window_sum(gi, w, c0, gd, hg):
        win = hg
        for k in range(1, w):
            win = win + hist_ref[POOL_HIST - k, :, c0:c0 + gd]
        return win

    y = _pool_groups(h, window_sum, lambda w: float(min(n_past + 1, w)), pw_ref, ps_ref)
    o_ref[...] = x + _rms(y, g_post_ref[...])


def _pool_sample(x, hist_t, g_pre, pw, ps, g_post, n_past):
    bd, d = x.shape
    return pl.pallas_call(
        functools.partial(_pool_sample_body, n_past=n_past),
        out_shape=[jax.ShapeDtypeStruct((bd, d), F32), jax.ShapeDtypeStruct((bd, d), F32)],
        compiler_params=pltpu.CompilerParams(vmem_limit_bytes=VMEM_LIMIT_BYTES),
        name="pool_sample",
    )(x, hist_t, g_pre, pw, ps, g_post)


def _glu(h, w_in_ref, b_in_ref):
    d = h.shape[-1]
    a = _dot(h, w_in_ref[...]) + b_in_ref[...]
    return a[:, :d] * jax.nn.sigmoid(a[:, d:])


def _conv_tail(c, x, ln_g_ref, ln_b_ref, w_out_ref, b_out_ref, g_post_ref):
    mu = jnp.mean(c, axis=-1, keepdims=True)
    cc = c - mu
    var = jnp.mean(cc * cc, axis=-1, keepdims=True)
    z = cc * lax.rsqrt(var + LN_EPS) * ln_g_ref[...] + ln_b_ref[...]
    y = _dot((z * jax.nn.sigmoid(z)).astype(BF16), w_out_ref[...]) + b_out_ref[...]
    return x + _rms(y, g_post_ref[...])


def _conv_prompt_body(x_ref, hist_ref, g_pre_ref, w_in_ref, b_in_ref, w_dw_ref, b_dw_ref, ln_g_ref, ln_b_ref,
                      w_out_ref, b_out_ref, g_post_ref, o_ref, tail_ref, ubuf, cbuf, *, ts, rc):
    t = pl.program_id(1)
    d = x_ref.shape[-1]

    @pl.when(t == 0)
    def _():
        ubuf[0:CONV_HALO, :] = hist_ref[0]

    @pl.when(t > 0)
    def _():
        ubuf[0:CONV_HALO, :] = ubuf[ts:ts + CONV_HALO, :]

    x = x_ref[0]
    h = _rms(x, g_pre_ref[...]).astype(BF16)
    ubuf[CONV_HALO:CONV_HALO + ts, :] = _glu(h, w_in_ref, b_in_ref)
    tail_ref[0] = ubuf[ts:ts + CONV_HALO, :]

    lead = CONV_HALO - CONV_HIST

    def chunk(i, carry):
        base = pl.multiple_of(i * rc, rc)
        for l0 in range(0, d, LANES):
            win = ubuf[pl.ds(base, rc + CONV_HALO), l0:l0 + LANES]
            acc = jnp.broadcast_to(b_dw_ref[:, l0:l0 + LANES], (rc, LANES))
            for phase in range(SUBLANES):
                offs = [o for o in range(lead, lead + CONV_WIDTH) if o % SUBLANES == phase]
                shifted = win if phase == 0 else pltpu.roll(win, rc + CONV_HALO - phase, axis=0)
                for o in offs:
                    acc = acc + shifted[o - phase:o - phase + rc, :] * w_dw_ref[o - lead:o - lead + 1, l0:l0 + LANES]
            cbuf[pl.ds(base, rc), l0:l0 + LANES] = acc
        return carry

    lax.fori_loop(0, ts // rc, chunk, 0)
    o_ref[0] = _conv_tail(cbuf[...], x, ln_g_ref, ln_b_ref, w_out_ref, b_out_ref, g_post_ref)


def _conv_prompt(x, hist32, g_pre, w_in, b_in, w_dw, b_dw, ln_g, ln_b, w_out, b_out, g_post, ts, rc):
    b, t, d = x.shape
    assert t % ts == 0 and ts % rc == 0 and ts >= CONV_HALO
    tile = pl.BlockSpec((1, ts, d), lambda i, j: (i, j, 0))
    first = pl.BlockSpec((1, CONV_HALO, d), lambda i, j: (i, 0, 0))
    vec = _resident((1, d))
    return pl.pallas_call(
        functools.partial(_conv_prompt_body, ts=ts, rc=rc),
        grid=(b, t // ts),
        in_specs=[tile, first, vec, _resident(w_in.shape), _resident(b_in.shape), _resident(w_dw.shape), vec, vec,
                  vec, _resident(w_out.shape), vec, vec],
        out_specs=[tile, first],
        out_shape=[jax.ShapeDtypeStruct((b, t, d), F32), jax.ShapeDtypeStruct((b, CONV_HALO, d), F32)],
        scratch_shapes=[pltpu.VMEM((CONV_HALO + ts, d), F32), pltpu.VMEM((ts, d), F32)],
        compiler_params=_params("parallel", "arbitrary"),
        name="conv_prompt",
    )(x, hist32, g_pre, w_in, b_in, w_dw, b_dw, ln_g, ln_b, w_out, b_out, g_post)


def _conv_sample_body(x_ref, hist_ref, g_pre_ref, w_in_ref, b_in_ref, w_dw_ref, b_dw_ref, ln_g_ref, ln_b_ref,
                      w_out_ref, b_out_ref, g_post_ref, o_ref, u_ref):
    x = x_ref[...]
    h = _rms(x, g_pre_ref[...]).astype(BF16)
    u = _glu(h, w_in_ref, b_in_ref)
    u_ref[...] = u
    c = b_dw_ref[...] + u * w_dw_ref[CONV_HIST:CONV_WIDTH, :]
    for k in range(CONV_HIST):
        c = c + hist_ref[k] * w_dw_ref[k:k + 1, :]
    o_ref[...] = _conv_tail(c, x, ln_g_ref, ln_b_ref, w_out_ref, b_out_ref, g_post_ref)


def _conv_sample(x, hist_t, g_pre, w_in, b_in, w_dw, b_dw, ln_g, ln_b, w_out, b_out, g_post):
    bd, d = x.shape
    return pl.pallas_call(
        _conv_sample_body,
        out_shape=[jax.ShapeDtypeStruct((bd, d), F32), jax.ShapeDtypeStruct((bd, d), F32)],
        compiler_params=pltpu.CompilerParams(vmem_limit_bytes=VMEM_LIMIT_BYTES),
        name="conv_sample",
    )(x, hist_t, g_pre, w_in, b_in, w_dw, b_dw, ln_g, ln_b, w_out, b_out, g_post)


def _qkv_prompt_body(x_ref, g_pre_ref, w_ref, qkv_ref, kv_ref, *scratch, tm, dil, rows, first):
    t = pl.program_id(1)
    d = x_ref.shape[-1]
    h = _rms(x_ref[0], g_pre_ref[...]).astype(BF16)
    for part in range(3):
        r = _dot(h, w_ref[:, part * d:(part + 1) * d])
        if part == 0:
            r = r * (HEAD_DIM ** -0.5 * LOG2_E)
        if part > 0:
            @pl.when(t >= first)
            def _(r=r, part=part):
                kv_ref[0, :, (part - 1) * d:part * d] = r[tm - rows:, :]
        if dil == 1:
            qkv_ref[0, 0, :, part * d:(part + 1) * d] = r.astype(BF16)
        else:
            rbuf, rbuf2 = scratch
            n_slab = d // LANES
            for s in range(n_slab):
                rbuf[s] = r[:, s * LANES:(s + 1) * LANES]
            outer = max(dil // REGROUP_STRIDE, 1)
            inner = dil // outer
            for a in range(inner):
                if outer == 1:
                    pieces = [[rbuf[s, pl.ds(a, tm // dil, stride=dil), :] for s in range(n_slab)]]
                else:
                    for s in range(n_slab):
                        rbuf2[s] = rbuf[s, pl.ds(a, tm // inner, stride=inner), :]
                    pieces = [[rbuf2[s, pl.ds(c, tm // dil, stride=outer), :] for s in range(n_slab)]
                              for c in range(outer)]
                for c, piece in enumerate(pieces):
                    qkv_ref[0, a + inner * c, :, part * d:(part + 1) * d] = jnp.concatenate(piece, axis=1).astype(BF16)


def _qkv_prompt(x, g_pre, w_qkv, group, tm):
    b, t, d = x.shape
    win, dil = SWA_CONFIGS[group]
    keep = min(win, t)
    rows = min(keep, tm)
    assert t % tm == 0 and keep % rows == 0 and tm % (dil * BF16_SUBLANES) == 0
    first = t // tm - (keep // rows if keep >= tm else 1)
    tile = pl.BlockSpec((1, tm, d), lambda i, j: (i, j, 0))
    w_spec = pl.BlockSpec((d, 3 * d), lambda i, j: (0, group), pipeline_mode=pl.Buffered(1))
    return pl.pallas_call(
        functools.partial(_qkv_prompt_body, tm=tm, dil=dil, rows=rows, first=first),
        grid=(b, t // tm),
        in_specs=[tile, _resident((1, d)), w_spec],
        out_specs=[pl.BlockSpec((1, dil, tm // dil, 3 * d), lambda i, j: (i, 0, j, 0)),
                   pl.BlockSpec((1, rows, 2 * d), lambda i, j: (i, jnp.maximum(j - first, 0), 0))],
        out_shape=[jax.ShapeDtypeStruct((b, dil, t // dil, 3 * d), BF16), jax.ShapeDtypeStruct((b, keep, 2 * d), F32)],
        scratch_shapes=[pltpu.VMEM((d // LANES, tm, LANES), F32),
                        pltpu.VMEM((d // LANES, tm // min(dil, REGROUP_STRIDE), LANES), F32)] if dil > 1 else [],
        compiler_params=_params("parallel", "arbitrary"),
        name=f"qkv_prompt_d{dil}",
    )(x, g_pre, w_qkv)


def _qkv_sample_body(x_ref, g_pre_ref, w_ref, o_ref):
    h = _rms(x_ref[...], g_pre_ref[...]).astype(BF16)
    o_ref[...] = _dot(h, w_ref[...])


def _qkv_sample(x, g_pre, w_qkv):
    bd, _ = x.shape
    return pl.pallas_call(
        _qkv_sample_body,
        out_shape=jax.ShapeDtypeStruct((bd, w_qkv.shape[1]), F32),
        compiler_params=pltpu.CompilerParams(vmem_limit_bytes=VMEM_LIMIT_BYTES),
        name="qkv_sample",
    )(x, g_pre, w_qkv)


def _attn_prompt_body(*refs, dil, units, chained, last, rider_dils):
    refs = list(refs)
    q_ref, kc_ref, vc_ref, kp_ref, vp_ref, bias_ref = refs[:6]
    del refs[:6]
    if chained:
        op_ref, mp_ref, sp_ref = refs[:3]
        del refs[:3]
    rider_in = [refs[RIDER_INPUTS * i:RIDER_INPUTS * (i + 1)] for i in range(len(rider_dils))]
    del refs[:RIDER_INPUTS * len(rider_dils)]
    o_ref = refs[0]
    del refs[:1]
    if not last:
        m_ref, s_ref = refs[:2]
        del refs[:2]
    n = pl.program_id(1)
    res = pl.program_id(2)

    if rider_dils:
        step = (pl.program_id(0) * pl.num_programs(1) + n) * pl.num_programs(2) + res

        @pl.when(step >= 0)
        def _():
            for i, rider_dil in enumerate(rider_dils):
                qh_r, kn_r, vn_r, kcache_r, vcache_r, bias_c_r, bias_n_r = rider_in[i]
                ro_ref, rl_ref = refs[2 * i:2 * i + 2]
                half = step % bias_c_r.shape[0]
                ro_ref[0, 0], rl_ref[0, 0] = _sample_attn_heads(
                    qh_r[0, 0], kn_r[0, 0], vn_r[0, 0], kcache_r.at[0], vcache_r.at[0],
                    bias_c_r[half], bias_n_r[half], rider_dil)

    kj = lax.broadcasted_iota(jnp.int32, (2 * BLK, 2 * BLK), 0)
    qi = lax.broadcasted_iota(jnp.int32, (2 * BLK, 2 * BLK), 1) % BLK
    delta = qi + BLK - kj
    band = (delta >= 0) & (delta <= SPAN)
    first_key = jnp.where(n > 0, 0, BLK)
    band_first = band & (kj >= first_key)
    lane = lax.broadcasted_iota(jnp.int32, (BLK, HEAD_PAIR), 1)
    low = lane < HEAD_DIM
    n_pairs = q_ref.shape[-1] // HEAD_PAIR
    head_row = lax.broadcasted_iota(jnp.int32, (2 * n_pairs, BLK), 0)

    def unit_operands(u):
        if dil > 1:
            cat = lambda prev_ref, cur_ref: jnp.concatenate([prev_ref[0, u], cur_ref[0, u]], axis=0)
            return (q_ref[0, u], cat(kp_ref, kc_ref), cat(vp_ref, vc_ref), band_first,
                    pl.ds(res * units + u, BLK, stride=dil))
        own = slice(u * BLK, (u + 1) * BLK)
        if u == 0:
            cat = lambda prev_ref, cur_ref: jnp.concatenate([prev_ref[0, 0], cur_ref[0, 0, own]], axis=0)
            return q_ref[0, 0, own], cat(kp_ref, kc_ref), cat(vp_ref, vc_ref), band_first, own
        both = slice((u - 1) * BLK, (u + 1) * BLK)
        return q_ref[0, 0, own], kc_ref[0, 0, both], vc_ref[0, 0, both], band, own

    def pair_rows(stat_t):
        return jnp.stack([jnp.concatenate([stat_t[2 * p:2 * p + 1], stat_t[2 * p + 1:2 * p + 2]], axis=1)
                          for p in range(n_pairs)], axis=0)

    ops = [unit_operands(u) for u in range(units)]
    items = [(u, pair) for u in range(units) for pair in range(n_pairs)]
    st = []
    for u, pair in items:
        q, k, _, valid, _ = ops[u]
        sl = slice(pair * HEAD_PAIR, (pair + 1) * HEAD_PAIR)
        q2 = q[:, sl]
        qq = jnp.concatenate([jnp.where(low, q2, jnp.zeros_like(q2)), jnp.where(low, jnp.zeros_like(q2), q2)], axis=0)
        score = lax.dot_general(k[:, sl], qq, (((1,), (1,)), ((), ())), preferred_element_type=F32)
        st.append(jnp.where(valid, score + bias_ref[pair], NEG_INF))
    m_new = [jnp.max(x, axis=0, keepdims=True) for x in st]
    if chained:
        m_prev, s_prev = [], []
        for u in range(units):
            rows = ops[u][4]
            m_prev += list(pair_rows(mp_ref[0, rows, :].T))
            s_prev += list(pair_rows(sp_ref[0, rows, :].T))
        m_new = [jnp.maximum(a, b) for a, b in zip(m_new, m_prev)]
    p = [jnp.exp2(x - m) for x, m in zip(st, m_new)]
    s_new = [jnp.sum(x, axis=0, keepdims=True) for x in p]
    if chained:
        alpha = [jnp.exp2(a - b) for a, b in zip(m_prev, m_new)]
        s_new = [sp * a + sn for sp, a, sn in zip(s_prev, alpha, s_new)]
    p = [x.astype(BF16) for x in p]
    for u, (_, _, v, _, rows) in enumerate(ops):
        m_out_t = jnp.zeros((2 * n_pairs, BLK), F32)
        s_out_t = jnp.zeros((2 * n_pairs, BLK), F32)
        for pair in range(n_pairs):
            i = u * n_pairs + pair
            sl = slice(pair * HEAD_PAIR, (pair + 1) * HEAD_PAIR)
            pv = lax.dot_general(v[:, sl], p[i], (((0,), (0,)), ((), ())), preferred_element_type=F32)
            o_t = [pv[:HEAD_DIM, :BLK], pv[HEAD_DIM:, BLK:]]
            if chained:
                o_prev_t = op_ref[0, pair, rows, :].T
            for half in range(2):
                cols = slice(half * BLK, (half + 1) * BLK)
                if chained:
                    o_t[half] = o_prev_t[half * HEAD_DIM:(half + 1) * HEAD_DIM] * alpha[i][:, cols] + o_t[half]
                if last:
                    o_t[half] = o_t[half] / s_new[i][:, cols]
                else:
                    m_out_t = jnp.where(head_row == 2 * pair + half, m_new[i][:, cols], m_out_t)
                    s_out_t = jnp.where(head_row == 2 * pair + half, s_new[i][:, cols], s_out_t)
            o_ref[0, pair, rows, :] = jnp.concatenate(o_t, axis=0).T
        if not last:
            pad = jnp.zeros((BLK - 2 * n_pairs, BLK), F32)
            m_ref[0, rows, :] = jnp.concatenate([m_out_t, pad], axis=0).T
            s_ref[0, rows, :] = jnp.concatenate([s_out_t, pad], axis=0).T


def _rider_specs(rider, grid):
    bd, hh, dh, n_cache, layer, n_split = (rider[k] for k in ("bd", "hh", "dh", "n_cache", "layer", "n_split"))
    assert grid[0] * grid[1] * grid[2] == bd * n_split

    def item(i, n, r):
        step = (i * grid[1] + n) * grid[2] + r
        return step // n_split, step % n_split

    small = lambda rows, cols: pl.BlockSpec((1, 1, rows, cols), lambda i, n, r: (*item(i, n, r), 0, 0))

    def kv(which):
        def index(i, n, r):
            seq, sub = item(i, n, r)
            return layer, seq, which, sub, 0, 0
        return pl.BlockSpec((None, 1, None, hh, dh, n_cache), index)

    qh, kn, vn, cache_t, bias_c, bias_n = rider["arrays"]
    in_specs = [small(hh, dh), small(hh, dh), small(dh, hh), kv(0), kv(1),
                _resident(bias_c.shape), _resident(bias_n.shape)]
    assert len(in_specs) == RIDER_INPUTS
    out_shape = [jax.ShapeDtypeStruct((bd, n_split, dh, hh), F32), jax.ShapeDtypeStruct((bd, n_split, hh, 1), F32)]
    return in_specs, [qh, kn, vn, cache_t, cache_t, bias_c, bias_n], [small(dh, hh), small(hh, 1)], out_shape


def _attn_prompt(qkv, bias, dil, prev, last, units, riders=()):
    b, _, sub, d3 = qkv.shape
    d = d3 // 3
    t = sub * dil
    n_pairs = d // HEAD_PAIR
    assert sub % BLK == 0 and d % HEAD_PAIR == 0 and d // HEAD_DIM <= LANES
    if dil > 1:
        assert dil % units == 0
        grid = (b, sub // BLK, dil // units)
        span = BLK * dil
        blk = (1, units, BLK, d)
        col = lambda part: pl.BlockSpec(blk, lambda i, n, r: (i, r, n, part))
        col_prev = lambda part: pl.BlockSpec(blk, lambda i, n, r: (i, r, jnp.maximum(n - 1, 0), part))
    else:
        assert (sub // BLK) % units == 0
        grid = (b, sub // (BLK * units), 1)
        span = BLK * units
        col = lambda part: pl.BlockSpec((1, 1, span, d), lambda i, n, r: (i, 0, n, part))
        col_prev = lambda part: pl.BlockSpec((1, 1, BLK, d), lambda i, n, r: (i, 0, jnp.maximum(n * units - 1, 0), part))

    o_spec = pl.BlockSpec((1, n_pairs, span, HEAD_PAIR), lambda i, n, r: (i, 0, n, 0))
    st_spec = pl.BlockSpec((1, span, LANES), lambda i, n, r: (i, n, 0))
    in_specs = [col(0), col(1), col(2), col_prev(1), col_prev(2), _resident(bias.shape)]
    args = [qkv, qkv, qkv, qkv, qkv, bias]
    if prev is not None:
        in_specs += [o_spec, st_spec, st_spec]
        args += list(prev)
    o_shape = jax.ShapeDtypeStruct((b, n_pairs, t, HEAD_PAIR), F32)
    st_shape = jax.ShapeDtypeStruct((b, t, LANES), F32)
    out_specs = [o_spec] if last else [o_spec, st_spec, st_spec]
    out_shape = [o_shape] if last else [o_shape, st_shape, st_shape]
    n_own = len(out_shape)
    for rider in riders:
        r_in, r_args, r_out, r_shape = _rider_specs(rider, grid)
        in_specs += r_in
        args += r_args
        out_specs += r_out
        out_shape += r_shape
    outs = pl.pallas_call(
        functools.partial(_attn_prompt_body, dil=dil, units=units, chained=prev is not None, last=last,
                          rider_dils=tuple(rider["dil"] for rider in riders)),
        grid=grid,
        in_specs=in_specs,
        out_specs=out_specs,
        out_shape=out_shape,
        compiler_params=_params("parallel", "arbitrary", "arbitrary"),
        name=f"attn_prompt_d{dil}",
    )(*args)
    return outs[:n_own], outs[n_own:]


def _bias_table(rel_bias, g, dil):
    onehot = np.eye(N_BUCKETS, dtype=np.float32)[_t5_bucket(np.arange(SPAN + 1) * dil)]
    return jnp.dot(rel_bias[:, g, :].astype(F32).T, jnp.asarray(onehot).T, precision=lax.Precision.HIGHEST)


def _prompt_bias(table):
    nh = table.shape[0]
    width = 3 * BLK - 1
    rev = jnp.concatenate([jnp.broadcast_to(table[:, SPAN:], (nh, BLK - 1)), table[:, ::-1],
                           jnp.broadcast_to(table[:, :1], (nh, BLK - 1)), jnp.zeros((nh, 1), F32)], axis=1)
    skew = jnp.tile(rev, (1, BLK))[:, :BLK * width].reshape(nh, BLK, width)
    bias = skew[:, :, BLK - 1:]
    bias_t = jnp.transpose(bias, (0, 2, 1)).reshape(nh // 2, 2, 2 * BLK, BLK)
    return jnp.transpose(bias_t, (0, 2, 1, 3)).reshape(nh // 2, 2 * BLK, 2 * BLK)


def _sample_attn_heads(q_h, kn, vn_t, kc_ref, vc_ref, bias_c, bias_n, dil):
    hh, n_cache = bias_c.shape
    pos = lax.broadcasted_iota(jnp.int32, (1, n_cache), 1)
    on_grid = pos % dil == 0
    head = lax.broadcasted_iota(jnp.int32, (1, hh), 1)
    q_h = q_h * (HEAD_DIM ** -0.5)
    ln = jnp.sum(kn * q_h, axis=1, keepdims=True) + bias_n
    q_b = q_h.astype(BF16)
    lc = jnp.concatenate([_dot(q_b, kc_ref[h].astype(BF16))[h:h + 1] for h in range(hh)], axis=0)
    lc = jnp.where(on_grid, lc + bias_c, NEG_INF)
    m = jnp.maximum(jnp.max(lc, axis=1, keepdims=True), ln)
    pc = jnp.exp(lc - m)
    pn = jnp.exp(ln - m)
    s = jnp.sum(pc, axis=1, keepdims=True) + pn
    p_b = pc.astype(BF16)
    o_all = jnp.zeros_like(vn_t)
    for h in range(hh):
        o = lax.dot_general(vc_ref[h].astype(BF16), p_b, (((1,), (1,)), ((), ())), preferred_element_type=F32)
        o_all = jnp.where(head == h, (o + pn[h:h + 1, :] * vn_t) / s[h:h + 1, :], o_all)
    return o_all, m + jnp.log(s)


def _sample_rider(q, kn, vn, cache, layer, table, dil, n_split):
    bd, nh, dh = q.shape
    n_cache = cache.shape[2]
    assert n_cache == SPAN * dil and nh % n_split == 0
    hh = nh // n_split
    split = lambda a: a.reshape(bd, n_split, hh, dh)
    bias_c = jnp.repeat(table[:, :0:-1], dil, axis=1).reshape(n_split, hh, n_cache)
    bias_n = table[:, :1].reshape(n_split, hh, 1)
    return dict(arrays=(split(q), split(kn), jnp.swapaxes(split(vn), 2, 3),
                        jnp.transpose(cache, (0, 1, 3, 4, 5, 2)), bias_c, bias_n),
                layer=layer, dil=dil, bd=bd, hh=hh, dh=dh, n_cache=n_cache, n_split=n_split)


def _merge_sample_body(o0_ref, o1_ref, o2_ref, l0_ref, l1_ref, l2_ref, out_ref):
    lses = (l0_ref[...], l1_ref[...], l2_ref[...])
    outs = (o0_ref[...], o1_ref[...], o2_ref[...])
    mx = jnp.maximum(jnp.maximum(lses[0], lses[1]), lses[2])
    es = [jnp.exp(l - mx) for l in lses]
    tot = es[0] + es[1] + es[2]
    out_ref[...] = sum((e / tot) * o for e, o in zip(es, outs))


def _merge_sample(outs, lses):
    return pl.pallas_call(
        _merge_sample_body,
        out_shape=jax.ShapeDtypeStruct(outs[0].shape, F32),
        name="merge_sample",
    )(*outs, *lses)


def _wo_prompt_body(x_ref, o_ref_in, w_ref, g_post_ref, f0, f1, f2, f3, out_ref):
    o = jnp.concatenate([o_ref_in[0, p] for p in range(o_ref_in.shape[1])], axis=1)
    y = _dot(o.astype(BF16), w_ref[...])
    out_ref[0] = _ffn_tile(x_ref[0] + _rms(y, g_post_ref[...]), f0, f1, f2, f3)


def _wo_prompt(x, o, w_o, g_post, ffn, tm):
    b, t, d = x.shape
    row = pl.BlockSpec((1, tm, d), lambda i, j: (i, j, 0))
    return pl.pallas_call(
        _wo_prompt_body,
        grid=(b, t // tm),
        in_specs=[row, pl.BlockSpec((1, o.shape[1], tm, HEAD_PAIR), lambda i, j: (i, 0, j, 0)),
                  _resident(w_o.shape), _resident((1, d))] + _ffn_specs(ffn),
        out_specs=row,
        out_shape=jax.ShapeDtypeStruct((b, t, d), F32),
        compiler_params=_params("parallel", "parallel"),
        name="wo_ffn_prompt",
    )(x, o, w_o, g_post, *ffn[:4])


def _wo_sample_body(x_ref, o_ref_in, w_ref, g_post_ref, out_ref):
    y = _dot(o_ref_in[...].astype(BF16), w_ref[...])
    out_ref[...] = x_ref[...] + _rms(y, g_post_ref[...])


def _wo_sample(x, o, w_o, g_post):
    return pl.pallas_call(
        _wo_sample_body,
        out_shape=jax.ShapeDtypeStruct(x.shape, F32),
        compiler_params=pltpu.CompilerParams(vmem_limit_bytes=VMEM_LIMIT_BYTES),
        name="wo_sample",
    )(x, o, w_o, g_post)


def kernel(x_prompt, x_sample, state_pool, cache_swa_g0, cache_swa_g1, cache_swa_g2, state_conv,
           norm_g, w_ffn_in, w_ffn_out, pool_w, pool_scale, w_qkv, w_o, rel_bias,
           conv_w_in, conv_b_in, conv_w_dw, conv_b_dw, conv_ln_g, conv_ln_b, conv_w_out, conv_b_out):
    b, t, d = x_prompt.shape
    bd, t_dec, _ = x_sample.shape
    assert t_dec == 1 and d % HEAD_PAIR == 0
    nh = d // HEAD_DIM
    depth = norm_g.shape[0]
    caches = (cache_swa_g0, cache_swa_g1, cache_swa_g2)

    tm = ROW_TILE if (b * t) % ROW_TILE == 0 else b * t
    ts = min(ROW_TILE, t)
    tq = min(QKV_ROW_TILE, t)

    xp = x_prompt
    xs = x_sample.reshape(bd, d)
    vec = lambda a: a.reshape(1, -1).astype(F32)

    w_ffn_in_bf16 = w_ffn_in.astype(BF16)
    w_ffn_out_bf16 = w_ffn_out.astype(BF16)
    pool_p, pool_s, conv_p, conv_s = [], [], [], []
    swa_p = [[] for _ in range(N_GROUPS)]
    swa_s = [[] for _ in range(N_GROUPS)]
    for i in range(depth):
        kind, j = i % 3, i // 3
        g = [vec(norm_g[i, k]) for k in range(4)]
        ffn = (g[2], w_ffn_in_bf16, w_ffn_out_bf16, g[3], i)
        if kind == 0:
            pw = pool_w[j].astype(BF16)
            ps = vec(pool_scale[j])
            xp, tail = _pool_prompt(xp, jnp.zeros((b, POOL_HALO, d), F32), g[0], pw, ps, g[1], ffn, ts, 0)
            pool_p.append(tail[:, POOL_HALO - POOL_HIST:])
            hist = state_pool[j]
            xs, hs = _pool_sample(xs, jnp.swapaxes(hist, 0, 1), g[0], pw, ps, g[1], PAST_LEN)
            pool_s.append(jnp.concatenate([hist[:, 1:], hs[:, None, :]], axis=1))
        elif kind == 1:
            wq = w_qkv[j].astype(BF16)
            wo = w_o[j].astype(BF16)
            outs = [_qkv_prompt(xp, g[0], wq, gi, tq) for gi in range(N_GROUPS)]
            tables = [_bias_table(rel_bias, gi, dil) for gi, (_, dil) in enumerate(SWA_CONFIGS)]
            qkv_s = _qkv_sample(xs, g[0], wq).reshape(bd, N_GROUPS, 3, nh, HEAD_DIM)
            order = sorted(range(N_GROUPS), key=lambda gi: -SWA_CONFIGS[gi][1])
            prev = None
            o_s, lse_s = [None] * N_GROUPS, [None] * N_GROUPS
            by_size = sorted(range(N_GROUPS), key=lambda sg: SWA_CONFIGS[sg][1])
            riding = {order[0]: by_size[:-1], order[-1]: by_size[-1:]}
            for k, gi in enumerate(order):
                units = ATTN_UNITS_FIRST if k == 0 else ATTN_UNITS
                steps = b * (t // BLK) // units
                assert steps % bd == 0
                riders = [_sample_rider(qkv_s[:, sg, 0], qkv_s[:, sg, 1], qkv_s[:, sg, 2], caches[sg], j, tables[sg],
                                        SWA_CONFIGS[sg][1], steps // bd) for sg in riding.get(gi, ())]
                prev, rider_outs = _attn_prompt(outs[gi][0], _prompt_bias(tables[gi] * LOG2_E), SWA_CONFIGS[gi][1],
                                                prev, k == N_GROUPS - 1, units, riders)
                for i, sg in enumerate(riding.get(gi, ())):
                    o_s[sg] = jnp.swapaxes(rider_outs[2 * i], 2, 3).reshape(bd, nh, HEAD_DIM)
                    lse_s[sg] = rider_outs[2 * i + 1].reshape(bd, nh, 1)
            for gi in range(N_GROUPS):
                swa_p[gi].append(outs[gi][1].reshape(b, -1, 2, nh, HEAD_DIM))
                swa_s[gi].append(qkv_s[:, gi, 1:3].reshape(bd, 1, 2, nh, HEAD_DIM))
            xp = _wo_prompt(xp, prev[0], wo, g[1], ffn, ts)
            merged = _merge_sample(o_s, lse_s).reshape(bd, d)
            xs = _wo_sample(xs, merged, wo, g[1])
        else:
            cw = (conv_w_in[j].astype(BF16), vec(conv_b_in[j]),
                  jnp.pad(conv_w_dw[j], ((0, CONV_HALO - CONV_WIDTH), (0, 0))), vec(conv_b_dw[j]),
                  vec(conv_ln_g[j]), vec(conv_ln_b[j]), conv_w_out[j].astype(BF16), vec(conv_b_out[j]))
            xp, tail = _conv_prompt(xp, jnp.zeros((b, CONV_HALO, d), F32), g[0], *cw, g[1], ts, min(CONV_ROW_CHUNK, ts))
            conv_p.append(tail[:, CONV_HALO - CONV_HIST:])
            hist = state_conv[j]
            xs, us = _conv_sample(xs, jnp.swapaxes(hist, 0, 1), g[0], *cw, g[1])
            conv_s.append(jnp.concatenate([hist[:, 1:], us[:, None, :]], axis=1))
            xp = _ffn(xp.reshape(b * t, d), ffn, tm).reshape(b, t, d)
        xs = _ffn(xs, ffn, bd)

    return (xp, xs.reshape(bd, 1, d),
            jnp.stack(pool_p), jnp.stack(pool_s),
            jnp.stack(swa_p[0]), jnp.stack(swa_p[1]), jnp.stack(swa_p[2]),
            jnp.stack(swa_s[0]), jnp.stack(swa_s[1]), jnp.stack(swa_s[2]),
            jnp.stack(conv_p), jnp.stack(conv_s))
```

```python
import functools
import math

import numpy as np
import jax
import jax.numpy as jnp
from jax import lax
from jax.experimental import pallas as pl
from jax.experimental.pallas import tpu as pltpu

F32 = jnp.float32
BF16 = jnp.bfloat16

RMS_EPS = 1e-6
LN_EPS = 1e-5
NEG_INF = -1e30
PAST_LEN = 8192
POOL_WINDOWS = (2, 4, 8, 16)
POOL_HIST = max(POOL_WINDOWS) - 1
SWA_CONFIGS = ((128, 1), (512, 4), (2048, 16))
N_GROUPS = len(SWA_CONFIGS)
SPAN = 128
BLK = 128
HEAD_DIM = 64
N_BUCKETS = 32
MAX_DISTANCE = 2048
CONV_WIDTH = 31
CONV_HIST = CONV_WIDTH - 1

LANES = 128
SUBLANES = 8
MXU_DIM = 256
VMEM_LIMIT_BYTES = 56 * 1024 * 1024
REGROUP_STRIDE = 4


POOL_HALO = 16
CONV_HALO = 32
HEAD_PAIR = 2 * HEAD_DIM
LOG2_E = math.log2(math.e)
BF16_SUBLANES = 16
ROW_TILE = 512
QKV_ROW_TILE = 1024
CONV_ROW_CHUNK = 128
ATTN_UNITS = 2
ATTN_UNITS_FIRST = 4
RIDER_INPUTS = 7


def _params(*sem):
    return pltpu.CompilerParams(dimension_semantics=sem, vmem_limit_bytes=VMEM_LIMIT_BYTES)


def _resident(shape):
    nd = len(shape)
    return pl.BlockSpec(shape, lambda *_: (0,) * nd, pipeline_mode=pl.Buffered(1))


def _rms(x, g):
    return x * lax.rsqrt(jnp.mean(x * x, axis=-1, keepdims=True) + RMS_EPS) * g


def _dot(a, b):
    return jnp.dot(a, b, preferred_element_type=F32)


def _t5_bucket(dist):
    max_exact = N_BUCKETS // 2
    d = np.maximum(np.asarray(dist), 0)
    large = max_exact + (np.log(np.maximum(d, max_exact) / max_exact) / math.log(MAX_DISTANCE / max_exact)
                         * (N_BUCKETS - max_exact)).astype(np.int64)
    large = np.minimum(large, N_BUCKETS - 1)
    return np.where(d < max_exact, d, large).astype(np.int32)


def _ff_chunks(d_ff):
    assert d_ff % MXU_DIM == 0
    n = d_ff // MXU_DIM
    first = (n + 1) // 2 * MXU_DIM
    return ((0, first), (first, d_ff - first)) if d_ff > first else ((0, d_ff),)


def _ffn_tile(x, g_pre_ref, w_in_ref, w_out_ref, g_post_ref):
    d_ff = w_out_ref.shape[0]
    h = _rms(x, g_pre_ref[...]).astype(BF16)
    acc = None
    for c0, cw in _ff_chunks(d_ff):
        gate = _dot(h, w_in_ref[:, c0:c0 + cw])
        up = _dot(h, w_in_ref[:, d_ff + c0:d_ff + c0 + cw])
        act = (gate * jax.nn.sigmoid(gate) * up).astype(BF16)
        part = _dot(act, w_out_ref[c0:c0 + cw, :])
        acc = part if acc is None else acc + part
    return x + _rms(acc, g_post_ref[...])


def _layer_resident(shape, layer):
    nd = len(shape)
    return pl.BlockSpec((None,) + tuple(shape[1:]), lambda *_: (layer,) + (0,) * (nd - 1), pipeline_mode=pl.Buffered(1))


def _ffn_specs(ffn):
    g_pre, w_in, w_out, g_post, layer = ffn
    return [_resident(g_pre.shape), _layer_resident(w_in.shape, layer), _layer_resident(w_out.shape, layer),
            _resident(g_post.shape)]


def _ffn_extra_rows(xs_ref, xs_out_ref, ffn_refs, n_axes):
    last = pl.program_id(0) == pl.num_programs(0) - 1
    for axis in range(1, n_axes):
        last = last & (pl.program_id(axis) == pl.num_programs(axis) - 1)

    @pl.when(last)
    def _():
        xs_out_ref[...] = _ffn_tile(xs_ref[...], *ffn_refs)


def _extra_rows_spec(xs):
    return pl.BlockSpec(xs.shape, lambda *_: (0,) * xs.ndim)


def _ffn_body(x_ref, g_pre_ref, w_in_ref, w_out_ref, g_post_ref, xs_ref, o_ref, xs_out_ref):
    o_ref[...] = _ffn_tile(x_ref[...], g_pre_ref, w_in_ref, w_out_ref, g_post_ref)
    _ffn_extra_rows(xs_ref, xs_out_ref, (g_pre_ref, w_in_ref, w_out_ref, g_post_ref), 1)


def _ffn(x, ffn, tm, xs):
    n, d = x.shape
    assert n % tm == 0
    row = pl.BlockSpec((tm, d), lambda i: (i, 0))
    return pl.pallas_call(
        _ffn_body,
        grid=(n // tm,),
        in_specs=[row] + _ffn_specs(ffn) + [_extra_rows_spec(xs)],
        out_specs=[row, _extra_rows_spec(xs)],
        out_shape=[jax.ShapeDtypeStruct((n, d), F32), jax.ShapeDtypeStruct(xs.shape, F32)],
        compiler_params=_params("arbitrary"),
        name="ffn",
    )(x, *ffn[:4], xs)


def _pool_groups(h, window_sum, cnt_of, pw_ref, ps_ref):
    gd = h.shape[-1] // len(POOL_WINDOWS)
    outs = []
    for gi, w in enumerate(POOL_WINDOWS):
        c0 = gi * gd
        hg = h[:, c0:c0 + gd]
        dg = window_sum(gi, w, c0, gd, hg) / cnt_of(w) - hg
        outs.append(_dot(dg.astype(BF16), pw_ref[gi]))
    return jnp.concatenate(outs, axis=1) * ps_ref[...]


def _pool_prompt_body(x_ref, xprev_ref, hist_ref, g_pre_ref, pw_ref, ps_ref, g_post_ref, f0, f1, f2, f3, xs_ref,
                      o_ref, tail_ref, xs_out_ref, buf, *, ts, n_past):
    t = pl.program_id(1)
    x = x_ref[0]
    g_pre = g_pre_ref[...]
    h = _rms(x, g_pre)
    halo = jnp.where(t == 0, hist_ref[0], _rms(xprev_ref[0], g_pre))
    buf[0:POOL_HALO, :] = halo
    buf[POOL_HALO:POOL_HALO + ts, :] = h
    tail_ref[0] = buf[ts:ts + POOL_HALO, :]
    pos1 = n_past + 1 + t * ts + lax.broadcasted_iota(jnp.int32, (ts, 1), 0)

    def window_sum(gi, w, c0, gd, hg):
        assert w & (w - 1) == 0 and w <= POOL_HALO
        acc = buf[:, c0:c0 + gd]
        step = 1
        while step < w:
            acc = acc + pltpu.roll(acc, step, axis=0)
            step *= 2
        return acc[POOL_HALO:, :]

    y = _pool_groups(h, window_sum, lambda w: jnp.minimum(pos1, w).astype(F32), pw_ref, ps_ref)
    o_ref[0] = _ffn_tile(x + _rms(y, g_post_ref[...]), f0, f1, f2, f3)
    _ffn_extra_rows(xs_ref, xs_out_ref, (f0, f1, f2, f3), 2)


def _pool_prompt(x, hist16, g_pre, pw, ps, g_post, ffn, xs, ts, n_past):
    b, t, d = x.shape
    assert t % ts == 0 and ts % POOL_HALO == 0
    per = ts // POOL_HALO
    tile = pl.BlockSpec((1, ts, d), lambda i, j: (i, j, 0))
    halo = pl.BlockSpec((1, POOL_HALO, d), lambda i, j: (i, jnp.maximum(j * per - 1, 0), 0))
    first = pl.BlockSpec((1, POOL_HALO, d), lambda i, j: (i, 0, 0))
    return pl.pallas_call(
        functools.partial(_pool_prompt_body, ts=ts, n_past=n_past),
        grid=(b, t // ts),
        in_specs=[tile, halo, first, _resident((1, d)), _resident(pw.shape), _resident((1, d)), _resident((1, d))]
        + _ffn_specs(ffn) + [_extra_rows_spec(xs)],
        out_specs=[tile, first, _extra_rows_spec(xs)],
        out_shape=[jax.ShapeDtypeStruct((b, t, d), F32), jax.ShapeDtypeStruct((b, POOL_HALO, d), F32),
                   jax.ShapeDtypeStruct(xs.shape, F32)],
        scratch_shapes=[pltpu.VMEM((POOL_HALO + ts, d), F32)],
        compiler_params=_params("arbitrary", "arbitrary"),
        name="pool_ffn_prompt",
    )(x, x, hist16, g_pre, pw, ps, g_post, *ffn[:4], xs)


def _pool_sample_body(x_ref, hist_ref, g_pre_ref, pw_ref, ps_ref, g_post_ref, o_ref, h_ref, *, n_past):
    x = x_ref[...]
    h = _rms(x, g_pre_ref[...])
    h_ref[...] = h

    def window_sum(gi, w, c0, gd, hg):
        win = hg
        for k in range(1, w):
            win = win + hist_ref[POOL_HIST - k, :, c0:c0 + gd]
        return win

    y = _pool_groups(h, window_sum, lambda w: float(min(n_past + 1, w)), pw_ref, ps_ref)
    o_ref[...] = x + _rms(y, g_post_ref[...])


def _pool_sample(x, hist_t, g_pre, pw, ps, g_post, n_past):
    bd, d = x.shape
    return pl.pallas_call(
        functools.partial(_pool_sample_body, n_past=n_past),
        out_shape=[jax.ShapeDtypeStruct((bd, d), F32), jax.ShapeDtypeStruct((bd, d), F32)],
        compiler_params=pltpu.CompilerParams(vmem_limit_bytes=VMEM_LIMIT_BYTES),
        name="pool_sample",
    )(x, hist_t, g_pre, pw, ps, g_post)


def _glu(h, w_in_ref, b_in_ref):
    d = h.shape[-1]
    a = _dot(h, w_in_ref[...]) + b_in_ref[...]
    return a[:, :d] * jax.nn.sigmoid(a[:, d:])


def _conv_tail(c, x, ln_g_ref, ln_b_ref, w_out_ref, b_out_ref, g_post_ref):
    mu = jnp.mean(c, axis=-1, keepdims=True)
    cc = c - mu
    var = jnp.mean(cc * cc, axis=-1, keepdims=True)
    z = cc * lax.rsqrt(var + LN_EPS) * ln_g_ref[...] + ln_b_ref[...]
    y = _dot((z * jax.nn.sigmoid(z)).astype(BF16), w_out_ref[...]) + b_out_ref[...]
    return x + _rms(y, g_post_ref[...])


def _conv_prompt_body(x_ref, hist_ref, g_pre_ref, w_in_ref, b_in_ref, w_dw_ref, b_dw_ref, ln_g_ref, ln_b_ref,
                      w_out_ref, b_out_ref, g_post_ref, o_ref, tail_ref, ubuf, cbuf, *, ts, rc):
    t = pl.program_id(1)
    d = x_ref.shape[-1]

    @pl.when(t == 0)
    def _():
        ubuf[0:CONV_HALO, :] = hist_ref[0]

    @pl.when(t > 0)
    def _():
        ubuf[0:CONV_HALO, :] = ubuf[ts:ts + CONV_HALO, :]

    x = x_ref[0]
    h = _rms(x, g_pre_ref[...]).astype(BF16)
    ubuf[CONV_HALO:CONV_HALO + ts, :] = _glu(h, w_in_ref, b_in_ref)
    tail_ref[0] = ubuf[ts:ts + CONV_HALO, :]

    lead = CONV_HALO - CONV_HIST

    def chunk(i, carry):
        base = pl.multiple_of(i * rc, rc)
        for l0 in range(0, d, LANES):
            win = ubuf[pl.ds(base, rc + CONV_HALO), l0:l0 + LANES]
            acc = jnp.broadcast_to(b_dw_ref[:, l0:l0 + LANES], (rc, LANES))
            for phase in range(SUBLANES):
                offs = [o for o in range(lead, lead + CONV_WIDTH) if o % SUBLANES == phase]
                shifted = win if phase == 0 else pltpu.roll(win, rc + CONV_HALO - phase, axis=0)
                for o in offs:
                    acc = acc + shifted[o - phase:o - phase + rc, :] * w_dw_ref[o - lead:o - lead + 1, l0:l0 + LANES]
            cbuf[pl.ds(base, rc), l0:l0 + LANES] = acc
        return carry

    lax.fori_loop(0, ts // rc, chunk, 0)
    o_ref[0] = _conv_tail(cbuf[...], x, ln_g_ref, ln_b_ref, w_out_ref, b_out_ref, g_post_ref)


def _conv_prompt(x, hist32, g_pre, w_in, b_in, w_dw, b_dw, ln_g, ln_b, w_out, b_out, g_post, ts, rc):
    b, t, d = x.shape
    assert t % ts == 0 and ts % rc == 0 and ts >= CONV_HALO
    tile = pl.BlockSpec((1, ts, d), lambda i, j: (i, j, 0))
    first = pl.BlockSpec((1, CONV_HALO, d), lambda i, j: (i, 0, 0))
    vec = _resident((1, d))
    return pl.pallas_call(
        functools.partial(_conv_prompt_body, ts=ts, rc=rc),
        grid=(b, t // ts),
        in_specs=[tile, first, vec, _resident(w_in.shape), _resident(b_in.shape), _resident(w_dw.shape), vec, vec,
                  vec, _resident(w_out.shape), vec, vec],
        out_specs=[tile, first],
        out_shape=[jax.ShapeDtypeStruct((b, t, d), F32), jax.ShapeDtypeStruct((b, CONV_HALO, d), F32)],
        scratch_shapes=[pltpu.VMEM((CONV_HALO + ts, d), F32), pltpu.VMEM((ts, d), F32)],
        compiler_params=_params("parallel", "arbitrary"),
        name="conv_prompt",
    )(x, hist32, g_pre, w_in, b_in, w_dw, b_dw, ln_g, ln_b, w_out, b_out, g_post)


def _conv_sample_body(x_ref, hist_ref, g_pre_ref, w_in_ref, b_in_ref, w_dw_ref, b_dw_ref, ln_g_ref, ln_b_ref,
                      w_out_ref, b_out_ref, g_post_ref, o_ref, u_ref):
    x = x_ref[...]
    h = _rms(x, g_pre_ref[...]).astype(BF16)
    u = _glu(h, w_in_ref, b_in_ref)
    u_ref[...] = u
    c = b_dw_ref[...] + u * w_dw_ref[CONV_HIST:CONV_WIDTH, :]
    for k in range(CONV_HIST):
        c = c + hist_ref[k] * w_dw_ref[k:k + 1, :]
    o_ref[...] = _conv_tail(c, x, ln_g_ref, ln_b_ref, w_out_ref, b_out_ref, g_post_ref)


def _conv_sample(x, hist_t, g_pre, w_in, b_in, w_dw, b_dw, ln_g, ln_b, w_out, b_out, g_post):
    bd, d = x.shape
    return pl.pallas_call(
        _conv_sample_body,
        out_shape=[jax.ShapeDtypeStruct((bd, d), F32), jax.ShapeDtypeStruct((bd, d), F32)],
        compiler_params=pltpu.CompilerParams(vmem_limit_bytes=VMEM_LIMIT_BYTES),
        name="conv_sample",
    )(x, hist_t, g_pre, w_in, b_in, w_dw, b_dw, ln_g, ln_b, w_out, b_out, g_post)


def _qkv_prompt_body(x_ref, g_pre_ref, w_ref, qkv_ref, kv_ref, *scratch, tm, dil, rows, first):
    t = pl.program_id(1)
    d = x_ref.shape[-1]
    h = _rms(x_ref[0], g_pre_ref[...]).astype(BF16)
    for part in range(3):
        r = _dot(h, w_ref[:, part * d:(part + 1) * d])
        if part == 0:
            r = r * (HEAD_DIM ** -0.5 * LOG2_E)
        if part > 0:
            @pl.when(t >= first)
            def _(r=r, part=part):
                kv_ref[0, :, (part - 1) * d:part * d] = r[tm - rows:, :]
        if dil == 1:
            qkv_ref[0, 0, :, part * d:(part + 1) * d] = r.astype(BF16)
        else:
            rbuf, rbuf2 = scratch
            n_slab = d // LANES
            for s in range(n_slab):
                rbuf[s] = r[:, s * LANES:(s + 1) * LANES]
            outer = max(dil // REGROUP_STRIDE, 1)
            inner = dil // outer
            for a in range(inner):
                if outer == 1:
                    pieces = [[rbuf[s, pl.ds(a, tm // dil, stride=dil), :] for s in range(n_slab)]]
                else:
                    for s in range(n_slab):
                        rbuf2[s] = rbuf[s, pl.ds(a, tm // inner, stride=inner), :]
                    pieces = [[rbuf2[s, pl.ds(c, tm // dil, stride=outer), :] for s in range(n_slab)]
                              for c in range(outer)]
                for c, piece in enumerate(pieces):
                    qkv_ref[0, a + inner * c, :, part * d:(part + 1) * d] = jnp.concatenate(piece, axis=1).astype(BF16)


def _qkv_prompt(x, g_pre, w_qkv, group, tm):
    b, t, d = x.shape
    win, dil = SWA_CONFIGS[group]
    keep = min(win, t)
    rows = min(keep, tm)
    assert t % tm == 0 and keep % rows == 0 and tm % (dil * BF16_SUBLANES) == 0
    first = t // tm - (keep // rows if keep >= tm else 1)
    tile = pl.BlockSpec((1, tm, d), lambda i, j: (i, j, 0))
    w_spec = pl.BlockSpec((d, 3 * d), lambda i, j: (0, group), pipeline_mode=pl.Buffered(1))
    return pl.pallas_call(
        functools.partial(_qkv_prompt_body, tm=tm, dil=dil, rows=rows, first=first),
        grid=(b, t // tm),
        in_specs=[tile, _resident((1, d)), w_spec],
        out_specs=[pl.BlockSpec((1, dil, tm // dil, 3 * d), lambda i, j: (i, 0, j, 0)),
                   pl.BlockSpec((1, rows, 2 * d), lambda i, j: (i, jnp.maximum(j - first, 0), 0))],
        out_shape=[jax.ShapeDtypeStruct((b, dil, t // dil, 3 * d), BF16), jax.ShapeDtypeStruct((b, keep, 2 * d), F32)],
        scratch_shapes=[pltpu.VMEM((d // LANES, tm, LANES), F32),
                        pltpu.VMEM((d // LANES, tm // min(dil, REGROUP_STRIDE), LANES), F32)] if dil > 1 else [],
        compiler_params=_params("parallel", "arbitrary"),
        name=f"qkv_prompt_d{dil}",
    )(x, g_pre, w_qkv)


def _qkv_sample_body(x_ref, g_pre_ref, w_ref, o_ref):
    h = _rms(x_ref[...], g_pre_ref[...]).astype(BF16)
    o_ref[...] = _dot(h, w_ref[...])


def _qkv_sample(x, g_pre, w_qkv):
    bd, _ = x.shape
    return pl.pallas_call(
        _qkv_sample_body,
        out_shape=jax.ShapeDtypeStruct((bd, w_qkv.shape[1]), F32),
        compiler_params=pltpu.CompilerParams(vmem_limit_bytes=VMEM_LIMIT_BYTES),
        name="qkv_sample",
    )(x, g_pre, w_qkv)


def _attn_prompt_body(*refs, dil, units, chained, last, rider_dils):
    refs = list(refs)
    q_ref, kc_ref, vc_ref, kp_ref, vp_ref, bias_ref = refs[:6]
    del refs[:6]
    if chained:
        op_ref, mp_ref, sp_ref = refs[:3]
        del refs[:3]
    rider_in = [refs[RIDER_INPUTS * i:RIDER_INPUTS * (i + 1)] for i in range(len(rider_dils))]
    del refs[:RIDER_INPUTS * len(rider_dils)]
    o_ref = refs[0]
    del refs[:1]
    if not last:
        m_ref, s_ref = refs[:2]
        del refs[:2]
    n = pl.program_id(1)
    res = pl.program_id(2)

    if rider_dils:
        step = (pl.program_id(0) * pl.num_programs(1) + n) * pl.num_programs(2) + res

        @pl.when(step >= 0)
        def _():
            for i, rider_dil in enumerate(rider_dils):
                qh_r, kn_r, vn_r, kcache_r, vcache_r, bias_c_r, bias_n_r = rider_in[i]
                ro_ref, rl_ref = refs[2 * i:2 * i + 2]
                half = step % bias_c_r.shape[0]
                ro_ref[0, 0], rl_ref[0, 0] = _sample_attn_heads(
                    qh_r[0, 0], kn_r[0, 0], vn_r[0, 0], kcache_r.at[0], vcache_r.at[0],
                    bias_c_r[half], bias_n_r[half], rider_dil)

    kj = lax.broadcasted_iota(jnp.int32, (2 * BLK, 2 * BLK), 0)
    qi = lax.broadcasted_iota(jnp.int32, (2 * BLK, 2 * BLK), 1) % BLK
    delta = qi + BLK - kj
    band = (delta >= 0) & (delta <= SPAN)
    first_key = jnp.where(n > 0, 0, BLK)
    band_first = band & (kj >= first_key)
    lane = lax.broadcasted_iota(jnp.int32, (BLK, HEAD_PAIR), 1)
    low = lane < HEAD_DIM
    n_pairs = q_ref.shape[-1] // HEAD_PAIR
    head_row = lax.broadcasted_iota(jnp.int32, (2 * n_pairs, BLK), 0)

    def unit_operands(u):
        if dil > 1:
            cat = lambda prev_ref, cur_ref: jnp.concatenate([prev_ref[0, u], cur_ref[0, u]], axis=0)
            return (q_ref[0, u], cat(kp_ref, kc_ref), cat(vp_ref, vc_ref), band_first,
                    pl.ds(res * units + u, BLK, stride=dil))
        own = slice(u * BLK, (u + 1) * BLK)
        if u == 0:
            cat = lambda prev_ref, cur_ref: jnp.concatenate([prev_ref[0, 0], cur_ref[0, 0, own]], axis=0)
            return q_ref[0, 0, own], cat(kp_ref, kc_ref), cat(vp_ref, vc_ref), band_first, own
        both = slice((u - 1) * BLK, (u + 1) * BLK)
        return q_ref[0, 0, own], kc_ref[0, 0, both], vc_ref[0, 0, both], band, own

    def pair_rows(stat_t):
        return jnp.stack([jnp.concatenate([stat_t[2 * p:2 * p + 1], stat_t[2 * p + 1:2 * p + 2]], axis=1)
                          for p in range(n_pairs)], axis=0)

    ops = [unit_operands(u) for u in range(units)]
    items = [(u, pair) for u in range(units) for pair in range(n_pairs)]
    st = []
    for u, pair in items:
        q, k, _, valid, _ = ops[u]
        sl = slice(pair * HEAD_PAIR, (pair + 1) * HEAD_PAIR)
        q2 = q[:, sl]
        qq = jnp.concatenate([jnp.where(low, q2, jnp.zeros_like(q2)), jnp.where(low, jnp.zeros_like(q2), q2)], axis=0)
        score = lax.dot_general(k[:, sl], qq, (((1,), (1,)), ((), ())), preferred_element_type=F32)
        st.append(jnp.where(valid, score + bias_ref[pair], NEG_INF))
    m_new = [jnp.max(x, axis=0, keepdims=True) for x in st]
    if chained:
        m_prev, s_prev = [], []
        for u in range(units):
            rows = ops[u][4]
            m_prev += list(pair_rows(mp_ref[0, rows, :].T))
            s_prev += list(pair_rows(sp_ref[0, rows, :].T))
        m_new = [jnp.maximum(a, b) for a, b in zip(m_new, m_prev)]
    p = [jnp.exp2(x - m) for x, m in zip(st, m_new)]
    s_new = [jnp.sum(x, axis=0, keepdims=True) for x in p]
    if chained:
        alpha = [jnp.exp2(a - b) for a, b in zip(m_prev, m_new)]
        s_new = [sp * a + sn for sp, a, sn in zip(s_prev, alpha, s_new)]
    p = [x.astype(BF16) for x in p]
    for u, (_, _, v, _, rows) in enumerate(ops):
        m_out_t = jnp.zeros((2 * n_pairs, BLK), F32)
        s_out_t = jnp.zeros((2 * n_pairs, BLK), F32)
        for pair in range(n_pairs):
            i = u * n_pairs + pair
            sl = slice(pair * HEAD_PAIR, (pair + 1) * HEAD_PAIR)
            pv = lax.dot_general(v[:, sl], p[i], (((0,), (0,)), ((), ())), preferred_element_type=F32)
            o_t = [pv[:HEAD_DIM, :BLK], pv[HEAD_DIM:, BLK:]]
            if chained:
                o_prev_t = op_ref[0, pair, rows, :].T
            for half in range(2):
                cols = slice(half * BLK, (half + 1) * BLK)
                if chained:
                    o_t[half] = o_prev_t[half * HEAD_DIM:(half + 1) * HEAD_DIM] * alpha[i][:, cols] + o_t[half]
                if last:
                    o_t[half] = o_t[half] / s_new[i][:, cols]
                else:
                    m_out_t = jnp.where(head_row == 2 * pair + half, m_new[i][:, cols], m_out_t)
                    s_out_t = jnp.where(head_row == 2 * pair + half, s_new[i][:, cols], s_out_t)
            o_ref[0, pair, rows, :] = jnp.concatenate(o_t, axis=0).T
        if not last:
            pad = jnp.zeros((BLK - 2 * n_pairs, BLK), F32)
            m_ref[0, rows, :] = jnp.concatenate([m_out_t, pad], axis=0).T
            s_ref[0, rows, :] = jnp.concatenate([s_out_t, pad], axis=0).T


def _rider_specs(rider, grid):
    bd, hh, dh, n_cache, layer, n_split = (rider[k] for k in ("bd", "hh", "dh", "n_cache", "layer", "n_split"))
    assert grid[0] * grid[1] * grid[2] == bd * n_split

    def item(i, n, r):
        step = (i * grid[1] + n) * grid[2] + r
        return step // n_split, step % n_split

    small = lambda rows, cols: pl.BlockSpec((1, 1, rows, cols), lambda i, n, r: (*item(i, n, r), 0, 0))

    def kv(which):
        def index(i, n, r):
            seq, sub = item(i, n, r)
            return layer, seq, which, sub, 0, 0
        return pl.BlockSpec((None, 1, None, hh, dh, n_cache), index)

    qh, kn, vn, cache_t, bias_c, bias_n = rider["arrays"]
    in_specs = [small(hh, dh), small(hh, dh), small(dh, hh), kv(0), kv(1),
                _resident(bias_c.shape), _resident(bias_n.shape)]
    assert len(in_specs) == RIDER_INPUTS
    out_shape = [jax.ShapeDtypeStruct((bd, n_split, dh, hh), F32), jax.ShapeDtypeStruct((bd, n_split, hh, 1), F32)]
    return in_specs, [qh, kn, vn, cache_t, cache_t, bias_c, bias_n], [small(dh, hh), small(hh, 1)], out_shape


def _attn_prompt(qkv, bias, dil, prev, last, units, riders=()):
    b, _, sub, d3 = qkv.shape
    d = d3 // 3
    t = sub * dil
    n_pairs = d // HEAD_PAIR
    assert sub % BLK == 0 and d % HEAD_PAIR == 0 and d // HEAD_DIM <= LANES
    if dil > 1:
        assert dil % units == 0
        grid = (b, sub // BLK, dil // units)
        span = BLK * dil
        blk = (1, units, BLK, d)
        col = lambda part: pl.BlockSpec(blk, lambda i, n, r: (i, r, n, part))
        col_prev = lambda part: pl.BlockSpec(blk, lambda i, n, r: (i, r, jnp.maximum(n - 1, 0), part))
    else:
        assert (sub // BLK) % units == 0
        grid = (b, sub // (BLK * units), 1)
        span = BLK * units
        col = lambda part: pl.BlockSpec((1, 1, span, d), lambda i, n, r: (i, 0, n, part))
        col_prev = lambda part: pl.BlockSpec((1, 1, BLK, d), lambda i, n, r: (i, 0, jnp.maximum(n * units - 1, 0), part))

    o_spec = pl.BlockSpec((1, n_pairs, span, HEAD_PAIR), lambda i, n, r: (i, 0, n, 0))
    st_spec = pl.BlockSpec((1, span, LANES), lambda i, n, r: (i, n, 0))
    in_specs = [col(0), col(1), col(2), col_prev(1), col_prev(2), _resident(bias.shape)]
    args = [qkv, qkv, qkv, qkv, qkv, bias]
    if prev is not None:
        in_specs += [o_spec, st_spec, st_spec]
        args += list(prev)
    o_shape = jax.ShapeDtypeStruct((b, n_pairs, t, HEAD_PAIR), F32)
    st_shape = jax.ShapeDtypeStruct((b, t, LANES), F32)
    out_specs = [o_spec] if last else [o_spec, st_spec, st_spec]
    out_shape = [o_shape] if last else [o_shape, st_shape, st_shape]
    n_own = len(out_shape)
    for rider in riders:
        r_in, r_args, r_out, r_shape = _rider_specs(rider, grid)
        in_specs += r_in
        args += r_args
        out_specs += r_out
        out_shape += r_shape
    outs = pl.pallas_call(
        functools.partial(_attn_prompt_body, dil=dil, units=units, chained=prev is not None, last=last,
                          rider_dils=tuple(rider["dil"] for rider in riders)),
        grid=grid,
        in_specs=in_specs,
        out_specs=out_specs,
        out_shape=out_shape,
        compiler_params=_params("parallel", "arbitrary", "arbitrary"),
        name=f"attn_prompt_d{dil}",
    )(*args)
    return outs[:n_own], outs[n_own:]


def _bias_table(rel_bias, g, dil):
    onehot = np.eye(N_BUCKETS, dtype=np.float32)[_t5_bucket(np.arange(SPAN + 1) * dil)]
    return jnp.dot(rel_bias[:, g, :].astype(F32).T, jnp.asarray(onehot).T, precision=lax.Precision.HIGHEST)


def _prompt_bias(table):
    nh = table.shape[0]
    width = 3 * BLK - 1
    rev = jnp.concatenate([jnp.broadcast_to(table[:, SPAN:], (nh, BLK - 1)), table[:, ::-1],
                           jnp.broadcast_to(table[:, :1], (nh, BLK - 1)), jnp.zeros((nh, 1), F32)], axis=1)
    skew = jnp.tile(rev, (1, BLK))[:, :BLK * width].reshape(nh, BLK, width)
    bias = skew[:, :, BLK - 1:]
    bias_t = jnp.transpose(bias, (0, 2, 1)).reshape(nh // 2, 2, 2 * BLK, BLK)
    return jnp.transpose(bias_t, (0, 2, 1, 3)).reshape(nh // 2, 2 * BLK, 2 * BLK)


def _sample_attn_heads(q_h, kn, vn_t, kc_ref, vc_ref, bias_c, bias_n, dil):
    hh, n_cache = bias_c.shape
    pos = lax.broadcasted_iota(jnp.int32, (1, n_cache), 1)
    on_grid = pos % dil == 0
    head = lax.broadcasted_iota(jnp.int32, (1, hh), 1)
    q_h = q_h * (HEAD_DIM ** -0.5)
    ln = jnp.sum(kn * q_h, axis=1, keepdims=True) + bias_n
    q_b = q_h.astype(BF16)
    lc = jnp.concatenate([_dot(q_b, kc_ref[h].astype(BF16))[h:h + 1] for h in range(hh)], axis=0)
    lc = jnp.where(on_grid, lc + bias_c, NEG_INF)
    m = jnp.maximum(jnp.max(lc, axis=1, keepdims=True), ln)
    pc = jnp.exp(lc - m)
    pn = jnp.exp(ln - m)
    s = jnp.sum(pc, axis=1, keepdims=True) + pn
    p_b = pc.astype(BF16)
    o_all = jnp.zeros_like(vn_t)
    for h in range(hh):
        o = lax.dot_general(vc_ref[h].astype(BF16), p_b, (((1,), (1,)), ((), ())), preferred_element_type=F32)
        o_all = jnp.where(head == h, (o + pn[h:h + 1, :] * vn_t) / s[h:h + 1, :], o_all)
    return o_all, m + jnp.log(s)


def _sample_rider(q, kn, vn, cache, layer, table, dil, n_split):
    bd, nh, dh = q.shape
    n_cache = cache.shape[2]
    assert n_cache == SPAN * dil and nh % n_split == 0
    hh = nh // n_split
    split = lambda a: a.reshape(bd, n_split, hh, dh)
    bias_c = jnp.repeat(table[:, :0:-1], dil, axis=1).reshape(n_split, hh, n_cache)
    bias_n = table[:, :1].reshape(n_split, hh, 1)
    return dict(arrays=(split(q), split(kn), jnp.swapaxes(split(vn), 2, 3),
                        jnp.transpose(cache, (0, 1, 3, 4, 5, 2)), bias_c, bias_n),
                layer=layer, dil=dil, bd=bd, hh=hh, dh=dh, n_cache=n_cache, n_split=n_split)


def _merge_sample_body(o0_ref, o1_ref, o2_ref, l0_ref, l1_ref, l2_ref, out_ref):
    lses = (l0_ref[...], l1_ref[...], l2_ref[...])
    outs = (o0_ref[...], o1_ref[...], o2_ref[...])
    mx = jnp.maximum(jnp.maximum(lses[0], lses[1]), lses[2])
    es = [jnp.exp(l - mx) for l in lses]
    tot = es[0] + es[1] + es[2]
    out_ref[...] = sum((e / tot) * o for e, o in zip(es, outs))


def _merge_sample(outs, lses):
    return pl.pallas_call(
        _merge_sample_body,
        out_shape=jax.ShapeDtypeStruct(outs[0].shape, F32),
        name="merge_sample",
    )(*outs, *lses)


def _wo_prompt_body(x_ref, o_ref_in, w_ref, g_post_ref, f0, f1, f2, f3, xs_ref, out_ref, xs_out_ref):
    o = jnp.concatenate([o_ref_in[0, p] for p in range(o_ref_in.shape[1])], axis=1)
    y = _dot(o.astype(BF16), w_ref[...])
    out_ref[0] = _ffn_tile(x_ref[0] + _rms(y, g_post_ref[...]), f0, f1, f2, f3)
    _ffn_extra_rows(xs_ref, xs_out_ref, (f0, f1, f2, f3), 2)


def _wo_prompt(x, o, w_o, g_post, ffn, xs, tm):
    b, t, d = x.shape
    row = pl.BlockSpec((1, tm, d), lambda i, j: (i, j, 0))
    return pl.pallas_call(
        _wo_prompt_body,
        grid=(b, t // tm),
        in_specs=[row, pl.BlockSpec((1, o.shape[1], tm, HEAD_PAIR), lambda i, j: (i, 0, j, 0)),
                  _resident(w_o.shape), _resident((1, d))] + _ffn_specs(ffn) + [_extra_rows_spec(xs)],
        out_specs=[row, _extra_rows_spec(xs)],
        out_shape=[jax.ShapeDtypeStruct((b, t, d), F32), jax.ShapeDtypeStruct(xs.shape, F32)],
        compiler_params=_params("arbitrary", "arbitrary"),
        name="wo_ffn_prompt",
    )(x, o, w_o, g_post, *ffn[:4], xs)


def _wo_sample_body(x_ref, o_ref_in, w_ref, g_post_ref, out_ref):
    y = _dot(o_ref_in[...].astype(BF16), w_ref[...])
    out_ref[...] = x_ref[...] + _rms(y, g_post_ref[...])


def _wo_sample(x, o, w_o, g_post):
    return pl.pallas_call(
        _wo_sample_body,
        out_shape=jax.ShapeDtypeStruct(x.shape, F32),
        compiler_params=pltpu.CompilerParams(vmem_limit_bytes=VMEM_LIMIT_BYTES),
        name="wo_sample",
    )(x, o, w_o, g_post)


def kernel(x_prompt, x_sample, state_pool, cache_swa_g0, cache_swa_g1, cache_swa_g2, state_conv,
           norm_g, w_ffn_in, w_ffn_out, pool_w, pool_scale, w_qkv, w_o, rel_bias,
           conv_w_in, conv_b_in, conv_w_dw, conv_b_dw, conv_ln_g, conv_ln_b, conv_w_out, conv_b_out):
    b, t, d = x_prompt.shape
    bd, t_dec, _ = x_sample.shape
    assert t_dec == 1 and d % HEAD_PAIR == 0
    nh = d // HEAD_DIM
    depth = norm_g.shape[0]
    caches = (cache_swa_g0, cache_swa_g1, cache_swa_g2)

    tm = ROW_TILE if (b * t) % ROW_TILE == 0 else b * t
    ts = min(ROW_TILE, t)
    tq = min(QKV_ROW_TILE, t)

    xp = x_prompt
    xs = x_sample.reshape(bd, d)
    vec = lambda a: a.reshape(1, -1).astype(F32)

    w_ffn_in_bf16 = w_ffn_in.astype(BF16)
    w_ffn_out_bf16 = w_ffn_out.astype(BF16)
    pool_p, pool_s, conv_p, conv_s = [], [], [], []
    swa_p = [[] for _ in range(N_GROUPS)]
    swa_s = [[] for _ in range(N_GROUPS)]
    for i in range(depth):
        kind, j = i % 3, i // 3
        g = [vec(norm_g[i, k]) for k in range(4)]
        ffn = (g[2], w_ffn_in_bf16, w_ffn_out_bf16, g[3], i)
        if kind == 0:
            pw = pool_w[j].astype(BF16)
            ps = vec(pool_scale[j])
            hist = state_pool[j]
            xs, hs = _pool_sample(xs, jnp.swapaxes(hist, 0, 1), g[0], pw, ps, g[1], PAST_LEN)
            pool_s.append(jnp.concatenate([hist[:, 1:], hs[:, None, :]], axis=1))
            xp, tail, xs = _pool_prompt(xp, jnp.zeros((b, POOL_HALO, d), F32), g[0], pw, ps, g[1], ffn, xs, ts, 0)
            pool_p.append(tail[:, POOL_HALO - POOL_HIST:])
        elif kind == 1:
            wq = w_qkv[j].astype(BF16)
            wo = w_o[j].astype(BF16)
            outs = [_qkv_prompt(xp, g[0], wq, gi, tq) for gi in range(N_GROUPS)]
            tables = [_bias_table(rel_bias, gi, dil) for gi, (_, dil) in enumerate(SWA_CONFIGS)]
            qkv_s = _qkv_sample(xs, g[0], wq).reshape(bd, N_GROUPS, 3, nh, HEAD_DIM)
            order = sorted(range(N_GROUPS), key=lambda gi: -SWA_CONFIGS[gi][1])
            prev = None
            o_s, lse_s = [None] * N_GROUPS, [None] * N_GROUPS
            by_size = sorted(range(N_GROUPS), key=lambda sg: SWA_CONFIGS[sg][1])
            riding = {order[0]: by_size[:-1], order[-1]: by_size[-1:]}
            for k, gi in enumerate(order):
                units = ATTN_UNITS_FIRST if k == 0 else ATTN_UNITS
                steps = b * (t // BLK) // units
                assert steps % bd == 0
                riders = [_sample_rider(qkv_s[:, sg, 0], qkv_s[:, sg, 1], qkv_s[:, sg, 2], caches[sg], j, tables[sg],
                                        SWA_CONFIGS[sg][1], steps // bd) for sg in riding.get(gi, ())]
                prev, rider_outs = _attn_prompt(outs[gi][0], _prompt_bias(tables[gi] * LOG2_E), SWA_CONFIGS[gi][1],
                                                prev, k == N_GROUPS - 1, units, riders)
                for i, sg in enumerate(riding.get(gi, ())):
                    o_s[sg] = jnp.swapaxes(rider_outs[2 * i], 2, 3).reshape(bd, nh, HEAD_DIM)
                    lse_s[sg] = rider_outs[2 * i + 1].reshape(bd, nh, 1)
            for gi in range(N_GROUPS):
                swa_p[gi].append(outs[gi][1].reshape(b, -1, 2, nh, HEAD_DIM))
                swa_s[gi].append(qkv_s[:, gi, 1:3].reshape(bd, 1, 2, nh, HEAD_DIM))
            merged = _merge_sample(o_s, lse_s).reshape(bd, d)
            xs = _wo_sample(xs, merged, wo, g[1])
            xp, xs = _wo_prompt(xp, prev[0], wo, g[1], ffn, xs, ts)
        else:
            cw = (conv_w_in[j].astype(BF16), vec(conv_b_in[j]),
                  jnp.pad(conv_w_dw[j], ((0, CONV_HALO - CONV_WIDTH), (0, 0))), vec(conv_b_dw[j]),
                  vec(conv_ln_g[j]), vec(conv_ln_b[j]), conv_w_out[j].astype(BF16), vec(conv_b_out[j]))
            xp, tail = _conv_prompt(xp, jnp.zeros((b, CONV_HALO, d), F32), g[0], *cw, g[1], ts, min(CONV_ROW_CHUNK, ts))
            conv_p.append(tail[:, CONV_HALO - CONV_HIST:])
            hist = state_conv[j]
            xs, us = _conv_sample(xs, jnp.swapaxes(hist, 0, 1), g[0], *cw, g[1])
            conv_s.append(jnp.concatenate([hist[:, 1:], us[:, None, :]], axis=1))
            xp, xs = _ffn(xp.reshape(b * t, d), ffn, tm, xs)
            xp = xp.reshape(b, t, d)

    return (xp, xs.reshape(bd, 1, d),
            jnp.stack(pool_p), jnp.stack(pool_s),
            jnp.stack(swa_p[0]), jnp.stack(swa_p[1]), jnp.stack(swa_p[2]),
            jnp.stack(swa_s[0]), jnp.stack(swa_s[1]), jnp.stack(swa_s[2]),
            jnp.stack(conv_p), jnp.stack(conv_s))
```
